```python
import math, functools
import jax, jax.numpy as jnp
from jax import lax
import numpy as np

D_MODEL = 2048
BATCH = 2
SEQ = 4096
DEPTH = 1
DEC_BATCH = 128
DEC_SEQ = 1
PAST_LEN = 2048
PAGE_SIZE = 128

H_ATT = 8
DK_ATT = 64
DV_ATT = 128
H_RNN = 8
DK_RNN = 128
DV_RNN = 128
D_ATT = H_ATT * DV_ATT
D_RNN = H_RNN * DV_RNN
D_MIX = D_ATT + D_RNN
QK_ATT_W = H_ATT * 2 * DK_ATT
RNN_K_W = H_RNN * DK_RNN
IN_WIDTHS = (QK_ATT_W, QK_ATT_W, D_ATT, RNN_K_W, RNN_K_W, D_RNN, D_RNN)
D_IN = QK_ATT_W * 2 + D_ATT + RNN_K_W * 2 + D_RNN * 2
D_FF = 5632
CONV_W = 3
D_PLE = 256
ROPE_THETA = 10000.0
EPS = 1e-6
Q_BLOCK = 128
RNN_CHUNK = 16
NEG_INF = -1e30

kernel_name = "hymba_diffattn_hgrn2_convffn_step"


def rms_norm(x, gain):
    xf = x.astype(jnp.float32)
    y = xf * lax.rsqrt(jnp.mean(xf * xf, axis=-1, keepdims=True) + EPS)
    return (y * gain.astype(jnp.float32)).astype(x.dtype)


def rotary(x, pos):
    half = DK_ATT // 2
    inv_freq = 1.0 / (ROPE_THETA ** (jnp.arange(half, dtype=jnp.float32) * (2.0 / DK_ATT)))
    ang = pos.astype(jnp.float32)[:, None] * inv_freq[None, :]
    cos = jnp.cos(ang)[:, None, None, :]
    sin = jnp.sin(ang)[:, None, None, :]
    xf = x.astype(jnp.float32)
    x1, x2 = xf[..., :half], xf[..., half:]
    return jnp.concatenate([x1 * cos - x2 * sin, x2 * cos + x1 * sin], axis=-1).astype(x.dtype)


def diff_weights(s, lam):
    p = jax.nn.softmax(s, axis=-1)
    return p[:, :, 0] - lam * p[:, :, 1]


def attend_prompt(q, k, v, lam):
    B, S = q.shape[0], q.shape[1]
    nb = S // Q_BLOCK
    scale = DK_ATT ** -0.5
    qb = q.reshape(B, nb, Q_BLOCK, H_ATT, 2, DK_ATT).transpose(1, 0, 2, 3, 4, 5)
    kpos = jnp.arange(S)

    def block(args):
        qi, bi = args
        s = jnp.einsum('bqhmd,bkhmd->bhmqk', qi, k, preferred_element_type=jnp.float32) * scale
        qpos = bi * Q_BLOCK + jnp.arange(Q_BLOCK)
        s = jnp.where(kpos[None, :] <= qpos[:, None], s, NEG_INF)
        w = diff_weights(s, lam)
        return jnp.einsum('bhqk,bkhd->bqhd', w, v).astype(q.dtype)

    o = lax.map(block, (qb, jnp.arange(nb)))
    return o.transpose(1, 0, 2, 3, 4).reshape(B, S, H_ATT, DV_ATT)


def attend_sample(q, k_new, v_new, lam, k_past, v_past):
    T, P = q.shape[1], k_past.shape[1]
    scale = DK_ATT ** -0.5
    s_past = jnp.einsum('bqhmd,bkhmd->bhmqk', q, k_past, preferred_element_type=jnp.float32) * scale
    s_new = jnp.einsum('bqhmd,bkhmd->bhmqk', q, k_new, preferred_element_type=jnp.float32) * scale
    causal = jnp.tril(jnp.ones((T, T), dtype=bool))
    s_new = jnp.where(causal, s_new, NEG_INF)
    w = diff_weights(jnp.concatenate([s_past, s_new], axis=-1), lam)
    o = (jnp.einsum('bhqk,bkhd->bqhd', w[..., :P], v_past)
         + jnp.einsum('bhqk,bkhd->bqhd', w[..., P:], v_new))
    return o.astype(q.dtype)


def hgrn2_recurrence(q, k, v, log_f, s0):
    B, L = q.shape[0], q.shape[1]
    C = math.gcd(L, RNN_CHUNK)
    n = L // C

    def to_chunks(a):
        return a.astype(jnp.float32).reshape(B, n, C, H_RNN, a.shape[-1]).transpose(1, 0, 3, 2, 4)

    causal = jnp.tril(jnp.ones((C, C), dtype=bool))

    def step(S, inp):
        qc, kc, vc, gc = inp
        b = jnp.cumsum(gc, axis=-2)
        q_dec = qc * jnp.exp(b)
        a = jnp.einsum('bhtk,bhsk->bhts', q_dec, kc * jnp.exp(-b))
        a = jnp.where(causal, a, 0.0)
        o = jnp.einsum('bhts,bhsv->bhtv', a, vc) + jnp.einsum('bhtk,bhkv->bhtv', q_dec, S)
        b_last = b[:, :, -1:, :]
        S_new = (jnp.exp(b_last[:, :, 0, :])[..., None] * S
                 + jnp.einsum('bhsk,bhsv->bhkv', kc * jnp.exp(b_last - b), vc))
        return S_new, o

    S_fin, o = lax.scan(step, s0.astype(jnp.float32), (to_chunks(q), to_chunks(k), to_chunks(v), to_chunks(log_f)))
    o = o.transpose(1, 0, 3, 2, 4).reshape(B, L, H_RNN, DV_RNN)
    return o, S_fin


def conv_ffn(h, buf, w_gate, w_up, conv_w, conv_b, w_down):
    L = h.shape[1]
    g = h @ w_gate
    ext = jnp.concatenate([buf.astype(g.dtype), g], axis=1)
    c = conv_b
    for j in range(CONV_W):
        c = c + ext[:, j:j + L] * conv_w[j]
    out = (jax.nn.silu(c) * (h @ w_up)) @ w_down
    return out, ext[:, L:]


def layer_forward(x, p, pos, attend, rnn_state0, conv_buf0, lam_init, lb, wts):
    (norm_mix, w_in, q_gain, k_gain, lam_q1, lam_k1, lam_q2, lam_k2, att_gain, rnn_gain,
     w_out, norm_ffn, w_gate, w_up, conv_w, conv_b, w_down, norm_ple, w_ple_gate, w_ple_proj) = wts
    B, L = x.shape[0], x.shape[1]
    h = rms_norm(x, norm_mix)
    z = h @ w_in
    splits = [int(s) for s in np.cumsum(IN_WIDTHS)[:-1]]
    q_a, k_a, v_a, q_r, f_r, i_r, g_r = jnp.split(z, splits, axis=-1)
    q_a = rotary(rms_norm(q_a.reshape(B, L, H_ATT, 2, DK_ATT), q_gain), pos)
    k_a = rotary(rms_norm(k_a.reshape(B, L, H_ATT, 2, DK_ATT), k_gain), pos)
    v_a = v_a.reshape(B, L, H_ATT, DV_ATT)
    lam = (jnp.exp(jnp.sum(lam_q1.astype(jnp.float32) * lam_k1.astype(jnp.float32)))
           - jnp.exp(jnp.sum(lam_q2.astype(jnp.float32) * lam_k2.astype(jnp.float32))) + lam_init)
    o_att = attend(q_a, k_a, v_a, lam)
    o_att = rms_norm(o_att, att_gain) * (1.0 - lam_init)
    lbf = lb.reshape(H_RNN, DK_RNN)
    zf = f_r.astype(jnp.float32).reshape(B, L, H_RNN, DK_RNN)
    log_f = jnp.log(lbf + (1.0 - lbf) * jax.nn.sigmoid(zf))
    k_r = (1.0 - lbf) * jax.nn.sigmoid(-zf)
    o_rnn, rnn_state = hgrn2_recurrence(q_r.reshape(B, L, H_RNN, DK_RNN), k_r,
                                        i_r.reshape(B, L, H_RNN, DV_RNN), log_f, rnn_state0)
    o_rnn = rms_norm(o_rnn.astype(x.dtype), rnn_gain) * jax.nn.silu(g_r.reshape(B, L, H_RNN, DV_RNN))
    x = x + jnp.concatenate([o_att.reshape(B, L, D_ATT), o_rnn.reshape(B, L, D_RNN)], axis=-1) @ w_out
    ffn, conv_buf = conv_ffn(rms_norm(x, norm_ffn), conv_buf0, w_gate, w_up, conv_w, conv_b, w_down)
    x = x + ffn
    x = x + jax.nn.sigmoid(rms_norm(x, norm_ple) @ w_ple_gate) * (p @ w_ple_proj)
    return x, k_a.reshape(B, L, H_ATT, 2 * DK_ATT), v_a, rnn_state, conv_buf


def setup_inputs(seed: int = 0) -> dict:
    key = jax.random.key(seed)
    ks = jax.random.split(key, 40)
    n_pages = PAST_LEN // PAGE_SIZE
    n_used = DEC_BATCH * n_pages
    n_pool = (5 * n_used + 3) // 4
    f32 = jnp.float32

    def nrm(k, shape, scale=1.0):
        return jax.random.normal(k, shape, f32) * scale

    def gain(k, shape):
        return 1.0 + 0.02 * jax.random.normal(k, shape, f32)

    page_table = jax.random.permutation(ks[0], n_pool)[:n_used].reshape(DEC_BATCH, n_pages).astype(jnp.int32)
    return {
        "x_prompt": nrm(ks[1], (BATCH, SEQ, D_MODEL)),
        "x_sample": nrm(ks[2], (DEC_BATCH, DEC_SEQ, D_MODEL)),
        "p_prompt": nrm(ks[3], (DEPTH, BATCH, SEQ, D_PLE)),
        "p_sample": nrm(ks[4], (DEPTH, DEC_BATCH, DEC_SEQ, D_PLE)),
        "cache_k": nrm(ks[5], (DEPTH, n_pool, PAGE_SIZE, H_ATT, 2 * DK_ATT)),
        "cache_v": nrm(ks[6], (DEPTH, n_pool, PAGE_SIZE, H_ATT, DV_ATT)),
        "state_rnn": nrm(ks[7], (DEPTH, DEC_BATCH, H_RNN, DK_RNN, DV_RNN), 0.3),
        "state_ffn_conv": nrm(ks[8], (DEPTH, DEC_BATCH, CONV_W - 1, D_FF)),
        "page_table": page_table,
        "norm_mix": gain(ks[9], (DEPTH, D_MODEL)),
        "w_in": nrm(ks[10], (DEPTH, D_MODEL, D_IN), D_MODEL ** -0.5),
        "q_norm": gain(ks[11], (DEPTH, DK_ATT)),
        "k_norm": gain(ks[12], (DEPTH, DK_ATT)),
        "lam_q1": nrm(ks[13], (DEPTH, DK_ATT), 0.1),
        "lam_k1": nrm(ks[14], (DEPTH, DK_ATT), 0.1),
        "lam_q2": nrm(ks[15], (DEPTH, DK_ATT), 0.1),
        "lam_k2": nrm(ks[16], (DEPTH, DK_ATT), 0.1),
        "att_out_norm": gain(ks[17], (DEPTH, DV_ATT)),
        "rnn_out_norm": gain(ks[18], (DEPTH, DV_RNN)),
        "lower_bounds": nrm(ks[19], (DEPTH + 1, RNN_K_W), 0.1),
        "w_out": nrm(ks[20], (DEPTH, D_MIX, D_MODEL), D_MIX ** -0.5),
        "norm_ffn": gain(ks[21], (DEPTH, D_MODEL)),
        "w_gate": nrm(ks[22], (DEPTH, D_MODEL, D_FF), D_MODEL ** -0.5),
        "w_up": nrm(ks[23], (DEPTH, D_MODEL, D_FF), D_MODEL ** -0.5),
        "conv_w": nrm(ks[24], (DEPTH, CONV_W, D_FF), CONV_W ** -0.5),
        "conv_b": nrm(ks[25], (DEPTH, D_FF), 0.02),
        "w_down": nrm(ks[26], (DEPTH, D_FF, D_MODEL), D_FF ** -0.5),
        "norm_ple": gain(ks[27], (DEPTH, D_MODEL)),
        "w_ple_gate": nrm(ks[28], (DEPTH, D_MODEL, D_MODEL), D_MODEL ** -0.5),
        "w_ple_proj": nrm(ks[29], (DEPTH, D_PLE, D_MODEL), D_PLE ** -0.5),
    }


def reference(x_prompt, x_sample, p_prompt, p_sample, cache_k, cache_v, state_rnn, state_ffn_conv, page_table,
              norm_mix, w_in, q_norm, k_norm, lam_q1, lam_k1, lam_q2, lam_k2, att_out_norm, rnn_out_norm,
              lower_bounds, w_out, norm_ffn, w_gate, w_up, conv_w, conv_b, w_down, norm_ple, w_ple_gate,
              w_ple_proj):
    B, S = x_prompt.shape[0], x_prompt.shape[1]
    DB, T = x_sample.shape[0], x_sample.shape[1]
    n_pages = page_table.shape[1]
    past = n_pages * PAGE_SIZE
    pos_p = jnp.arange(S, dtype=jnp.int32)
    pos_s = past + jnp.arange(T, dtype=jnp.int32)
    lb_all = jnp.cumsum(jax.nn.softmax(lower_bounds.astype(jnp.float32), axis=0), axis=0)

    yp, ys = x_prompt, x_sample
    kp_l, vp_l, sp_l, cp_l, ks_l, vs_l, ss_l, cs_l = [], [], [], [], [], [], [], []
    for i in range(DEPTH):
        wts = (norm_mix[i], w_in[i], q_norm[i], k_norm[i], lam_q1[i], lam_k1[i], lam_q2[i], lam_k2[i],
               att_out_norm[i], rnn_out_norm[i], w_out[i], norm_ffn[i], w_gate[i], w_up[i], conv_w[i],
               conv_b[i], w_down[i], norm_ple[i], w_ple_gate[i], w_ple_proj[i])
        lam_init = 0.8 - 0.6 * math.exp(-0.3 * i)
        s0_p = jnp.zeros((B, H_RNN, DK_RNN, DV_RNN), jnp.float32)
        c0_p = jnp.zeros((B, CONV_W - 1, D_FF), x_prompt.dtype)
        yp, kp, vp, sp, cp = layer_forward(yp, p_prompt[i], pos_p, attend_prompt, s0_p, c0_p,
                                           lam_init, lb_all[i], wts)
        kp_l.append(kp.reshape(B, S // PAGE_SIZE, PAGE_SIZE, H_ATT, 2 * DK_ATT))
        vp_l.append(vp.reshape(B, S // PAGE_SIZE, PAGE_SIZE, H_ATT, DV_ATT))
        sp_l.append(sp.astype(state_rnn.dtype))
        cp_l.append(cp)
        k_past = cache_k[i, page_table].reshape(DB, past, H_ATT, 2, DK_ATT)
        v_past = cache_v[i, page_table].reshape(DB, past, H_ATT, DV_ATT)
        attend_s = functools.partial(attend_sample, k_past=k_past, v_past=v_past)
        ys, ks_, vs_, ss_, cs_ = layer_forward(ys, p_sample[i], pos_s, attend_s, state_rnn[i], state_ffn_conv[i],
                                               lam_init, lb_all[i], wts)
        ks_l.append(ks_)
        vs_l.append(vs_)
        ss_l.append(ss_.astype(state_rnn.dtype))
        cs_l.append(cs_)

    k_prompt = jnp.stack(kp_l)
    v_prompt = jnp.stack(vp_l)
    rnn_prompt = jnp.stack(sp_l)
    conv_prompt = jnp.stack(cp_l)
    k_sample = jnp.stack(ks_l)
    v_sample = jnp.stack(vs_l)
    rnn_sample = jnp.stack(ss_l)
    conv_sample = jnp.stack(cs_l)
    return (yp, ys, k_prompt, v_prompt, rnn_prompt, conv_prompt, k_sample, v_sample, rnn_sample, conv_sample)
```

```python
import functools
import math

import jax
import jax.numpy as jnp
from jax import lax
from jax.experimental import pallas as pl
from jax.experimental.pallas import tpu as pltpu

H_ATT = 8
DK_ATT = 64
H_RNN = 8
HEAD_W = 128
D_HEADS = H_ATT * HEAD_W
PAGE_SIZE = 128
CONV_W = 3
ROPE_THETA = 10000.0
EPS = 1e-6
NEG_INF = -1e30
RNN_CHUNK = 32
RNN_ROWS = 256
VMEM_LIMIT_V7X = 56 * 1024 * 1024

F32 = jnp.float32
BF16 = jnp.bfloat16


def _cparams(*sem):
    return pltpu.CompilerParams(dimension_semantics=sem, vmem_limit_bytes=VMEM_LIMIT_V7X)


def _tile(n, pref):
    t = min(n, pref)
    while n % t:
        t //= 2
    return t


def _split3(x):
    hi = x.astype(BF16)
    r1 = x - hi.astype(F32)
    mid = r1.astype(BF16)
    lo = (r1 - mid.astype(F32)).astype(BF16)
    return hi, mid, lo


def _dot(a, b):
    return jnp.dot(a, b, preferred_element_type=F32)


def _dot_nt(a, b):
    return lax.dot_general(a, b, (((1,), (1,)), ((), ())), preferred_element_type=F32)


def _dot_tn(a, b):
    return lax.dot_general(a, b, (((0,), (0,)), ((), ())), preferred_element_type=F32)


def _rms(x, gain):
    return x * lax.rsqrt(jnp.mean(x * x, axis=-1, keepdims=True) + EPS) * gain


def _params_kernel(lam_ref, lb_in_ref, lam_out_ref, lb_out_ref, *, layer, lam_init):
    v = lam_ref[...]
    s1 = jnp.sum(v[0:1] * v[1:2], axis=-1, keepdims=True)
    s2 = jnp.sum(v[2:3] * v[3:4], axis=-1, keepdims=True)
    lam = jnp.exp(s1) - jnp.exp(s2) + lam_init
    lam_out_ref[...] = jnp.broadcast_to(lam, lam_out_ref.shape)
    lb = lb_in_ref[...]
    e = jnp.exp(lb - jnp.max(lb, axis=0, keepdims=True))
    sm = e / jnp.sum(e, axis=0, keepdims=True)
    lb_out_ref[...] = jnp.sum(sm[0:layer + 1], axis=0, keepdims=True)


def layer_params(lam_vecs, lower_bounds, layer, lam_init):
    return pl.pallas_call(
        functools.partial(_params_kernel, layer=layer, lam_init=lam_init),
        out_shape=(jax.ShapeDtypeStruct((1, HEAD_W), F32),
                   jax.ShapeDtypeStruct((1, lower_bounds.shape[1]), F32)),
    )(lam_vecs, lower_bounds)


def _norm_matmul_kernel(x_ref, g_ref, w_ref, o_ref, h_ref):
    @pl.when(pl.program_id(1) == 0)
    def _():
        h_ref[...] = _rms(x_ref[...], g_ref[...]).astype(BF16)

    o_ref[...] = _dot(h_ref[...], w_ref[...])


def norm_matmul(x, gain, w, tm_pref=1024, tn_pref=512):
    M, K = x.shape
    N = w.shape[1]
    tm, tn = _tile(M, tm_pref), _tile(N, tn_pref)
    return pl.pallas_call(
        _norm_matmul_kernel,
        grid=(M // tm, N // tn),
        in_specs=[pl.BlockSpec((tm, K), lambda i, j: (i, 0)),
                  pl.BlockSpec((1, K), lambda i, j: (0, 0)),
                  pl.BlockSpec((K, tn), lambda i, j: (0, j))],
        out_specs=pl.BlockSpec((tm, tn), lambda i, j: (i, j)),
        out_shape=jax.ShapeDtypeStruct((M, N), F32),
        scratch_shapes=[pltpu.VMEM((tm, K), BF16)],
        compiler_params=_cparams("parallel", "arbitrary"),
    )(x, gain, w)


def _qk_prep_kernel(zq_ref, zk_ref, zv_ref, cos_ref, sin_ref, qg_ref, kg_ref, grp_ref,
                    q_ref, k_ref, kb_ref, v_ref, vb_ref):
    cos = cos_ref[...]
    sin = sin_ref[...]
    lane = lax.broadcasted_iota(jnp.int32, cos.shape, 1)
    first_half = (lane % DK_ATT) < (DK_ATT // 2)
    grp = grp_ref[...]

    def norm_rope(x, gain):
        hi, mid, lo = _split3(x * x)
        ss = (_dot(hi, grp) + _dot(mid, grp) + _dot(lo, grp)) * (1.0 / DK_ATT)
        y = x * lax.rsqrt(ss + EPS) * gain
        swapped = jnp.where(first_half, pltpu.roll(y, HEAD_W - DK_ATT // 2, 1),
                            pltpu.roll(y, DK_ATT // 2, 1))
        return y * cos + swapped * sin

    for h in range(H_ATT):
        sl = slice(h * HEAD_W, (h + 1) * HEAD_W)
        q = norm_rope(zq_ref[:, sl], qg_ref[...]) * (DK_ATT ** -0.5)
        q_ref[:, sl] = q.astype(BF16)
        k = norm_rope(zk_ref[:, sl], kg_ref[...])
        k_ref[:, sl] = k
        kb_ref[:, sl] = k.astype(BF16)
    v = zv_ref[...]
    v_ref[...] = v
    vb_ref[...] = v.astype(BF16)


def qk_prep(z, cos, sin, q_gain, k_gain, tm_pref=512):
    M = z.shape[0]
    tm = _tile(M, tm_pref)
    lane = jnp.arange(HEAD_W)
    grp = (lane[:, None] // DK_ATT == lane[None, :] // DK_ATT).astype(BF16)
    col = lambda c: pl.BlockSpec((tm, D_HEADS), lambda i: (i, c))
    row = pl.BlockSpec((tm, HEAD_W), lambda i: (i, 0))
    vec = pl.BlockSpec((1, HEAD_W), lambda i: (0, 0))
    out = pl.BlockSpec((tm, D_HEADS), lambda i: (i, 0))
    sd = lambda dt: jax.ShapeDtypeStruct((M, D_HEADS), dt)
    return pl.pallas_call(
        _qk_prep_kernel,
        grid=(M // tm,),
        in_specs=[col(0), col(1), col(2), row, row, vec, vec,
                  pl.BlockSpec((HEAD_W, HEAD_W), lambda i: (0, 0))],
        out_specs=(out, out, out, out, out),
        out_shape=(sd(BF16), sd(F32), sd(BF16), sd(F32), sd(BF16)),
        compiler_params=_cparams("parallel"),
    )(z, z, z, cos, sin, q_gain, k_gain, grp)


def _attn_prompt_kernel(lam_ref, q_ref, k_ref, v_ref, g_ref, o_ref, *, tq, out_scale):
    qi = pl.program_id(2)
    q = q_ref[0]
    lane = lax.broadcasted_iota(jnp.int32, q.shape, 1)
    zero = jnp.zeros_like(q)
    q2 = jnp.concatenate([jnp.where(lane < DK_ATT, q, zero),
                          jnp.where(lane >= DK_ATT, q, zero)], axis=0)

    def step(kb, carry, masked):
        m, l, acc = carry
        start = pl.multiple_of(kb * tq, tq)
        k = k_ref[0, pl.ds(start, tq), :]
        v = v_ref[0, pl.ds(start, tq), :]
        s = _dot_nt(q2, k)
        if masked:
            r = lax.broadcasted_iota(jnp.int32, s.shape, 0)
            c = lax.broadcasted_iota(jnp.int32, s.shape, 1)
            s = jnp.where(c <= r % tq, s, NEG_INF)
        m_new = jnp.maximum(m, jnp.max(s, axis=-1, keepdims=True))
        alpha = jnp.exp(m - m_new)
        p = jnp.exp(s - m_new)
        l = alpha * l + jnp.sum(p, axis=-1, keepdims=True)
        acc = alpha * acc + _dot(p.astype(BF16), v)
        return m_new, l, acc

    init = (jnp.full((2 * tq, 1), NEG_INF, F32), jnp.zeros((2 * tq, 1), F32),
            jnp.zeros((2 * tq, HEAD_W), F32))
    carry = lax.fori_loop(0, qi, lambda kb, c: step(kb, c, False), init)
    _, l, acc = step(qi, carry, True)
    o = acc / l
    od = o[:tq] - lam_ref[...] * o[tq:]
    o_ref[0] = (_rms(od, g_ref[...]) * out_scale).astype(BF16)


def attn_prompt(q, k, v, lam, att_gain, out_scale, tq_pref=256):
    B, S, _ = q.shape
    tq = _tile(S, tq_pref)
    kv = pl.BlockSpec((1, S, HEAD_W), lambda b, h, i: (b, 0, h))
    qo = pl.BlockSpec((1, tq, HEAD_W), lambda b, h, i: (b, i, h))
    vec = pl.BlockSpec((1, HEAD_W), lambda b, h, i: (0, 0))
    return pl.pallas_call(
        functools.partial(_attn_prompt_kernel, tq=tq, out_scale=out_scale),
        grid=(B, H_ATT, S // tq),
        in_specs=[vec, qo, kv, kv, vec],
        out_specs=qo,
        out_shape=jax.ShapeDtypeStruct(q.shape, BF16),
        compiler_params=_cparams("parallel", "parallel", "arbitrary"),
    )(lam, q, k, v, att_gain)


def _attn_sample_kernel(pt_ref, lam_ref, q_ref, kn_ref, vn_ref, g_ref, half_ref, *refs, n_pages, out_scale):
    del pt_ref
    k_refs, v_refs, o_ref = refs[:n_pages], refs[n_pages:2 * n_pages], refs[2 * n_pages]
    q = q_ref[0].astype(F32)
    half_sum = half_ref[...]

    def update(carry, k3, v3):
        m, l, a1, a2 = carry
        T = k3.shape[0]
        s = _dot((k3 * q).reshape(T * H_ATT, HEAD_W).astype(BF16), half_sum).reshape(T, H_ATT, 2 * HEAD_W)
        m_new = jnp.maximum(m, jnp.max(s, axis=0))
        alpha = jnp.exp(m - m_new)
        p = jnp.exp(s - m_new)
        l = alpha * l + jnp.sum(p, axis=0)
        a1 = alpha[:, :HEAD_W] * a1 + jnp.sum(p[:, :, :HEAD_W] * v3, axis=0)
        a2 = alpha[:, HEAD_W:] * a2 + jnp.sum(p[:, :, HEAD_W:] * v3, axis=0)
        return m_new, l, a1, a2

    carry = (jnp.full((H_ATT, 2 * HEAD_W), NEG_INF, F32), jnp.zeros((H_ATT, 2 * HEAD_W), F32),
             jnp.zeros((H_ATT, HEAD_W), F32), jnp.zeros((H_ATT, HEAD_W), F32))
    for j in range(n_pages):
        carry = update(carry, k_refs[j][0], v_refs[j][0])
    _, l, a1, a2 = update(carry, kn_ref[...], vn_ref[...])
    od = a1 / l[:, :HEAD_W] - lam_ref[...] * (a2 / l[:, HEAD_W:])
    o_ref[0] = _rms(od, g_ref[...]) * out_scale


def attn_sample(q, k_new, v_new, cache_k, cache_v, page_table, lam, att_gain, out_scale):
    DB = q.shape[0]
    n_pages = page_table.shape[1]
    lane = jnp.arange(HEAD_W)
    half_sum = (lane[:, None] // DK_ATT == jnp.arange(2 * HEAD_W)[None, :] // HEAD_W).astype(BF16)
    tok = pl.BlockSpec((1, H_ATT, HEAD_W), lambda b, pt: (b, 0, 0))
    vec = pl.BlockSpec((1, HEAD_W), lambda b, pt: (0, 0))
    page = lambda j: pl.BlockSpec((1, PAGE_SIZE, H_ATT, HEAD_W),
                                  lambda b, pt: (pt[b * n_pages + j], 0, 0, 0))
    pages = [page(j) for j in range(n_pages)]
    return pl.pallas_call(
        functools.partial(_attn_sample_kernel, n_pages=n_pages, out_scale=out_scale),
        grid_spec=pltpu.PrefetchScalarGridSpec(
            num_scalar_prefetch=1,
            grid=(DB,),
            in_specs=[vec, tok, tok, tok, vec,
                      pl.BlockSpec((HEAD_W, 2 * HEAD_W), lambda b, pt: (0, 0))] + pages + pages,
            out_specs=tok,
        ),
        out_shape=jax.ShapeDtypeStruct((DB, H_ATT, HEAD_W), F32),
        compiler_params=_cparams("arbitrary"),
    )(page_table.reshape(-1), lam, q, k_new, v_new, att_gain, half_sum,
      *([cache_k] * n_pages), *([cache_v] * n_pages))


def _gates(zf, lb):
    f = lb + (1.0 - lb) * jax.nn.sigmoid(zf)
    return jnp.log(f), (1.0 - lb) * jax.nn.sigmoid(-zf)


def _hgrn_prompt_kernel(q_ref, f_ref, i_ref, g_ref, lb_ref, gain_ref, tril_ref, o_ref, st_ref, s_scr,
                        *, rows, chunk):
    t = pl.program_id(1)

    @pl.when(t == 0)
    def _():
        s_scr[...] = jnp.zeros_like(s_scr)

    tril = tril_ref[...]
    causal = (lax.broadcasted_iota(jnp.int32, (chunk, chunk), 1)
              <= lax.broadcasted_iota(jnp.int32, (chunk, chunk), 0))
    mid = chunk // 2 - 1

    def head(h, _):
        sl = pl.ds(pl.multiple_of(h * HEAD_W, HEAD_W), HEAD_W)
        log_f, kk = _gates(f_ref[:, sl], lb_ref[:, sl])
        hi, md, lo = _split3(log_f)
        bc = _dot(tril, hi) + _dot(tril, md) + _dot(tril, lo)
        q, v, g = q_ref[:, sl], i_ref[:, sl], g_ref[:, sl]
        st = s_scr[h]
        for c in range(rows // chunk):
            r0 = c * chunk
            bl = bc[r0:r0 + chunk]
            if c:
                bl = bl - bc[r0 - 1:r0]
            qc, kc, vc = q[r0:r0 + chunk], kk[r0:r0 + chunk], v[r0:r0 + chunk].astype(BF16)
            b_mid, b_last = bl[mid:mid + 1], bl[chunk - 1:chunk]
            a = _dot_nt((qc * jnp.exp(bl - b_mid)).astype(BF16), (kc * jnp.exp(b_mid - bl)).astype(BF16))
            a = jnp.where(causal, a, 0.0)
            o = _dot(a.astype(BF16), vc) + _dot_nt((qc * jnp.exp(bl)).astype(BF16), st.astype(BF16))
            st = st * jnp.exp(b_last) + _dot_tn(vc, (kc * jnp.exp(b_last - bl)).astype(BF16))
            y = _rms(o, gain_ref[...]) * (g[r0:r0 + chunk] * jax.nn.sigmoid(g[r0:r0 + chunk]))
            o_ref[r0:r0 + chunk, sl] = y.astype(BF16)
        s_scr[h] = st
        return 0

    lax.fori_loop(0, H_RNN, head, 0)

    @pl.when(t == pl.num_programs(1) - 1)
    def _():
        for h in range(H_RNN):
            st_ref[0, h] = s_scr[h].T


def hgrn_prompt(z, lb, rnn_gain, B, S):
    rows = _tile(S, RNN_ROWS)
    chunk = _tile(rows, RNN_CHUNK)
    nt = S // rows
    tril = (jnp.arange(rows)[:, None] >= jnp.arange(rows)[None, :]).astype(BF16)
    col = lambda c: pl.BlockSpec((rows, D_HEADS), lambda b, t: (b * nt + t, c))
    return pl.pallas_call(
        functools.partial(_hgrn_prompt_kernel, rows=rows, chunk=chunk),
        grid=(B, nt),
        in_specs=[col(3), col(4), col(5), col(6),
                  pl.BlockSpec((1, D_HEADS), lambda b, t: (0, 0)),
                  pl.BlockSpec((1, HEAD_W), lambda b, t: (0, 0)),
                  pl.BlockSpec((rows, rows), lambda b, t: (0, 0))],
        out_specs=(pl.BlockSpec((rows, D_HEADS), lambda b, t: (b * nt + t, 0)),
                   pl.BlockSpec((1, H_RNN, HEAD_W, HEAD_W), lambda b, t: (b, 0, 0, 0))),
        out_shape=(jax.ShapeDtypeStruct((B * S, D_HEADS), BF16),
                   jax.ShapeDtypeStruct((B, H_RNN, HEAD_W, HEAD_W), F32)),
        scratch_shapes=[pltpu.VMEM((H_RNN, HEAD_W, HEAD_W), F32)],
        compiler_params=_cparams("parallel", "arbitrary"),
    )(z, z, z, z, lb, rnn_gain, tril)


def _hgrn_step_kernel(q_ref, f_ref, i_ref, g_ref, lb_ref, gain_ref, s0_ref, o_ref, s1_ref, *, bs):
    def column(rowvec):
        return jnp.broadcast_to(rowvec, (HEAD_W, HEAD_W)).T

    def head(h, _):
        sl = pl.ds(pl.multiple_of(h * HEAD_W, HEAD_W), HEAD_W)
        log_f, kk = _gates(f_ref[:, sl], lb_ref[:, sl])
        q, v, g = q_ref[:, sl], i_ref[:, sl], g_ref[:, sl]
        decay = jnp.exp(log_f)
        q_dec = q * decay
        a = jnp.sum(q_dec * (kk * jnp.exp(-log_f)), axis=-1, keepdims=True)
        inter = []
        for b in range(bs):
            s0 = s0_ref[b, h]
            inter.append(jnp.sum(column(q_dec[b:b + 1]) * s0, axis=0, keepdims=True))
            s1_ref[b, h] = column(decay[b:b + 1]) * s0 + column(kk[b:b + 1]) * v[b:b + 1]
        o = a * v + jnp.concatenate(inter, axis=0)
        o_ref[:, sl] = (_rms(o, gain_ref[...]) * (g * jax.nn.sigmoid(g))).astype(BF16)
        return 0

    lax.fori_loop(0, H_RNN, head, 0)


def hgrn_step(z, lb, rnn_gain, s0, bs_pref=8):
    DB = z.shape[0]
    bs = _tile(DB, bs_pref)
    col = lambda c: pl.BlockSpec((bs, D_HEADS), lambda i: (i, c))
    st = pl.BlockSpec((bs, H_RNN, HEAD_W, HEAD_W), lambda i: (i, 0, 0, 0))
    return pl.pallas_call(
        functools.partial(_hgrn_step_kernel, bs=bs),
        grid=(DB // bs,),
        in_specs=[col(3), col(4), col(5), col(6),
                  pl.BlockSpec((1, D_HEADS), lambda i: (0, 0)),
                  pl.BlockSpec((1, HEAD_W), lambda i: (0, 0)), st],
        out_specs=(pl.BlockSpec((bs, D_HEADS), lambda i: (i, 0)), st),
        out_shape=(jax.ShapeDtypeStruct((DB, D_HEADS), BF16), jax.ShapeDtypeStruct(s0.shape, F32)),
        compiler_params=_cparams("parallel"),
    )(z, z, z, z, lb, rnn_gain, s0)


def _out_proj_kernel(a_ref, r_ref, wa_ref, wr_ref, x_ref, g_ref, x1_ref, h1_ref):
    x1 = x_ref[...] + _dot(a_ref[...], wa_ref[...]) + _dot(r_ref[...], wr_ref[...])
    x1_ref[...] = x1
    h1_ref[...] = _rms(x1, g_ref[...]).astype(BF16)


def out_proj(o_att, o_rnn, w_out, x, ffn_gain, tm_pref=256):
    M, D = x.shape
    tm = _tile(M, tm_pref)
    act = pl.BlockSpec((tm, D_HEADS), lambda i: (i, 0))
    full = pl.BlockSpec((tm, D), lambda i: (i, 0))
    return pl.pallas_call(
        _out_proj_kernel,
        grid=(M // tm,),
        in_specs=[act, act,
                  pl.BlockSpec((D_HEADS, D), lambda i: (0, 0)),
                  pl.BlockSpec((D_HEADS, D), lambda i: (1, 0)),
                  full, pl.BlockSpec((1, D), lambda i: (0, 0))],
        out_specs=(full, full),
        out_shape=(jax.ShapeDtypeStruct((M, D), F32), jax.ShapeDtypeStruct((M, D), BF16)),
        compiler_params=_cparams("parallel"),
    )(o_att, o_rnn, w_out, w_out, x, ffn_gain)


def _ffn_seq_kernel(h_ref, wg_ref, wu_ref, cw_ref, cb_ref, act_ref, tail_ref, carry_ref, *, tiles_per_seq):
    m = pl.program_id(1)

    @pl.when(m % tiles_per_seq == 0)
    def _():
        carry_ref[...] = jnp.zeros_like(carry_ref)

    h = h_ref[...]
    g = _dot(h, wg_ref[...])
    u = _dot(h, wu_ref[...])
    tm = g.shape[0]
    r = lax.broadcasted_iota(jnp.int32, g.shape, 0)
    prev2, prev1 = carry_ref[0:1], carry_ref[1:2]
    g1 = jnp.where(r == 0, prev1, pltpu.roll(g, 1, 0))
    g2 = jnp.where(r == 0, prev2, jnp.where(r == 1, prev1, pltpu.roll(g, 2, 0)))
    cw = cw_ref[...]
    c = cb_ref[...] + g2 * cw[0:1] + g1 * cw[1:2] + g * cw[2:3]
    act_ref[...] = (c * jax.nn.sigmoid(c) * u).astype(BF16)
    tail = g[tm - (CONV_W - 1):]
    carry_ref[0:CONV_W - 1] = tail
    tail_ref[0] = tail


def ffn_seq(h, w_gate, w_up, conv_w, conv_b, B, S, tm_pref=512, tn_pref=512):
    M, D = h.shape
    F = w_gate.shape[1]
    tm, tn = _tile(S, tm_pref), _tile(F, tn_pref)
    tiles_per_seq = S // tm
    wspec = pl.BlockSpec((D, tn), lambda n, m: (0, n))
    return pl.pallas_call(
        functools.partial(_ffn_seq_kernel, tiles_per_seq=tiles_per_seq),
        grid=(F // tn, M // tm),
        in_specs=[pl.BlockSpec((tm, D), lambda n, m: (m, 0)), wspec, wspec,
                  pl.BlockSpec((CONV_W, tn), lambda n, m: (0, n)),
                  pl.BlockSpec((1, tn), lambda n, m: (0, n))],
        out_specs=(pl.BlockSpec((tm, tn), lambda n, m: (m, n)),
                   pl.BlockSpec((1, CONV_W - 1, tn), lambda n, m: (m // tiles_per_seq, 0, n))),
        out_shape=(jax.ShapeDtypeStruct((M, F), BF16), jax.ShapeDtypeStruct((B, CONV_W - 1, F), F32)),
        scratch_shapes=[pltpu.VMEM((8, tn), F32)],
        compiler_params=_cparams("parallel", "arbitrary"),
    )(h, w_gate, w_up, conv_w, conv_b)


def _ffn_step_kernel(h_ref, wg_ref, wu_ref, cw_ref, cb_ref, p2_ref, p1_ref, act_ref, g_ref):
    h = h_ref[...]
    g = _dot(h, wg_ref[...])
    u = _dot(h, wu_ref[...])
    cw = cw_ref[...]
    c = cb_ref[...] + p2_ref[...] * cw[0:1] + p1_ref[...] * cw[1:2] + g * cw[2:3]
    act_ref[...] = (c * jax.nn.sigmoid(c) * u).astype(BF16)
    g_ref[...] = g


def ffn_step(h, w_gate, w_up, conv_w, conv_b, prev2, prev1, tn_pref=512):
    M, D = h.shape
    F = w_gate.shape[1]
    tn = _tile(F, tn_pref)
    wspec = pl.BlockSpec((D, tn), lambda n: (0, n))
    tile = pl.BlockSpec((M, tn), lambda n: (0, n))
    return pl.pallas_call(
        _ffn_step_kernel,
        grid=(F // tn,),
        in_specs=[pl.BlockSpec((M, D), lambda n: (0, 0)), wspec, wspec,
                  pl.BlockSpec((CONV_W, tn), lambda n: (0, n)),
                  pl.BlockSpec((1, tn), lambda n: (0, n)), tile, tile],
        out_specs=(tile, tile),
        out_shape=(jax.ShapeDtypeStruct((M, F), BF16), jax.ShapeDtypeStruct((M, F), F32)),
        compiler_params=_cparams("parallel"),
    )(h, w_gate, w_up, conv_w, conv_b, prev2, prev1)


def _down_kernel(a_ref, w_ref, x_ref, o_ref):
    o_ref[...] = x_ref[...] + _dot(a_ref[...], w_ref[...])


def ffn_down(act, w_down, x, tm_pref=512, tn_pref=512):
    M, F = act.shape
    D = w_down.shape[1]
    tm, tn = _tile(M, tm_pref), _tile(D, tn_pref)
    tile = pl.BlockSpec((tm, tn), lambda i, j: (i, j))
    return pl.pallas_call(
        _down_kernel,
        grid=(M // tm, D // tn),
        in_specs=[pl.BlockSpec((tm, F), lambda i, j: (i, 0)),
                  pl.BlockSpec((F, tn), lambda i, j: (0, j)), tile],
        out_specs=tile,
        out_shape=jax.ShapeDtypeStruct((M, D), F32),
        compiler_params=_cparams("parallel", "arbitrary"),
    )(act, w_down, x)


def _ple_kernel(x_ref, g_ref, wg_ref, p_ref, wp_ref, o_ref, h_ref, *, tn):
    j = pl.program_id(1)

    @pl.when(j == 0)
    def _():
        h_ref[...] = _rms(x_ref[...], g_ref[...]).astype(BF16)

    gate = jax.nn.sigmoid(_dot(h_ref[...], wg_ref[...]))
    emb = _dot(p_ref[...].astype(BF16), wp_ref[...])
    o_ref[...] = x_ref[:, pl.ds(pl.multiple_of(j * tn, tn), tn)] + gate * emb


def ple(x, gain, w_gate, p, w_proj, tm_pref=512, tn_pref=512):
    M, D = x.shape
    DP = p.shape[1]
    tm, tn = _tile(M, tm_pref), _tile(D, tn_pref)
    return pl.pallas_call(
        functools.partial(_ple_kernel, tn=tn),
        grid=(M // tm, D // tn),
        in_specs=[pl.BlockSpec((tm, D), lambda i, j: (i, 0)),
                  pl.BlockSpec((1, D), lambda i, j: (0, 0)),
                  pl.BlockSpec((D, tn), lambda i, j: (0, j)),
                  pl.BlockSpec((tm, DP), lambda i, j: (i, 0)),
                  pl.BlockSpec((DP, tn), lambda i, j: (0, j))],
        out_specs=pl.BlockSpec((tm, tn), lambda i, j: (i, j)),
        out_shape=jax.ShapeDtypeStruct((M, D), F32),
        scratch_shapes=[pltpu.VMEM((tm, D), BF16)],
        compiler_params=_cparams("parallel", "arbitrary"),
    )(x, gain, w_gate, p, w_proj)


def _rope_tables(pos):
    half = DK_ATT // 2
    inv_freq = 1.0 / (ROPE_THETA ** (jnp.arange(half, dtype=F32) * (2.0 / DK_ATT)))
    ang = pos.astype(F32)[:, None] * inv_freq[None, :]
    cos, sin = jnp.cos(ang), jnp.sin(ang)
    return (jnp.concatenate([cos] * 4, axis=-1), jnp.concatenate([-sin, sin] * 2, axis=-1))


def _row(v):
    return v.reshape(1, -1).astype(F32)


def kernel(x_prompt, x_sample, p_prompt, p_sample, cache_k, cache_v, state_rnn, state_ffn_conv, page_table,
           norm_mix, w_in, q_norm, k_norm, lam_q1, lam_k1, lam_q2, lam_k2, att_out_norm, rnn_out_norm,
           lower_bounds, w_out, norm_ffn, w_gate, w_up, conv_w, conv_b, w_down, norm_ple, w_ple_gate,
           w_ple_proj):
    B, S, D = x_prompt.shape
    DB, T, _ = x_sample.shape
    assert T == 1, "the sample group is implemented for one new token per sequence"
    depth = w_in.shape[0]
    past = page_table.shape[1] * PAGE_SIZE
    cos_p, sin_p = _rope_tables(jnp.arange(S, dtype=jnp.int32))
    cos_p, sin_p = jnp.tile(cos_p, (B, 1)), jnp.tile(sin_p, (B, 1))
    cos_s, sin_s = _rope_tables(jnp.full((DB,), past, dtype=jnp.int32))

    yp = x_prompt.reshape(B * S, D)
    ys = x_sample.reshape(DB, D)
    outs = [[] for _ in range(8)]
    for i in range(depth):
        lam_init = 0.8 - 0.6 * math.exp(-0.3 * i)
        out_scale = 1.0 - lam_init
        lam, lb = layer_params(jnp.stack([lam_q1[i], lam_k1[i], lam_q2[i], lam_k2[i]]).astype(F32),
                               lower_bounds.astype(F32), i, lam_init)
        w_in_b, w_out_b = w_in[i].astype(BF16), w_out[i].astype(BF16)
        w_gate_b, w_up_b, w_down_b = w_gate[i].astype(BF16), w_up[i].astype(BF16), w_down[i].astype(BF16)
        w_pg_b, w_pp_b = w_ple_gate[i].astype(BF16), w_ple_proj[i].astype(BF16)
        q_gain = jnp.tile(_row(q_norm[i]), (1, 2))
        k_gain = jnp.tile(_row(k_norm[i]), (1, 2))
        att_gain, rnn_gain = _row(att_out_norm[i]), _row(rnn_out_norm[i])

        def mix_in(x, cos, sin):
            z = norm_matmul(x, _row(norm_mix[i]), w_in_b)
            return (z,) + tuple(qk_prep(z, cos, sin, q_gain, k_gain))

        def mix_out(x, p, o_att, o_rnn):
            return out_proj(o_att, o_rnn, w_out_b, x, _row(norm_ffn[i]))

        def embed(x, act, p):
            x = ffn_down(act, w_down_b, x)
            return ple(x, _row(norm_ple[i]), w_pg_b, p, w_pp_b)

        z, q, k, kb, v, vb = mix_in(yp, cos_p, sin_p)
        o_att = attn_prompt(q.reshape(B, S, -1), kb.reshape(B, S, -1), vb.reshape(B, S, -1),
                            lam, att_gain, out_scale).reshape(B * S, -1)
        o_rnn, rnn_p = hgrn_prompt(z, lb, rnn_gain, B, S)
        x1, h1 = mix_out(yp, None, o_att, o_rnn)
        act, conv_p = ffn_seq(h1, w_gate_b, w_up_b, conv_w[i], _row(conv_b[i]), B, S)
        yp = embed(x1, act, p_prompt[i].reshape(B * S, -1))
        outs[0].append(k.reshape(B, S // PAGE_SIZE, PAGE_SIZE, H_ATT, HEAD_W))
        outs[1].append(v.reshape(B, S // PAGE_SIZE, PAGE_SIZE, H_ATT, HEAD_W))
        outs[2].append(rnn_p.astype(state_rnn.dtype))
        outs[3].append(conv_p)

        z, q, k, kb, v, vb = mix_in(ys, cos_s, sin_s)
        per_head = lambda a: a.reshape(DB, H_ATT, HEAD_W)
        o_att = attn_sample(per_head(q), per_head(k), per_head(v), cache_k[i], cache_v[i], page_table,
                            lam, att_gain, out_scale).reshape(DB, D_HEADS).astype(BF16)
        o_rnn, rnn_s = hgrn_step(z, lb, rnn_gain, state_rnn[i].astype(F32))
        x1, h1 = mix_out(ys, None, o_att, o_rnn)
        buf = state_ffn_conv[i]
        act, g_new = ffn_step(h1, w_gate_b, w_up_b, conv_w[i], _row(conv_b[i]), buf[:, 0], buf[:, 1])
        ys = embed(x1, act, p_sample[i].reshape(DB, -1))
        outs[4].append(k.reshape(DB, T, H_ATT, HEAD_W))
        outs[5].append(v.reshape(DB, T, H_ATT, HEAD_W))
        outs[6].append(rnn_s.astype(state_rnn.dtype))
        outs[7].append(jnp.stack([buf[:, 1], g_new], axis=1))

    return (yp.reshape(B, S, D), ys.reshape(DB, T, D)) + tuple(jnp.stack(o) for o in outs)
```

```python
import functools
import math

import jax
import jax.numpy as jnp
from jax import lax
from jax.experimental import pallas as pl
from jax.experimental.pallas import tpu as pltpu

H_ATT = 8
DK_ATT = 64
H_RNN = 8
HEAD_W = 128
D_HEADS = H_ATT * HEAD_W
PAGE_SIZE = 128
CONV_W = 3
ROPE_THETA = 10000.0
EPS = 1e-6
NEG_INF = -1e30
RNN_CHUNK = 32
RNN_ROWS = 256
ATT_HEADS_PER_STEP = 4
VT_ROWS = HEAD_W + 16
Q_SCALE = DK_ATT ** -0.5 * math.log2(math.e)
VMEM_LIMIT_V7X = 56 * 1024 * 1024

F32 = jnp.float32
BF16 = jnp.bfloat16


def _cparams(*sem):
    return pltpu.CompilerParams(dimension_semantics=sem, vmem_limit_bytes=VMEM_LIMIT_V7X)


def _tile(n, pref):
    t = min(n, pref)
    while n % t:
        t //= 2
    return t


def _split3(x):
    hi = x.astype(BF16)
    r1 = x - hi.astype(F32)
    mid = r1.astype(BF16)
    lo = (r1 - mid.astype(F32)).astype(BF16)
    return hi, mid, lo


def _dot(a, b):
    return jnp.dot(a, b, preferred_element_type=F32)


def _dot_nt(a, b):
    return lax.dot_general(a, b, (((1,), (1,)), ((), ())), preferred_element_type=F32)


def _dot_tn(a, b):
    return lax.dot_general(a, b, (((0,), (0,)), ((), ())), preferred_element_type=F32)


def _rms(x, gain):
    return x * lax.rsqrt(jnp.mean(x * x, axis=-1, keepdims=True) + EPS) * gain


def _params_kernel(lam_ref, lb_in_ref, lam_out_ref, lb_out_ref, *, layer, lam_init):
    v = lam_ref[...]
    s1 = jnp.sum(v[0:1] * v[1:2], axis=-1, keepdims=True)
    s2 = jnp.sum(v[2:3] * v[3:4], axis=-1, keepdims=True)
    lam = jnp.exp(s1) - jnp.exp(s2) + lam_init
    lam_out_ref[...] = jnp.broadcast_to(lam, lam_out_ref.shape)
    lb = lb_in_ref[...]
    e = jnp.exp(lb - jnp.max(lb, axis=0, keepdims=True))
    sm = e / jnp.sum(e, axis=0, keepdims=True)
    lb_out_ref[...] = jnp.sum(sm[0:layer + 1], axis=0, keepdims=True)


def layer_params(lam_vecs, lower_bounds, layer, lam_init):
    return pl.pallas_call(
        functools.partial(_params_kernel, layer=layer, lam_init=lam_init),
        out_shape=(jax.ShapeDtypeStruct((1, HEAD_W), F32),
                   jax.ShapeDtypeStruct((1, lower_bounds.shape[1]), F32)),
    )(lam_vecs, lower_bounds)


def _norm_matmul_kernel(x_ref, g_ref, w_ref, o_ref, h_ref):
    @pl.when(pl.program_id(1) == 0)
    def _():
        h_ref[...] = _rms(x_ref[...], g_ref[...]).astype(BF16)

    o_ref[...] = _dot(h_ref[...], w_ref[...])


def norm_matmul(x, gain, w, tm_pref=1024, tn_pref=512):
    M, K = x.shape
    N = w.shape[1]
    tm, tn = _tile(M, tm_pref), _tile(N, tn_pref)
    return pl.pallas_call(
        _norm_matmul_kernel,
        grid=(M // tm, N // tn),
        in_specs=[pl.BlockSpec((tm, K), lambda i, j: (i, 0)),
                  pl.BlockSpec((1, K), lambda i, j: (0, 0)),
                  pl.BlockSpec((K, tn), lambda i, j: (0, j))],
        out_specs=pl.BlockSpec((tm, tn), lambda i, j: (i, j)),
        out_shape=jax.ShapeDtypeStruct((M, N), F32),
        scratch_shapes=[pltpu.VMEM((tm, K), BF16)],
        compiler_params=_cparams("parallel", "arbitrary"),
    )(x, gain, w)


def _qk_prep_kernel(zq_ref, zk_ref, zv_ref, cos_ref, sin_ref, qg_ref, kg_ref, grp_ref,
                    q_ref, k_ref, kb_ref, v_ref, vb_ref, *, transposed):
    cos = cos_ref[...]
    sin = sin_ref[...]
    lane = lax.broadcasted_iota(jnp.int32, cos.shape, 1)
    first_half = (lane % DK_ATT) < (DK_ATT // 2)
    grp = grp_ref[...]

    def norm_rope(x, gain):
        hi, mid, lo = _split3(x * x)
        ss = (_dot(hi, grp) + _dot(mid, grp) + _dot(lo, grp)) * (1.0 / DK_ATT)
        y = x * lax.rsqrt(ss + EPS) * gain
        swapped = jnp.where(first_half, pltpu.roll(y, HEAD_W - DK_ATT // 2, 1),
                            pltpu.roll(y, DK_ATT // 2, 1))
        return y * cos + swapped * sin

    for h in range(H_ATT):
        sl = slice(h * HEAD_W, (h + 1) * HEAD_W)
        q = norm_rope(zq_ref[:, sl], qg_ref[...]) * Q_SCALE
        k = norm_rope(zk_ref[:, sl], kg_ref[...])
        k_ref[:, sl] = k
        kb_ref[:, sl] = k.astype(BF16)
        v = zv_ref[:, sl]
        v_ref[:, sl] = v
        if transposed:
            q_ref[0, sl, :] = q.T.astype(BF16)
            r0 = h * VT_ROWS
            vb_ref[0, r0:r0 + HEAD_W, :] = v.T.astype(BF16)
            pad = (VT_ROWS - HEAD_W, v.shape[0])
            vb_ref[0, r0 + HEAD_W:r0 + VT_ROWS, :] = (lax.broadcasted_iota(jnp.int32, pad, 0) == 0).astype(BF16)
        else:
            q_ref[:, sl] = q.astype(BF16)
            vb_ref[:, sl] = v.astype(BF16)


def qk_prep(z, cos, sin, q_gain, k_gain, seqs=None, tm_pref=512):
    M = z.shape[0]
    tm = _tile(M if seqs is None else seqs[1], tm_pref)
    lane = jnp.arange(HEAD_W)
    grp = (lane[:, None] // DK_ATT == lane[None, :] // DK_ATT).astype(BF16)
    col = lambda c: pl.BlockSpec((tm, D_HEADS), lambda i: (i, c))
    row = pl.BlockSpec((tm, HEAD_W), lambda i: (i, 0))
    vec = pl.BlockSpec((1, HEAD_W), lambda i: (0, 0))
    out = pl.BlockSpec((tm, D_HEADS), lambda i: (i, 0))
    sd = lambda dt: jax.ShapeDtypeStruct((M, D_HEADS), dt)
    if seqs is None:
        out_q, out_v, sd_q, sd_v = out, out, sd(BF16), sd(BF16)
    else:
        B, S = seqs
        tiles = S // tm
        out_q = pl.BlockSpec((1, D_HEADS, tm), lambda i: (i // tiles, 0, i % tiles))
        out_v = pl.BlockSpec((1, H_ATT * VT_ROWS, tm), lambda i: (i // tiles, 0, i % tiles))
        sd_q = jax.ShapeDtypeStruct((B, D_HEADS, S), BF16)
        sd_v = jax.ShapeDtypeStruct((B, H_ATT * VT_ROWS, S), BF16)
    return pl.pallas_call(
        functools.partial(_qk_prep_kernel, transposed=seqs is not None),
        grid=(M // tm,),
        in_specs=[col(0), col(1), col(2), row, row, vec, vec,
                  pl.BlockSpec((HEAD_W, HEAD_W), lambda i: (0, 0))],
        out_specs=(out_q, out, out, out, out_v),
        out_shape=(sd_q, sd(F32), sd(BF16), sd(F32), sd_v),
        compiler_params=_cparams("parallel"),
    )(z, z, z, cos, sin, q_gain, k_gain, grp)


def _attn_prompt_kernel(lam_ref, qt_ref, k_ref, vt_ref, g_ref, o_ref, m_scr, acc_scr, *, tq, tk, out_scale):
    qi = pl.program_id(2)
    n_full = (qi * tq) // tk
    dim = lax.broadcasted_iota(jnp.int32, (HEAD_W, tq), 0)
    q2t = []
    for u in range(ATT_HEADS_PER_STEP):
        qt = qt_ref[0, u * HEAD_W:(u + 1) * HEAD_W, :]
        zero = jnp.zeros_like(qt)
        q2t.append(jnp.concatenate([jnp.where(dim < DK_ATT, qt, zero),
                                    jnp.where(dim >= DK_ATT, qt, zero)], axis=1))
    m_scr[...] = jnp.full(m_scr.shape, NEG_INF, F32)
    acc_scr[...] = jnp.zeros(acc_scr.shape, F32)

    def block(kb, masked):
        start = pl.multiple_of(kb * tk, tk)

        def scores(u):
            return _dot(k_ref[0, pl.ds(start, tk), u * HEAD_W:(u + 1) * HEAD_W], q2t[u])

        s_next = scores(0)
        for u in range(ATT_HEADS_PER_STEP):
            s = s_next
            if u + 1 < ATT_HEADS_PER_STEP:
                s_next = scores(u + 1)
            if masked:
                key = start + lax.broadcasted_iota(jnp.int32, s.shape, 0)
                qry = qi * tq + lax.broadcasted_iota(jnp.int32, s.shape, 1) % tq
                s = jnp.where(key <= qry, s, NEG_INF)
            m_prev = m_scr[u]
            m_new = jnp.maximum(m_prev, jnp.max(s, axis=0, keepdims=True))
            m_scr[u] = m_new
            p = jnp.exp2(s - m_new).astype(BF16)
            vt = vt_ref[0, u * VT_ROWS:(u + 1) * VT_ROWS, pl.ds(start, tk)]
            acc_scr[u] = jnp.exp2(m_prev - m_new) * acc_scr[u] + _dot(vt, p)

    def body(kb, carry):
        block(kb, False)
        return carry

    lax.fori_loop(0, n_full, body, 0)
    block(n_full, True)
    for u in range(ATT_HEADS_PER_STEP):
        acc = acc_scr[u]
        o = acc[:HEAD_W] / acc[HEAD_W:HEAD_W + 1]
        od = o[:, :tq] - lam_ref[:, :1] * o[:, tq:]
        y = od * lax.rsqrt(jnp.mean(od * od, axis=0, keepdims=True) + EPS) * g_ref[...] * out_scale
        o_ref[0, :, u * HEAD_W:(u + 1) * HEAD_W] = y.T.astype(BF16)


def attn_prompt(qt, k, vt, lam, att_gain, out_scale, tq_pref=256, tk_pref=512):
    B, S, _ = k.shape
    tq = _tile(S, tq_pref)
    tk = max(tq, _tile(S, tk_pref))
    U = ATT_HEADS_PER_STEP
    return pl.pallas_call(
        functools.partial(_attn_prompt_kernel, tq=tq, tk=tk, out_scale=out_scale),
        grid=(B, H_ATT // U, S // tq),
        in_specs=[pl.BlockSpec((1, HEAD_W), lambda b, h, i: (0, 0)),
                  pl.BlockSpec((1, U * HEAD_W, tq), lambda b, h, i: (b, h, i)),
                  pl.BlockSpec((1, S, U * HEAD_W), lambda b, h, i: (b, 0, h)),
                  pl.BlockSpec((1, U * VT_ROWS, S), lambda b, h, i: (b, h, 0)),
                  pl.BlockSpec((HEAD_W, 1), lambda b, h, i: (0, 0))],
        out_specs=pl.BlockSpec((1, tq, U * HEAD_W), lambda b, h, i: (b, i, h)),
        out_shape=jax.ShapeDtypeStruct(k.shape, BF16),
        scratch_shapes=[pltpu.VMEM((U, 1, 2 * tq), F32), pltpu.VMEM((U, VT_ROWS, 2 * tq), F32)],
        compiler_params=_cparams("parallel", "parallel", "arbitrary"),
    )(lam, qt, k, vt, att_gain.reshape(HEAD_W, 1))


def _attn_sample_kernel(pt_ref, lam_ref, q_ref, kn_ref, vn_ref, g_ref, *refs, n_pages, out_scale):
    del pt_ref
    k_refs, v_refs, o_ref = refs[:n_pages], refs[n_pages:2 * n_pages], refs[2 * n_pages]
    rows = 2 * H_ATT
    q = q_ref[0]
    q16 = jnp.concatenate([q, q], axis=0)
    r = lax.broadcasted_iota(jnp.int32, (rows, HEAD_W), 0)
    c = lax.broadcasted_iota(jnp.int32, (rows, HEAD_W), 1)
    qm = jnp.where(c // DK_ATT == r // H_ATT, q16, jnp.zeros_like(q16))
    keys = PAGE_SIZE * H_ATT
    own_head = (lax.broadcasted_iota(jnp.int32, (rows, keys), 1) % H_ATT
                == lax.broadcasted_iota(jnp.int32, (rows, keys), 0) % H_ATT)

    s = [jnp.where(own_head, _dot_nt(qm, k_refs[j][0].reshape(keys, HEAD_W).astype(BF16)), NEG_INF)
         for j in range(n_pages)]
    k_new = jnp.concatenate([kn_ref[0], kn_ref[0]], axis=0)
    v_new = jnp.concatenate([vn_ref[0], vn_ref[0]], axis=0)
    s_new = jnp.sum(qm.astype(F32) * k_new, axis=-1, keepdims=True)
    m = jnp.maximum(jnp.max(functools.reduce(jnp.maximum, s), axis=-1, keepdims=True), s_new)
    p = [jnp.exp2(sj - m) for sj in s]
    p_new = jnp.exp2(s_new - m)
    l = jnp.sum(functools.reduce(jnp.add, p), axis=-1, keepdims=True) + p_new
    acc = p_new * v_new
    for j in range(n_pages):
        acc = acc + _dot(p[j].astype(BF16), v_refs[j][0].reshape(keys, HEAD_W).astype(BF16))
    o = acc / l
    od = o[:H_ATT] - lam_ref[...] * o[H_ATT:]
    o_ref[0] = _rms(od, g_ref[...]) * out_scale


def attn_sample(q, k_new, v_new, cache_k, cache_v, page_table, lam, att_gain, out_scale):
    DB = q.shape[0]
    n_pages = page_table.shape[1]
    tok =pl.BlockSpec((1, H_ATT, HEAD_W), lambda b, pt: (b, 0, 0))
    vec = pl.BlockSpec((1, HEAD_W), lambda b, pt: (0, 0))
    page = lambda j: pl.BlockSpec((1, PAGE_SIZE, H_ATT, HEAD_W),
                                  lambda b, pt: (pt[b * n_pages + j], 0, 0, 0))
    pages = [page(j) for j in range(n_pages)]
    return pl.pallas_call(
        functools.partial(_attn_sample_kernel, n_pages=n_pages, out_scale=out_scale),
        grid_spec=pltpu.PrefetchScalarGridSpec(
            num_scalar_prefetch=1,
            grid=(DB,),
            in_specs=[vec, tok, tok, tok, vec] + pages + pages,
            out_specs=tok,
        ),
        out_shape=jax.ShapeDtypeStruct((DB, H_ATT, HEAD_W), F32),
        compiler_params=_cparams("arbitrary"),
    )(page_table.reshape(-1), lam, q, k_new, v_new, att_gain,
      *([cache_k] * n_pages), *([cache_v] * n_pages))


def _gates(zf, lb):
    f = lb + (1.0 - lb) * jax.nn.sigmoid(zf)
    return jnp.log(f), (1.0 - lb) * jax.nn.sigmoid(-zf)


def _hgrn_prompt_kernel(q_ref, f_ref, i_ref, g_ref, lb_ref, gain_ref, tril_ref, o_ref, st_ref, s_scr,
                        *, rows, chunk):
    t = pl.program_id(1)

    @pl.when(t == 0)
    def _():
        s_scr[...] = jnp.zeros_like(s_scr)

    tril = tril_ref[...]
    r = lax.broadcasted_iota(jnp.int32, (rows, rows), 0)
    c = lax.broadcasted_iota(jnp.int32, (rows, rows), 1)
    band = (c <= r) & (c // chunk == r // chunk)
    mid = chunk // 2 - 1
    n_chunks = rows // chunk

    def head(h, _):
        sl = pl.ds(pl.multiple_of(h * HEAD_W, HEAD_W), HEAD_W)
        log_f, kk = _gates(f_ref[:, sl], lb_ref[:, sl])
        hi, md, lo = _split3(log_f)
        bc = _dot(tril, hi) + _dot(tril, md) + _dot(tril, lo)
        q, g = q_ref[:, sl], g_ref[:, sl]
        v = i_ref[:, sl].astype(BF16)
        q_mid, k_mid, q_dec, k_end, decay = [], [], [], [], []
        for c in range(n_chunks):
            r0 = c * chunk
            bl = bc[r0:r0 + chunk]
            if c:
                bl = bl - bc[r0 - 1:r0]
            qc, kc = q[r0:r0 + chunk], kk[r0:r0 + chunk]
            b_mid, b_last = bl[mid:mid + 1], bl[chunk - 1:chunk]
            q_mid.append(qc * jnp.exp(bl - b_mid))
            k_mid.append(kc * jnp.exp(b_mid - bl))
            q_dec.append((qc * jnp.exp(bl)).astype(BF16))
            k_end.append((kc * jnp.exp(b_last - bl)).astype(BF16))
            decay.append(jnp.exp(b_last))
        a = _dot_nt(jnp.concatenate(q_mid, axis=0).astype(BF16), jnp.concatenate(k_mid, axis=0).astype(BF16))
        intra = _dot(jnp.where(band, a, 0.0).astype(BF16), v)
        update = [_dot_tn(v[c * chunk:(c + 1) * chunk], k_end[c]) for c in range(n_chunks)]
        st = s_scr[h]
        for c in range(n_chunks):
            r0 = c * chunk
            o = intra[r0:r0 + chunk] + _dot_nt(q_dec[c], st.astype(BF16))
            st = st * decay[c] + update[c]
            y = _rms(o, gain_ref[...]) * (g[r0:r0 + chunk] * jax.nn.sigmoid(g[r0:r0 + chunk]))
            o_ref[r0:r0 + chunk, sl] = y.astype(BF16)
        s_scr[h] = st
        return 0

    lax.fori_loop(0, H_RNN, head, 0, unroll=2)

    @pl.when(t == pl.num_programs(1) - 1)
    def _():
        for h in range(H_RNN):
            st_ref[0, h] = s_scr[h].T


def hgrn_prompt(z, lb, rnn_gain, B, S):
    rows = _tile(S, RNN_ROWS)
    chunk = _tile(rows, RNN_CHUNK)
    nt = S // rows
    tril = (jnp.arange(rows)[:, None] >= jnp.arange(rows)[None, :]).astype(BF16)
    col = lambda c: pl.BlockSpec((rows, D_HEADS), lambda b, t: (b * nt + t, c))
    return pl.pallas_call(
        functools.partial(_hgrn_prompt_kernel, rows=rows, chunk=chunk),
        grid=(B, nt),
        in_specs=[col(3), col(4), col(5), col(6),
                  pl.BlockSpec((1, D_HEADS), lambda b, t: (0, 0)),
                  pl.BlockSpec((1, HEAD_W), lambda b, t: (0, 0)),
                  pl.BlockSpec((rows, rows), lambda b, t: (0, 0))],
        out_specs=(pl.BlockSpec((rows, D_HEADS), lambda b, t: (b * nt + t, 0)),
                   pl.BlockSpec((1, H_RNN, HEAD_W, HEAD_W), lambda b, t: (b, 0, 0, 0))),
        out_shape=(jax.ShapeDtypeStruct((B * S, D_HEADS), BF16),
                   jax.ShapeDtypeStruct((B, H_RNN, HEAD_W, HEAD_W), F32)),
        scratch_shapes=[pltpu.VMEM((H_RNN, HEAD_W, HEAD_W), F32)],
        compiler_params=_cparams("parallel", "arbitrary"),
    )(z, z, z, z, lb, rnn_gain, tril)


def _hgrn_step_kernel(q_ref, f_ref, i_ref, g_ref, lb_ref, gain_ref, s0_ref, o_ref, s1_ref, *, bs):
    def column(rowvec):
        return jnp.broadcast_to(rowvec, (HEAD_W, HEAD_W)).T

    def head(h, _):
        sl = pl.ds(pl.multiple_of(h * HEAD_W, HEAD_W), HEAD_W)
        log_f, kk = _gates(f_ref[:, sl], lb_ref[:, sl])
        q, v, g = q_ref[:, sl], i_ref[:, sl], g_ref[:, sl]
        decay = jnp.exp(log_f)
        q_dec = q * decay
        a = jnp.sum(q_dec * (kk * jnp.exp(-log_f)), axis=-1, keepdims=True)
        inter = []
        for b in range(bs):
            s0 = s0_ref[b, h]
            inter.append(jnp.sum(column(q_dec[b:b + 1]) * s0, axis=0, keepdims=True))
            s1_ref[b, h] = column(decay[b:b + 1]) * s0 + column(kk[b:b + 1]) * v[b:b + 1]
        o = a * v + jnp.concatenate(inter, axis=0)
        o_ref[:, sl] = (_rms(o, gain_ref[...]) * (g * jax.nn.sigmoid(g))).astype(BF16)
        return 0

    lax.fori_loop(0, H_RNN, head, 0)


def hgrn_step(z, lb, rnn_gain, s0, bs_pref=8):
    DB = z.shape[0]
    bs = _tile(DB, bs_pref)
    col = lambda c: pl.BlockSpec((bs, D_HEADS), lambda i: (i, c))
    st = pl.BlockSpec((bs, H_RNN, HEAD_W, HEAD_W), lambda i: (i, 0, 0, 0))
    return pl.pallas_call(
        functools.partial(_hgrn_step_kernel, bs=bs),
        grid=(DB // bs,),
        in_specs=[col(3), col(4), col(5), col(6),
                  pl.BlockSpec((1, D_HEADS), lambda i: (0, 0)),
                  pl.BlockSpec((1, HEAD_W), lambda i: (0, 0)), st],
        out_specs=(pl.BlockSpec((bs, D_HEADS), lambda i: (i, 0)), st),
        out_shape=(jax.ShapeDtypeStruct((DB, D_HEADS), BF16), jax.ShapeDtypeStruct(s0.shape, F32)),
        compiler_params=_cparams("parallel"),
    )(z, z, z, z, lb, rnn_gain, s0)


def _out_proj_kernel(a_ref, r_ref, wa_ref, wr_ref, x_ref, g_ref, x1_ref, h1_ref):
    x1 = x_ref[...] + _dot(a_ref[...], wa_ref[...]) + _dot(r_ref[...], wr_ref[...])
    x1_ref[...] = x1
    h1_ref[...] = _rms(x1, g_ref[...]).astype(BF16)


def out_proj(o_att, o_rnn, w_out, x, ffn_gain, tm_pref=256):
    M, D = x.shape
    tm = _tile(M, tm_pref)
    act = pl.BlockSpec((tm, D_HEADS), lambda i: (i, 0))
    full = pl.BlockSpec((tm, D), lambda i: (i, 0))
    return pl.pallas_call(
        _out_proj_kernel,
        grid=(M // tm,),
        in_specs=[act, act,
                  pl.BlockSpec((D_HEADS, D), lambda i: (0, 0)),
                  pl.BlockSpec((D_HEADS, D), lambda i: (1, 0)),
                  full, pl.BlockSpec((1, D), lambda i: (0, 0))],
        out_specs=(full, full),
        out_shape=(jax.ShapeDtypeStruct((M, D), F32), jax.ShapeDtypeStruct((M, D), BF16)),
        compiler_params=_cparams("parallel"),
    )(o_att, o_rnn, w_out, w_out, x, ffn_gain)


def _ffn_seq_kernel(h_ref, wg_ref, wu_ref, cw_ref, cb_ref, act_ref, tail_ref, carry_ref, *, tiles_per_seq):
    m = pl.program_id(1)

    @pl.when(m % tiles_per_seq == 0)
    def _():
        carry_ref[...] = jnp.zeros_like(carry_ref)

    h = h_ref[...]
    g = _dot(h, wg_ref[...])
    u = _dot(h, wu_ref[...])
    tm = g.shape[0]
    r = lax.broadcasted_iota(jnp.int32, g.shape, 0)
    prev2, prev1 = carry_ref[0:1], carry_ref[1:2]
    g1 = jnp.where(r == 0, prev1, pltpu.roll(g, 1, 0))
    g2 = jnp.where(r == 0, prev2, jnp.where(r == 1, prev1, pltpu.roll(g, 2, 0)))
    cw = cw_ref[...]
    c = cb_ref[...] + g2 * cw[0:1] + g1 * cw[1:2] + g * cw[2:3]
    act_ref[...] = (c * jax.nn.sigmoid(c) * u).astype(BF16)
    tail = g[tm - (CONV_W - 1):]
    carry_ref[0:CONV_W - 1] = tail
    tail_ref[0] = tail


def ffn_seq(h, w_gate, w_up, conv_w, conv_b, B, S, tm_pref=512, tn_pref=512):
    M, D = h.shape
    F = w_gate.shape[1]
    tm, tn = _tile(S, tm_pref), _tile(F, tn_pref)
    tiles_per_seq = S // tm
    wspec = pl.BlockSpec((D, tn), lambda n, m: (0, n))
    return pl.pallas_call(
        functools.partial(_ffn_seq_kernel, tiles_per_seq=tiles_per_seq),
        grid=(F // tn, M // tm),
        in_specs=[pl.BlockSpec((tm, D), lambda n, m: (m, 0)), wspec, wspec,
                  pl.BlockSpec((CONV_W, tn), lambda n, m: (0, n)),
                  pl.BlockSpec((1, tn), lambda n, m: (0, n))],
        out_specs=(pl.BlockSpec((tm, tn), lambda n, m: (m, n)),
                   pl.BlockSpec((1, CONV_W - 1, tn), lambda n, m: (m // tiles_per_seq, 0, n))),
        out_shape=(jax.ShapeDtypeStruct((M, F), BF16), jax.ShapeDtypeStruct((B, CONV_W - 1, F), F32)),
        scratch_shapes=[pltpu.VMEM((8, tn), F32)],
        compiler_params=_cparams("parallel", "arbitrary"),
    )(h, w_gate, w_up, conv_w, conv_b)


def _ffn_step_kernel(h_ref, wg_ref, wu_ref, cw_ref, cb_ref, p2_ref, p1_ref, act_ref, g_ref):
    h = h_ref[...]
    g = _dot(h, wg_ref[...])
    u = _dot(h, wu_ref[...])
    cw = cw_ref[...]
    c = cb_ref[...] + p2_ref[...] * cw[0:1] + p1_ref[...] * cw[1:2] + g * cw[2:3]
    act_ref[...] = (c * jax.nn.sigmoid(c) * u).astype(BF16)
    g_ref[...] = g


def ffn_step(h, w_gate, w_up, conv_w, conv_b, prev2, prev1, tn_pref=512):
    M, D = h.shape
    F = w_gate.shape[1]
    tn = _tile(F, tn_pref)
    wspec = pl.BlockSpec((D, tn), lambda n: (0, n))
    tile = pl.BlockSpec((M, tn), lambda n: (0, n))
    return pl.pallas_call(
        _ffn_step_kernel,
        grid=(F // tn,),
        in_specs=[pl.BlockSpec((M, D), lambda n: (0, 0)), wspec, wspec,
                  pl.BlockSpec((CONV_W, tn), lambda n: (0, n)),
                  pl.BlockSpec((1, tn), lambda n: (0, n)), tile, tile],
        out_specs=(tile, tile),
        out_shape=(jax.ShapeDtypeStruct((M, F), BF16), jax.ShapeDtypeStruct((M, F), F32)),
        compiler_params=_cparams("parallel"),
    )(h, w_gate, w_up, conv_w, conv_b, prev2, prev1)


def _down_kernel(a_ref, w_ref, x_ref, o_ref):
    o_ref[...] = x_ref[...] + _dot(a_ref[...], w_ref[...])


def ffn_down(act, w_down, x, tm_pref=512, tn_pref=512):
    M, F = act.shape
    D = w_down.shape[1]
    tm, tn = _tile(M, tm_pref), _tile(D, tn_pref)
    tile = pl.BlockSpec((tm, tn), lambda i, j: (i, j))
    return pl.pallas_call(
        _down_kernel,
        grid=(M // tm, D // tn),
        in_specs=[pl.BlockSpec((tm, F), lambda i, j: (i, 0)),
                  pl.BlockSpec((F, tn), lambda i, j: (0, j)), tile],
        out_specs=tile,
        out_shape=jax.ShapeDtypeStruct((M, D), F32),
        compiler_params=_cparams("parallel", "arbitrary"),
    )(act, w_down, x)


def _ple_kernel(x_ref, g_ref, wg_ref, p_ref, wp_ref, o_ref, h_ref, *, tn):
    j = pl.program_id(1)

    @pl.when(j == 0)
    def _():
        h_ref[...] = _rms(x_ref[...], g_ref[...]).astype(BF16)

    gate = jax.nn.sigmoid(_dot(h_ref[...], wg_ref[...]))
    emb = _dot(p_ref[...].astype(BF16), wp_ref[...])
    o_ref[...] = x_ref[:, pl.ds(pl.multiple_of(j * tn, tn), tn)] + gate * emb


def ple(x, gain, w_gate, p, w_proj, tm_pref=512, tn_pref=512):
    M, D = x.shape
    DP = p.shape[1]
    tm, tn = _tile(M, tm_pref), _tile(D, tn_pref)
    return pl.pallas_call(
        functools.partial(_ple_kernel, tn=tn),
        grid=(M // tm, D // tn),
        in_specs=[pl.BlockSpec((tm, D), lambda i, j: (i, 0)),
                  pl.BlockSpec((1, D), lambda i, j: (0, 0)),
                  pl.BlockSpec((D, tn), lambda i, j: (0, j)),
                  pl.BlockSpec((tm, DP), lambda i, j: (i, 0)),
                  pl.BlockSpec((DP, tn), lambda i, j: (0, j))],
        out_specs=pl.BlockSpec((tm, tn), lambda i, j: (i, j)),
        out_shape=jax.ShapeDtypeStruct((M, D), F32),
        scratch_shapes=[pltpu.VMEM((tm, D), BF16)],
        compiler_params=_cparams("parallel", "arbitrary"),
    )(x, gain, w_gate, p, w_proj)


def _rope_tables(pos):
    half = DK_ATT // 2
    inv_freq = 1.0 / (ROPE_THETA ** (jnp.arange(half, dtype=F32) * (2.0 / DK_ATT)))
    ang = pos.astype(F32)[:, None] * inv_freq[None, :]
    cos, sin = jnp.cos(ang), jnp.sin(ang)
    return (jnp.concatenate([cos] * 4, axis=-1), jnp.concatenate([-sin, sin] * 2, axis=-1))


def _row(v):
    return v.reshape(1, -1).astype(F32)


def kernel(x_prompt, x_sample, p_prompt, p_sample, cache_k, cache_v, state_rnn, state_ffn_conv, page_table,
           norm_mix, w_in, q_norm, k_norm, lam_q1, lam_k1, lam_q2, lam_k2, att_out_norm, rnn_out_norm,
           lower_bounds, w_out, norm_ffn, w_gate, w_up, conv_w, conv_b, w_down, norm_ple, w_ple_gate,
           w_ple_proj):
    B, S, D = x_prompt.shape
    DB, T, _ = x_sample.shape
    assert T == 1, "the sample group is implemented for one new token per sequence"
    depth = w_in.shape[0]
    past = page_table.shape[1] * PAGE_SIZE
    cos_p, sin_p = _rope_tables(jnp.arange(S, dtype=jnp.int32))
    cos_p, sin_p = jnp.tile(cos_p, (B, 1)), jnp.tile(sin_p, (B, 1))
    cos_s, sin_s = _rope_tables(jnp.full((DB,), past, dtype=jnp.int32))

    yp = x_prompt.reshape(B * S, D)
    ys = x_sample.reshape(DB, D)
    outs = [[] for _ in range(8)]
    for i in range(depth):
        lam_init = 0.8 - 0.6 * math.exp(-0.3 * i)
        out_scale = 1.0 - lam_init
        lam, lb = layer_params(jnp.stack([lam_q1[i], lam_k1[i], lam_q2[i], lam_k2[i]]).astype(F32),
                               lower_bounds.astype(F32), i, lam_init)
        w_in_b, w_out_b = w_in[i].astype(BF16), w_out[i].astype(BF16)
        w_gate_b, w_up_b, w_down_b = w_gate[i].astype(BF16), w_up[i].astype(BF16), w_down[i].astype(BF16)
        w_pg_b, w_pp_b = w_ple_gate[i].astype(BF16), w_ple_proj[i].astype(BF16)
        q_gain = jnp.tile(_row(q_norm[i]), (1, 2))
        k_gain = jnp.tile(_row(k_norm[i]), (1, 2))
        att_gain, rnn_gain = _row(att_out_norm[i]), _row(rnn_out_norm[i])

        def mix_in(x, cos, sin, seqs=None):
            z = norm_matmul(x, _row(norm_mix[i]), w_in_b)
            return (z,) + tuple(qk_prep(z, cos, sin, q_gain, k_gain, seqs))

        def mix_out(x, o_att, o_rnn):
            return out_proj(o_att, o_rnn, w_out_b, x, _row(norm_ffn[i]))

        def embed(x, act, p):
            x = ffn_down(act, w_down_b, x)
            return ple(x, _row(norm_ple[i]), w_pg_b, p, w_pp_b)

        z, qt, k, kb, v, vt = mix_in(yp, cos_p, sin_p, (B, S))
        o_att = attn_prompt(qt, kb.reshape(B, S, -1), vt, lam, att_gain, out_scale).reshape(B * S, -1)
        o_rnn, rnn_p = hgrn_prompt(z, lb, rnn_gain, B, S)
        x1, h1 = mix_out(yp, o_att, o_rnn)
        act, conv_p = ffn_seq(h1, w_gate_b, w_up_b, conv_w[i], _row(conv_b[i]), B, S)
        yp = embed(x1, act, p_prompt[i].reshape(B * S, -1))
        outs[0].append(k.reshape(B, S // PAGE_SIZE, PAGE_SIZE, H_ATT, HEAD_W))
        outs[1].append(v.reshape(B, S // PAGE_SIZE, PAGE_SIZE, H_ATT, HEAD_W))
        outs[2].append(rnn_p.astype(state_rnn.dtype))
        outs[3].append(conv_p)

        z, q, k, kb, v, vb = mix_in(ys, cos_s, sin_s)
        per_head = lambda a: a.reshape(DB, H_ATT, HEAD_W)
        o_att = attn_sample(per_head(q), per_head(k), per_head(v), cache_k[i], cache_v[i], page_table,
                            lam, att_gain, out_scale).reshape(DB, D_HEADS).astype(BF16)
        o_rnn, rnn_s = hgrn_step(z, lb, rnn_gain, state_rnn[i].astype(F32))
        x1, h1 = mix_out(ys, o_att, o_rnn)
        buf = state_ffn_conv[i]
        act, g_new = ffn_step(h1, w_gate_b, w_up_b, conv_w[i], _row(conv_b[i]), buf[:, 0], buf[:, 1])
        ys = embed(x1, act, p_sample[i].reshape(DB, -1))
        outs[4].append(k.reshape(DB, T, H_ATT, HEAD_W))
        outs[5].append(v.reshape(DB, T, H_ATT, HEAD_W))
        outs[6].append(rnn_s.astype(state_rnn.dtype))
        outs[7].append(jnp.stack([buf[:, 1], g_new], axis=1))

    return (yp.reshape(B, S, D), ys.reshape(DB, T, D)) + tuple(jnp.stack(o) for o in outs)
```

```python
import functools
import math

import jax
import jax.numpy as jnp
from jax import lax
from jax.experimental import pallas as pl
from jax.experimental.pallas import tpu as pltpu

H_ATT = 8
DK_ATT = 64
H_RNN = 8
HEAD_W = 128
D_HEADS = H_ATT * HEAD_W
PAGE_SIZE = 128
CONV_W = 3
ROPE_THETA = 10000.0
EPS = 1e-6
NEG_INF = -1e30
RNN_CHUNK = 32
RNN_ROWS = 256
ATT_HEADS_PER_STEP = 4
VT_ROWS = HEAD_W + 16
Q_SCALE = DK_ATT ** -0.5 * math.log2(math.e)
VMEM_LIMIT_V7X = 56 * 1024 * 1024

F32 = jnp.float32
BF16 = jnp.bfloat16


def _cparams(*sem):
    return pltpu.CompilerParams(dimension_semantics=sem, vmem_limit_bytes=VMEM_LIMIT_V7X)


def _tile(n, pref):
    t = min(n, pref)
    while n % t:
        t //= 2
    return t


def _split3(x):
    hi = x.astype(BF16)
    r1 = x - hi.astype(F32)
    mid = r1.astype(BF16)
    lo = (r1 - mid.astype(F32)).astype(BF16)
    return hi, mid, lo


def _dot(a, b):
    return jnp.dot(a, b, preferred_element_type=F32)


def _dot_nt(a, b):
    return lax.dot_general(a, b, (((1,), (1,)), ((), ())), preferred_element_type=F32)


def _dot_tn(a, b):
    return lax.dot_general(a, b, (((0,), (0,)), ((), ())), preferred_element_type=F32)


def _rms(x, gain):
    return x * lax.rsqrt(jnp.mean(x * x, axis=-1, keepdims=True) + EPS) * gain


def _params_kernel(lam_ref, lb_in_ref, lam_out_ref, lb_out_ref, *, layer, lam_init):
    v = lam_ref[...]
    s1 = jnp.sum(v[0:1] * v[1:2], axis=-1, keepdims=True)
    s2 = jnp.sum(v[2:3] * v[3:4], axis=-1, keepdims=True)
    lam = jnp.exp(s1) - jnp.exp(s2) + lam_init
    lam_out_ref[...] = jnp.broadcast_to(lam, lam_out_ref.shape)
    lb = lb_in_ref[...]
    e = jnp.exp(lb - jnp.max(lb, axis=0, keepdims=True))
    sm = e / jnp.sum(e, axis=0, keepdims=True)
    lb_out_ref[...] = jnp.sum(sm[0:layer + 1], axis=0, keepdims=True)


def layer_params(lam_vecs, lower_bounds, layer, lam_init):
    return pl.pallas_call(
        functools.partial(_params_kernel, layer=layer, lam_init=lam_init),
        out_shape=(jax.ShapeDtypeStruct((1, HEAD_W), F32),
                   jax.ShapeDtypeStruct((1, lower_bounds.shape[1]), F32)),
    )(lam_vecs, lower_bounds)


def _norm_matmul_kernel(x_ref, g_ref, w_ref, o_ref, h_ref):
    @pl.when(pl.program_id(1) == 0)
    def _():
        h_ref[...] = _rms(x_ref[...], g_ref[...]).astype(BF16)

    o_ref[...] = _dot(h_ref[...], w_ref[...])


def norm_matmul(x, gain, w, tm_pref=1024, tn_pref=1024):
    M, K = x.shape
    N = w.shape[1]
    tm, tn = _tile(M, tm_pref), _tile(N, tn_pref)
    return pl.pallas_call(
        _norm_matmul_kernel,
        grid=(M // tm, N // tn),
        in_specs=[pl.BlockSpec((tm, K), lambda i, j: (i, 0)),
                  pl.BlockSpec((1, K), lambda i, j: (0, 0)),
                  pl.BlockSpec((K, tn), lambda i, j: (0, j))],
        out_specs=pl.BlockSpec((tm, tn), lambda i, j: (i, j)),
        out_shape=jax.ShapeDtypeStruct((M, N), F32),
        scratch_shapes=[pltpu.VMEM((tm, K), BF16)],
        compiler_params=_cparams("parallel", "arbitrary"),
    )(x, gain, w)


def _qk_prep_kernel(zq_ref, zk_ref, zv_ref, cos_ref, sin_ref, qg_ref, kg_ref, grp_ref,
                    q_ref, k_ref, kb_ref, v_ref, vb_ref, *, transposed):
    cos = cos_ref[...]
    sin = sin_ref[...]
    lane = lax.broadcasted_iota(jnp.int32, cos.shape, 1)
    first_half = (lane % DK_ATT) < (DK_ATT // 2)
    grp = grp_ref[...]

    def norm_rope(x, gain):
        hi, mid, lo = _split3(x * x)
        ss = (_dot(hi, grp) + _dot(mid, grp) + _dot(lo, grp)) * (1.0 / DK_ATT)
        y = x * lax.rsqrt(ss + EPS) * gain
        swapped = jnp.where(first_half, pltpu.roll(y, HEAD_W - DK_ATT // 2, 1),
                            pltpu.roll(y, DK_ATT // 2, 1))
        return y * cos + swapped * sin

    for h in range(H_ATT):
        sl = slice(h * HEAD_W, (h + 1) * HEAD_W)
        q = norm_rope(zq_ref[:, sl], qg_ref[...]) * Q_SCALE
        k = norm_rope(zk_ref[:, sl], kg_ref[...])
        k_ref[:, sl] = k
        kb_ref[:, sl] = k.astype(BF16)
        v = zv_ref[:, sl]
        v_ref[:, sl] = v
        if transposed:
            q_ref[0, sl, :] = q.T.astype(BF16)
            r0 = h * VT_ROWS
            vb_ref[0, r0:r0 + HEAD_W, :] = v.T.astype(BF16)
            pad = (VT_ROWS - HEAD_W, v.shape[0])
            vb_ref[0, r0 + HEAD_W:r0 + VT_ROWS, :] = (lax.broadcasted_iota(jnp.int32, pad, 0) == 0).astype(BF16)
        else:
            q_ref[:, sl] = q.astype(BF16)
            vb_ref[:, sl] = v.astype(BF16)


def qk_prep(z, cos, sin, q_gain, k_gain, seqs=None, tm_pref=512):
    M = z.shape[0]
    tm = _tile(M if seqs is None else seqs[1], tm_pref)
    lane = jnp.arange(HEAD_W)
    grp = (lane[:, None] // DK_ATT == lane[None, :] // DK_ATT).astype(BF16)
    col = lambda c: pl.BlockSpec((tm, D_HEADS), lambda i: (i, c))
    row = pl.BlockSpec((tm, HEAD_W), lambda i: (i, 0))
    vec = pl.BlockSpec((1, HEAD_W), lambda i: (0, 0))
    out = pl.BlockSpec((tm, D_HEADS), lambda i: (i, 0))
    sd = lambda dt: jax.ShapeDtypeStruct((M, D_HEADS), dt)
    if seqs is None:
        out_q, out_v, sd_q, sd_v = out, out, sd(BF16), sd(BF16)
    else:
        B, S = seqs
        tiles = S // tm
        out_q = pl.BlockSpec((1, D_HEADS, tm), lambda i: (i // tiles, 0, i % tiles))
        out_v = pl.BlockSpec((1, H_ATT * VT_ROWS, tm), lambda i: (i // tiles, 0, i % tiles))
        sd_q = jax.ShapeDtypeStruct((B, D_HEADS, S), BF16)
        sd_v = jax.ShapeDtypeStruct((B, H_ATT * VT_ROWS, S), BF16)
    return pl.pallas_call(
        functools.partial(_qk_prep_kernel, transposed=seqs is not None),
        grid=(M // tm,),
        in_specs=[col(0), col(1), col(2), row, row, vec, vec,
                  pl.BlockSpec((HEAD_W, HEAD_W), lambda i: (0, 0))],
        out_specs=(out_q, out, out, out, out_v),
        out_shape=(sd_q, sd(F32), sd(BF16), sd(F32), sd_v),
        compiler_params=_cparams("parallel"),
    )(z, z, z, cos, sin, q_gain, k_gain, grp)


def _attn_prompt_kernel(lam_ref, qt_ref, k_ref, vt_ref, g_ref, o_ref, m_scr, acc_scr, *, tq, tk, out_scale):
    qi = pl.program_id(2)
    n_full = (qi * tq) // tk
    dim = lax.broadcasted_iota(jnp.int32, (HEAD_W, tq), 0)
    q2t = []
    for u in range(ATT_HEADS_PER_STEP):
        qt = qt_ref[0, u * HEAD_W:(u + 1) * HEAD_W, :]
        zero = jnp.zeros_like(qt)
        q2t.append(jnp.concatenate([jnp.where(dim < DK_ATT, qt, zero),
                                    jnp.where(dim >= DK_ATT, qt, zero)], axis=1))
    m_scr[...] = jnp.full(m_scr.shape, NEG_INF, F32)
    acc_scr[...] = jnp.zeros(acc_scr.shape, F32)

    def block(kb, masked):
        start = pl.multiple_of(kb * tk, tk)

        def scores(u):
            return _dot(k_ref[0, pl.ds(start, tk), u * HEAD_W:(u + 1) * HEAD_W], q2t[u])

        s_next = scores(0)
        for u in range(ATT_HEADS_PER_STEP):
            s = s_next
            if u + 1 < ATT_HEADS_PER_STEP:
                s_next = scores(u + 1)
            if masked:
                key = start + lax.broadcasted_iota(jnp.int32, s.shape, 0)
                qry = qi * tq + lax.broadcasted_iota(jnp.int32, s.shape, 1) % tq
                s = jnp.where(key <= qry, s, NEG_INF)
            m_prev = m_scr[u]
            m_new = jnp.maximum(m_prev, jnp.max(s, axis=0, keepdims=True))
            m_scr[u] = m_new
            p = jnp.exp2(s - m_new).astype(BF16)
            vt = vt_ref[0, u * VT_ROWS:(u + 1) * VT_ROWS, pl.ds(start, tk)]
            acc_scr[u] = jnp.exp2(m_prev - m_new) * acc_scr[u] + _dot(vt, p)

    def body(kb, carry):
        block(kb, False)
        return carry

    lax.fori_loop(0, n_full, body, 0)
    block(n_full, True)
    for u in range(ATT_HEADS_PER_STEP):
        acc = acc_scr[u]
        o = acc[:HEAD_W] / acc[HEAD_W:HEAD_W + 1]
        od = o[:, :tq] - lam_ref[:, :1] * o[:, tq:]
        y = od * lax.rsqrt(jnp.mean(od * od, axis=0, keepdims=True) + EPS) * g_ref[...] * out_scale
        o_ref[0, :, u * HEAD_W:(u + 1) * HEAD_W] = y.T.astype(BF16)


def attn_prompt(qt, k, vt, lam, att_gain, out_scale, tq_pref=256, tk_pref=512):
    B, S, _ = k.shape
    tq = _tile(S, tq_pref)
    tk = max(tq, _tile(S, tk_pref))
    U = ATT_HEADS_PER_STEP
    return pl.pallas_call(
        functools.partial(_attn_prompt_kernel, tq=tq, tk=tk, out_scale=out_scale),
        grid=(B, H_ATT // U, S // tq),
        in_specs=[pl.BlockSpec((1, HEAD_W), lambda b, h, i: (0, 0)),
                  pl.BlockSpec((1, U * HEAD_W, tq), lambda b, h, i: (b, h, i)),
                  pl.BlockSpec((1, S, U * HEAD_W), lambda b, h, i: (b, 0, h)),
                  pl.BlockSpec((1, U * VT_ROWS, S), lambda b, h, i: (b, h, 0)),
                  pl.BlockSpec((HEAD_W, 1), lambda b, h, i: (0, 0))],
        out_specs=pl.BlockSpec((1, tq, U * HEAD_W), lambda b, h, i: (b, i, h)),
        out_shape=jax.ShapeDtypeStruct(k.shape, BF16),
        scratch_shapes=[pltpu.VMEM((U, 1, 2 * tq), F32), pltpu.VMEM((U, VT_ROWS, 2 * tq), F32)],
        compiler_params=_cparams("parallel", "parallel", "arbitrary"),
    )(lam, qt, k, vt, att_gain.reshape(HEAD_W, 1))


def _attn_sample_kernel(pt_ref, lam_ref, q_ref, kn_ref, vn_ref, g_ref, *refs, n_pages, out_scale):
    del pt_ref
    k_refs, v_refs, o_ref = refs[:n_pages], refs[n_pages:2 * n_pages], refs[2 * n_pages]
    rows = 2 * H_ATT
    q = q_ref[0]
    q16 = jnp.concatenate([q, q], axis=0)
    r = lax.broadcasted_iota(jnp.int32, (rows, HEAD_W), 0)
    c = lax.broadcasted_iota(jnp.int32, (rows, HEAD_W), 1)
    qm = jnp.where(c // DK_ATT == r // H_ATT, q16, jnp.zeros_like(q16))
    keys = PAGE_SIZE * H_ATT
    own_head = (lax.broadcasted_iota(jnp.int32, (rows, keys), 1) % H_ATT
                == lax.broadcasted_iota(jnp.int32, (rows, keys), 0) % H_ATT)

    s = [jnp.where(own_head, _dot_nt(qm, k_refs[j][0].reshape(keys, HEAD_W).astype(BF16)), NEG_INF)
         for j in range(n_pages)]
    k_new = jnp.concatenate([kn_ref[0], kn_ref[0]], axis=0)
    v_new = jnp.concatenate([vn_ref[0], vn_ref[0]], axis=0)
    s_new = jnp.sum(qm.astype(F32) * k_new, axis=-1, keepdims=True)
    m = jnp.maximum(jnp.max(functools.reduce(jnp.maximum, s), axis=-1, keepdims=True), s_new)
    p = [jnp.exp2(sj - m) for sj in s]
    p_new = jnp.exp2(s_new - m)
    l = jnp.sum(functools.reduce(jnp.add, p), axis=-1, keepdims=True) + p_new
    acc = p_new * v_new
    for j in range(n_pages):
        acc = acc + _dot(p[j].astype(BF16), v_refs[j][0].reshape(keys, HEAD_W).astype(BF16))
    o = acc / l
    od = o[:H_ATT] - lam_ref[...] * o[H_ATT:]
    o_ref[0] = _rms(od, g_ref[...]) * out_scale


def attn_sample(q, k_new, v_new, cache_k, cache_v, page_table, lam, att_gain, out_scale):
    DB = q.shape[0]
    n_pages = page_table.shape[1]
    tok =pl.BlockSpec((1, H_ATT, HEAD_W), lambda b, pt: (b, 0, 0))
    vec = pl.BlockSpec((1, HEAD_W), lambda b, pt: (0, 0))
    page = lambda j: pl.BlockSpec((1, PAGE_SIZE, H_ATT, HEAD_W),
                                  lambda b, pt: (pt[b * n_pages + j], 0, 0, 0))
    pages = [page(j) for j in range(n_pages)]
    return pl.pallas_call(
        functools.partial(_attn_sample_kernel, n_pages=n_pages, out_scale=out_scale),
        grid_spec=pltpu.PrefetchScalarGridSpec(
            num_scalar_prefetch=1,
            grid=(DB,),
            in_specs=[vec, tok, tok, tok, vec] + pages + pages,
            out_specs=tok,
        ),
        out_shape=jax.ShapeDtypeStruct((DB, H_ATT, HEAD_W), F32),
        compiler_params=_cparams("arbitrary"),
    )(page_table.reshape(-1), lam, q, k_new, v_new, att_gain,
      *([cache_k] * n_pages), *([cache_v] * n_pages))


def _gates(zf, lb):
    f = lb + (1.0 - lb) * jax.nn.sigmoid(zf)
    return jnp.log(f), (1.0 - lb) * jax.nn.sigmoid(-zf)


def _hgrn_prompt_kernel(q_ref, f_ref, i_ref, g_ref, lb_ref, gain_ref, tril_ref, o_ref, st_ref, s_scr,
                        *, rows, chunk):
    t = pl.program_id(1)

    @pl.when(t == 0)
    def _():
        s_scr[...] = jnp.zeros_like(s_scr)

    tril = tril_ref[...]
    r = lax.broadcasted_iota(jnp.int32, (rows, rows), 0)
    c = lax.broadcasted_iota(jnp.int32, (rows, rows), 1)
    band = (c <= r) & (c // chunk == r // chunk)
    mid = chunk // 2 - 1
    n_chunks = rows // chunk

    def head(h, _):
        sl = pl.ds(pl.multiple_of(h * HEAD_W, HEAD_W), HEAD_W)
        log_f, kk = _gates(f_ref[:, sl], lb_ref[:, sl])
        hi, md, lo = _split3(log_f)
        bc = _dot(tril, hi) + _dot(tril, md) + _dot(tril, lo)
        q, g = q_ref[:, sl], g_ref[:, sl]
        v = i_ref[:, sl].astype(BF16)
        q_mid, k_mid, q_dec, k_end, decay = [], [], [], [], []
        for c in range(n_chunks):
            r0 = c * chunk
            bl = bc[r0:r0 + chunk]
            if c:
                bl = bl - bc[r0 - 1:r0]
            qc, kc = q[r0:r0 + chunk], kk[r0:r0 + chunk]
            b_mid, b_last = bl[mid:mid + 1], bl[chunk - 1:chunk]
            q_mid.append(qc * jnp.exp(bl - b_mid))
            k_mid.append(kc * jnp.exp(b_mid - bl))
            q_dec.append((qc * jnp.exp(bl)).astype(BF16))
            k_end.append((kc * jnp.exp(b_last - bl)).astype(BF16))
            decay.append(jnp.exp(b_last))
        a = _dot_nt(jnp.concatenate(q_mid, axis=0).astype(BF16), jnp.concatenate(k_mid, axis=0).astype(BF16))
        intra = _dot(jnp.where(band, a, 0.0).astype(BF16), v)
        update = [_dot_tn(v[c * chunk:(c + 1) * chunk], k_end[c]) for c in range(n_chunks)]
        st = s_scr[h]
        for c in range(n_chunks):
            r0 = c * chunk
            o = intra[r0:r0 + chunk] + _dot_nt(q_dec[c], st.astype(BF16))
            st = st * decay[c] + update[c]
            y = _rms(o, gain_ref[...]) * (g[r0:r0 + chunk] * jax.nn.sigmoid(g[r0:r0 + chunk]))
            o_ref[r0:r0 + chunk, sl] = y.astype(BF16)
        s_scr[h] = st
        return 0

    lax.fori_loop(0, H_RNN, head, 0, unroll=2)

    @pl.when(t == pl.num_programs(1) - 1)
    def _():
        for h in range(H_RNN):
            st_ref[0, h] = s_scr[h].T


def hgrn_prompt(z, lb, rnn_gain, B, S):
    rows = _tile(S, RNN_ROWS)
    chunk = _tile(rows, RNN_CHUNK)
    nt = S // rows
    tril = (jnp.arange(rows)[:, None] >= jnp.arange(rows)[None, :]).astype(BF16)
    col = lambda c: pl.BlockSpec((rows, D_HEADS), lambda b, t: (b * nt + t, c))
    return pl.pallas_call(
        functools.partial(_hgrn_prompt_kernel, rows=rows, chunk=chunk),
        grid=(B, nt),
        in_specs=[col(3), col(4), col(5), col(6),
                  pl.BlockSpec((1, D_HEADS), lambda b, t: (0, 0)),
                  pl.BlockSpec((1, HEAD_W), lambda b, t: (0, 0)),
                  pl.BlockSpec((rows, rows), lambda b, t: (0, 0))],
        out_specs=(pl.BlockSpec((rows, D_HEADS), lambda b, t: (b * nt + t, 0)),
                   pl.BlockSpec((1, H_RNN, HEAD_W, HEAD_W), lambda b, t: (b, 0, 0, 0))),
        out_shape=(jax.ShapeDtypeStruct((B * S, D_HEADS), BF16),
                   jax.ShapeDtypeStruct((B, H_RNN, HEAD_W, HEAD_W), F32)),
        scratch_shapes=[pltpu.VMEM((H_RNN, HEAD_W, HEAD_W), F32)],
        compiler_params=_cparams("parallel", "arbitrary"),
    )(z, z, z, z, lb, rnn_gain, tril)


def _hgrn_step_kernel(q_ref, f_ref, i_ref, g_ref, lb_ref, gain_ref, s0_ref, o_ref, s1_ref, *, bs):
    def column(rowvec):
        return jnp.broadcast_to(rowvec, (HEAD_W, HEAD_W)).T

    def head(h, _):
        sl = pl.ds(pl.multiple_of(h * HEAD_W, HEAD_W), HEAD_W)
        log_f, kk = _gates(f_ref[:, sl], lb_ref[:, sl])
        q, v, g = q_ref[:, sl], i_ref[:, sl], g_ref[:, sl]
        decay = jnp.exp(log_f)
        q_dec = q * decay
        a = jnp.sum(q_dec * (kk * jnp.exp(-log_f)), axis=-1, keepdims=True)
        inter = []
        for b in range(bs):
            s0 = s0_ref[b, h]
            inter.append(jnp.sum(column(q_dec[b:b + 1]) * s0, axis=0, keepdims=True))
            s1_ref[b, h] = column(decay[b:b + 1]) * s0 + column(kk[b:b + 1]) * v[b:b + 1]
        o = a * v + jnp.concatenate(inter, axis=0)
        o_ref[:, sl] = (_rms(o, gain_ref[...]) * (g * jax.nn.sigmoid(g))).astype(BF16)
        return 0

    lax.fori_loop(0, H_RNN, head, 0)


def hgrn_step(z, lb, rnn_gain, s0, bs_pref=8):
    DB = z.shape[0]
    bs = _tile(DB, bs_pref)
    col = lambda c: pl.BlockSpec((bs, D_HEADS), lambda i: (i, c))
    st = pl.BlockSpec((bs, H_RNN, HEAD_W, HEAD_W), lambda i: (i, 0, 0, 0))
    return pl.pallas_call(
        functools.partial(_hgrn_step_kernel, bs=bs),
        grid=(DB // bs,),
        in_specs=[col(3), col(4), col(5), col(6),
                  pl.BlockSpec((1, D_HEADS), lambda i: (0, 0)),
                  pl.BlockSpec((1, HEAD_W), lambda i: (0, 0)), st],
        out_specs=(pl.BlockSpec((bs, D_HEADS), lambda i: (i, 0)), st),
        out_shape=(jax.ShapeDtypeStruct((DB, D_HEADS), BF16), jax.ShapeDtypeStruct(s0.shape, F32)),
        compiler_params=_cparams("parallel"),
    )(z, z, z, z, lb, rnn_gain, s0)


def _out_proj_kernel(a_ref, r_ref, wa_ref, wr_ref, x_ref, g_ref, x1_ref, h1_ref):
    x1 = x_ref[...] + _dot(a_ref[...], wa_ref[...]) + _dot(r_ref[...], wr_ref[...])
    x1_ref[...] = x1
    h1_ref[...] = _rms(x1, g_ref[...]).astype(BF16)


def out_proj(o_att, o_rnn, w_out, x, ffn_gain, tm_pref=512):
    M, D = x.shape
    tm = _tile(M, tm_pref)
    act = pl.BlockSpec((tm, D_HEADS), lambda i: (i, 0))
    full = pl.BlockSpec((tm, D), lambda i: (i, 0))
    return pl.pallas_call(
        _out_proj_kernel,
        grid=(M // tm,),
        in_specs=[act, act,
                  pl.BlockSpec((D_HEADS, D), lambda i: (0, 0)),
                  pl.BlockSpec((D_HEADS, D), lambda i: (1, 0)),
                  full, pl.BlockSpec((1, D), lambda i: (0, 0))],
        out_specs=(full, full),
        out_shape=(jax.ShapeDtypeStruct((M, D), F32), jax.ShapeDtypeStruct((M, D), BF16)),
        compiler_params=_cparams("parallel"),
    )(o_att, o_rnn, w_out, w_out, x, ffn_gain)


def _ffn_seq_kernel(h_ref, wg_ref, wu_ref, cw_ref, cb_ref, act_ref, tail_ref, carry_ref, *, tiles_per_seq):
    m = pl.program_id(1)

    @pl.when(m % tiles_per_seq == 0)
    def _():
        carry_ref[...] = jnp.zeros_like(carry_ref)

    h = h_ref[...]
    g = _dot(h, wg_ref[...])
    u = _dot(h, wu_ref[...])
    tm = g.shape[0]
    r = lax.broadcasted_iota(jnp.int32, g.shape, 0)
    prev2, prev1 = carry_ref[0:1], carry_ref[1:2]
    g1 = jnp.where(r == 0, prev1, pltpu.roll(g, 1, 0))
    g2 = jnp.where(r == 0, prev2, jnp.where(r == 1, prev1, pltpu.roll(g, 2, 0)))
    cw = cw_ref[...]
    c = cb_ref[...] + g2 * cw[0:1] + g1 * cw[1:2] + g * cw[2:3]
    act_ref[...] = (c * jax.nn.sigmoid(c) * u).astype(BF16)
    tail = g[tm - (CONV_W - 1):]
    carry_ref[0:CONV_W - 1] = tail
    tail_ref[0] = tail


def ffn_seq(h, w_gate, w_up, conv_w, conv_b, B, S, tm_pref=512, tn_pref=1408):
    M, D = h.shape
    F = w_gate.shape[1]
    tm, tn = _tile(S, tm_pref), _tile(F, tn_pref)
    tiles_per_seq = S // tm
    wspec = pl.BlockSpec((D, tn), lambda n, m: (0, n))
    return pl.pallas_call(
        functools.partial(_ffn_seq_kernel, tiles_per_seq=tiles_per_seq),
        grid=(F // tn, M // tm),
        in_specs=[pl.BlockSpec((tm, D), lambda n, m: (m, 0)), wspec, wspec,
                  pl.BlockSpec((CONV_W, tn), lambda n, m: (0, n)),
                  pl.BlockSpec((1, tn), lambda n, m: (0, n))],
        out_specs=(pl.BlockSpec((tm, tn), lambda n, m: (m, n)),
                   pl.BlockSpec((1, CONV_W - 1, tn), lambda n, m: (m // tiles_per_seq, 0, n))),
        out_shape=(jax.ShapeDtypeStruct((M, F), BF16), jax.ShapeDtypeStruct((B, CONV_W - 1, F), F32)),
        scratch_shapes=[pltpu.VMEM((8, tn), F32)],
        compiler_params=_cparams("parallel", "arbitrary"),
    )(h, w_gate, w_up, conv_w, conv_b)


def _ffn_step_kernel(h_ref, wg_ref, wu_ref, cw_ref, cb_ref, p2_ref, p1_ref, act_ref, g_ref):
    h = h_ref[...]
    g = _dot(h, wg_ref[...])
    u = _dot(h, wu_ref[...])
    cw = cw_ref[...]
    c = cb_ref[...] + p2_ref[...] * cw[0:1] + p1_ref[...] * cw[1:2] + g * cw[2:3]
    act_ref[...] = (c * jax.nn.sigmoid(c) * u).astype(BF16)
    g_ref[...] = g


def ffn_step(h, w_gate, w_up, conv_w, conv_b, prev2, prev1, tn_pref=1408):
    M, D = h.shape
    F = w_gate.shape[1]
    tn = _tile(F, tn_pref)
    wspec = pl.BlockSpec((D, tn), lambda n: (0, n))
    tile = pl.BlockSpec((M, tn), lambda n: (0, n))
    return pl.pallas_call(
        _ffn_step_kernel,
        grid=(F // tn,),
        in_specs=[pl.BlockSpec((M, D), lambda n: (0, 0)), wspec, wspec,
                  pl.BlockSpec((CONV_W, tn), lambda n: (0, n)),
                  pl.BlockSpec((1, tn), lambda n: (0, n)), tile, tile],
        out_specs=(tile, tile),
        out_shape=(jax.ShapeDtypeStruct((M, F), BF16), jax.ShapeDtypeStruct((M, F), F32)),
        compiler_params=_cparams("parallel"),
    )(h, w_gate, w_up, conv_w, conv_b, prev2, prev1)


def _down_kernel(a_ref, w_ref, x_ref, o_ref):
    o_ref[...] = x_ref[...] + _dot(a_ref[...], w_ref[...])


def ffn_down(act, w_down, x, tm_pref=1024, tn_pref=512):
    M, F = act.shape
    D = w_down.shape[1]
    tm, tn = _tile(M, tm_pref), _tile(D, tn_pref)
    tile = pl.BlockSpec((tm, tn), lambda i, j: (i, j))
    return pl.pallas_call(
        _down_kernel,
        grid=(M // tm, D // tn),
        in_specs=[pl.BlockSpec((tm, F), lambda i, j: (i, 0)),
                  pl.BlockSpec((F, tn), lambda i, j: (0, j)), tile],
        out_specs=tile,
        out_shape=jax.ShapeDtypeStruct((M, D), F32),
        compiler_params=_cparams("parallel", "arbitrary"),
    )(act, w_down, x)


def _ple_kernel(x_ref, g_ref, wg_ref, p_ref, wp_ref, o_ref, h_ref, *, tn):
    j = pl.program_id(1)

    @pl.when(j == 0)
    def _():
        h_ref[...] = _rms(x_ref[...], g_ref[...]).astype(BF16)

    gate = jax.nn.sigmoid(_dot(h_ref[...], wg_ref[...]))
    emb = _dot(p_ref[...].astype(BF16), wp_ref[...])
    o_ref[...] = x_ref[:, pl.ds(pl.multiple_of(j * tn, tn), tn)] + gate * emb


def ple(x, gain, w_gate, p, w_proj, tm_pref=1024, tn_pref=1024):
    M, D = x.shape
    DP = p.shape[1]
    tm, tn = _tile(M, tm_pref), _tile(D, tn_pref)
    return pl.pallas_call(
        functools.partial(_ple_kernel, tn=tn),
        grid=(M // tm, D // tn),
        in_specs=[pl.BlockSpec((tm, D), lambda i, j: (i, 0)),
                  pl.BlockSpec((1, D), lambda i, j: (0, 0)),
                  pl.BlockSpec((D, tn), lambda i, j: (0, j)),
                  pl.BlockSpec((tm, DP), lambda i, j: (i, 0)),
                  pl.BlockSpec((DP, tn), lambda i, j: (0, j))],
        out_specs=pl.BlockSpec((tm, tn), lambda i, j: (i, j)),
        out_shape=jax.ShapeDtypeStruct((M, D), F32),
        scratch_shapes=[pltpu.VMEM((tm, D), BF16)],
        compiler_params=_cparams("parallel", "arbitrary"),
    )(x, gain, w_gate, p, w_proj)


def _rope_tables(pos):
    half = DK_ATT // 2
    inv_freq = 1.0 / (ROPE_THETA ** (jnp.arange(half, dtype=F32) * (2.0 / DK_ATT)))
    ang = pos.astype(F32)[:, None] * inv_freq[None, :]
    cos, sin = jnp.cos(ang), jnp.sin(ang)
    return (jnp.concatenate([cos] * 4, axis=-1), jnp.concatenate([-sin, sin] * 2, axis=-1))


def _row(v):
    return v.reshape(1, -1).astype(F32)


def kernel(x_prompt, x_sample, p_prompt, p_sample, cache_k, cache_v, state_rnn, state_ffn_conv, page_table,
           norm_mix, w_in, q_norm, k_norm, lam_q1, lam_k1, lam_q2, lam_k2, att_out_norm, rnn_out_norm,
           lower_bounds, w_out, norm_ffn, w_gate, w_up, conv_w, conv_b, w_down, norm_ple, w_ple_gate,
           w_ple_proj):
    B, S, D = x_prompt.shape
    DB, T, _ = x_sample.shape
    assert T == 1, "the sample group is implemented for one new token per sequence"
    depth = w_in.shape[0]
    past = page_table.shape[1] * PAGE_SIZE
    cos_p, sin_p = _rope_tables(jnp.arange(S, dtype=jnp.int32))
    cos_p, sin_p = jnp.tile(cos_p, (B, 1)), jnp.tile(sin_p, (B, 1))
    cos_s, sin_s = _rope_tables(jnp.full((DB,), past, dtype=jnp.int32))

    yp = x_prompt.reshape(B * S, D)
    ys = x_sample.reshape(DB, D)
    outs = [[] for _ in range(8)]
    for i in range(depth):
        lam_init = 0.8 - 0.6 * math.exp(-0.3 * i)
        out_scale = 1.0 - lam_init
        lam, lb = layer_params(jnp.stack([lam_q1[i], lam_k1[i], lam_q2[i], lam_k2[i]]).astype(F32),
                               lower_bounds.astype(F32), i, lam_init)
        w_in_b, w_out_b = w_in[i].astype(BF16), w_out[i].astype(BF16)
        w_gate_b, w_up_b, w_down_b = w_gate[i].astype(BF16), w_up[i].astype(BF16), w_down[i].astype(BF16)
        w_pg_b, w_pp_b = w_ple_gate[i].astype(BF16), w_ple_proj[i].astype(BF16)
        q_gain = jnp.tile(_row(q_norm[i]), (1, 2))
        k_gain = jnp.tile(_row(k_norm[i]), (1, 2))
        att_gain, rnn_gain = _row(att_out_norm[i]), _row(rnn_out_norm[i])

        def mix_in(x, cos, sin, seqs=None):
            z = norm_matmul(x, _row(norm_mix[i]), w_in_b)
            return (z,) + tuple(qk_prep(z, cos, sin, q_gain, k_gain, seqs))

        def mix_out(x, o_att, o_rnn):
            return out_proj(o_att, o_rnn, w_out_b, x, _row(norm_ffn[i]))

        def embed(x, act, p):
            x = ffn_down(act, w_down_b, x)
            return ple(x, _row(norm_ple[i]), w_pg_b, p, w_pp_b)

        z, qt, k, kb, v, vt = mix_in(yp, cos_p, sin_p, (B, S))
        o_att = attn_prompt(qt, kb.reshape(B, S, -1), vt, lam, att_gain, out_scale).reshape(B * S, -1)
        o_rnn, rnn_p = hgrn_prompt(z, lb, rnn_gain, B, S)
        x1, h1 = mix_out(yp, o_att, o_rnn)
        act, conv_p = ffn_seq(h1, w_gate_b, w_up_b, conv_w[i], _row(conv_b[i]), B, S)
        yp = embed(x1, act, p_prompt[i].reshape(B * S, -1))
        outs[0].append(k.reshape(B, S // PAGE_SIZE, PAGE_SIZE, H_ATT, HEAD_W))
        outs[1].append(v.reshape(B, S // PAGE_SIZE, PAGE_SIZE, H_ATT, HEAD_W))
        outs[2].append(rnn_p.astype(state_rnn.dtype))
        outs[3].append(conv_p)

        z, q, k, kb, v, vb = mix_in(ys, cos_s, sin_s)
        per_head = lambda a: a.reshape(DB, H_ATT, HEAD_W)
        o_att = attn_sample(per_head(q), per_head(k), per_head(v), cache_k[i], cache_v[i], page_table,
                            lam, att_gain, out_scale).reshape(DB, D_HEADS).astype(BF16)
        o_rnn, rnn_s = hgrn_step(z, lb, rnn_gain, state_rnn[i].astype(F32))
        x1, h1 = mix_out(ys, o_att, o_rnn)
        buf = state_ffn_conv[i]
        act, g_new = ffn_step(h1, w_gate_b, w_up_b, conv_w[i], _row(conv_b[i]), buf[:, 0], buf[:, 1])
        ys = embed(x1, act, p_sample[i].reshape(DB, -1))
        outs[4].append(k.reshape(DB, T, H_ATT, HEAD_W))
        outs[5].append(v.reshape(DB, T, H_ATT, HEAD_W))
        outs[6].append(rnn_s.astype(state_rnn.dtype))
        outs[7].append(jnp.stack([buf[:, 1], g_new], axis=1))

    return (yp.reshape(B, S, D), ys.reshape(DB, T, D)) + tuple(jnp.stack(o) for o in outs)
```

```python
import functools
import math

import jax
import jax.numpy as jnp
from jax import lax
from jax.experimental import pallas as pl
from jax.experimental.pallas import tpu as pltpu

H_ATT = 8
DK_ATT = 64
H_RNN = 8
HEAD_W = 128
D_HEADS = H_ATT * HEAD_W
PAGE_SIZE = 128
CONV_W = 3
ROPE_THETA = 10000.0
EPS = 1e-6
NEG_INF = -1e30
RNN_CHUNK = 32
RNN_ROWS = 256
ATT_HEADS_PER_STEP = 4
VT_ROWS = HEAD_W + 16
Q_SCALE = DK_ATT ** -0.5 * math.log2(math.e)
VMEM_LIMIT_V7X = 56 * 1024 * 1024

F32 = jnp.float32
BF16 = jnp.bfloat16


def _cparams(*sem):
    return pltpu.CompilerParams(dimension_semantics=sem, vmem_limit_bytes=VMEM_LIMIT_V7X)


def _tile(n, pref):
    t = min(n, pref)
    while n % t:
        t //= 2
    return t


def _split3(x):
    hi = x.astype(BF16)
    r1 = x - hi.astype(F32)
    mid = r1.astype(BF16)
    lo = (r1 - mid.astype(F32)).astype(BF16)
    return hi, mid, lo


def _dot(a, b):
    return jnp.dot(a, b, preferred_element_type=F32)


def _dot_nt(a, b):
    return lax.dot_general(a, b, (((1,), (1,)), ((), ())), preferred_element_type=F32)


def _dot_tn(a, b):
    return lax.dot_general(a, b, (((0,), (0,)), ((), ())), preferred_element_type=F32)


def _rms(x, gain):
    return x * lax.rsqrt(jnp.mean(x * x, axis=-1, keepdims=True) + EPS) * gain


def _params_kernel(lam_ref, lb_in_ref, lam_out_ref, lb_out_ref, *, layer, lam_init):
    v = lam_ref[...]
    s1 = jnp.sum(v[0:1] * v[1:2], axis=-1, keepdims=True)
    s2 = jnp.sum(v[2:3] * v[3:4], axis=-1, keepdims=True)
    lam = jnp.exp(s1) - jnp.exp(s2) + lam_init
    lam_out_ref[...] = jnp.broadcast_to(lam, lam_out_ref.shape)
    lb = lb_in_ref[...]
    e = jnp.exp(lb - jnp.max(lb, axis=0, keepdims=True))
    sm = e / jnp.sum(e, axis=0, keepdims=True)
    lb_out_ref[...] = jnp.sum(sm[0:layer + 1], axis=0, keepdims=True)


def layer_params(lam_vecs, lower_bounds, layer, lam_init):
    return pl.pallas_call(
        functools.partial(_params_kernel, layer=layer, lam_init=lam_init),
        out_shape=(jax.ShapeDtypeStruct((1, HEAD_W), F32),
                   jax.ShapeDtypeStruct((1, lower_bounds.shape[1]), F32)),
    )(lam_vecs, lower_bounds)


def _norm_matmul_kernel(x_ref, g_ref, w_ref, o_ref, h_ref):
    @pl.when(pl.program_id(1) == 0)
    def _():
        h_ref[...] = _rms(x_ref[...], g_ref[...]).astype(BF16)

    o_ref[...] = _dot(h_ref[...], w_ref[...])


def norm_matmul(x, gain, w, tm_pref=1024, tn_pref=1024):
    M, K = x.shape
    N = w.shape[1]
    tm, tn = _tile(M, tm_pref), _tile(N, tn_pref)
    return pl.pallas_call(
        _norm_matmul_kernel,
        grid=(M // tm, N // tn),
        in_specs=[pl.BlockSpec((tm, K), lambda i, j: (i, 0)),
                  pl.BlockSpec((1, K), lambda i, j: (0, 0)),
                  pl.BlockSpec((K, tn), lambda i, j: (0, j))],
        out_specs=pl.BlockSpec((tm, tn), lambda i, j: (i, j)),
        out_shape=jax.ShapeDtypeStruct((M, N), F32),
        scratch_shapes=[pltpu.VMEM((tm, K), BF16)],
        compiler_params=_cparams("parallel", "arbitrary"),
    )(x, gain, w)


def _qk_prep_kernel(zq_ref, zk_ref, zv_ref, cos_ref, sin_ref, qg_ref, kg_ref, grp_ref,
                    q_ref, k_ref, kb_ref, v_ref, vb_ref, *, transposed):
    cos = cos_ref[...]
    sin = sin_ref[...]
    lane = lax.broadcasted_iota(jnp.int32, cos.shape, 1)
    first_half = (lane % DK_ATT) < (DK_ATT // 2)
    grp = grp_ref[...]

    def norm_rope(x, gain):
        hi, mid, lo = _split3(x * x)
        ss = (_dot(hi, grp) + _dot(mid, grp) + _dot(lo, grp)) * (1.0 / DK_ATT)
        y = x * lax.rsqrt(ss + EPS) * gain
        swapped = jnp.where(first_half, pltpu.roll(y, HEAD_W - DK_ATT // 2, 1),
                            pltpu.roll(y, DK_ATT // 2, 1))
        return y * cos + swapped * sin

    for h in range(H_ATT):
        sl = slice(h * HEAD_W, (h + 1) * HEAD_W)
        q = norm_rope(zq_ref[:, sl], qg_ref[...]) * Q_SCALE
        k = norm_rope(zk_ref[:, sl], kg_ref[...])
        k_ref[:, sl] = k
        kb_ref[:, sl] = k.astype(BF16)
        v = zv_ref[:, sl]
        v_ref[:, sl] = v
        if transposed:
            q_ref[0, sl, :] = q.T.astype(BF16)
            r0 = h * VT_ROWS
            vb_ref[0, r0:r0 + HEAD_W, :] = v.T.astype(BF16)
            pad = (VT_ROWS - HEAD_W, v.shape[0])
            vb_ref[0, r0 + HEAD_W:r0 + VT_ROWS, :] = (lax.broadcasted_iota(jnp.int32, pad, 0) == 0).astype(BF16)
        else:
            q_ref[:, sl] = q.astype(BF16)
            vb_ref[:, sl] = v.astype(BF16)


def qk_prep(z, cos, sin, q_gain, k_gain, seqs=None, tm_pref=512):
    M = z.shape[0]
    tm = _tile(M if seqs is None else seqs[1], tm_pref)
    lane = jnp.arange(HEAD_W)
    grp = (lane[:, None] // DK_ATT == lane[None, :] // DK_ATT).astype(BF16)
    col = lambda c: pl.BlockSpec((tm, D_HEADS), lambda i: (i, c))
    row = pl.BlockSpec((tm, HEAD_W), lambda i: (i, 0))
    vec = pl.BlockSpec((1, HEAD_W), lambda i: (0, 0))
    out = pl.BlockSpec((tm, D_HEADS), lambda i: (i, 0))
    sd = lambda dt: jax.ShapeDtypeStruct((M, D_HEADS), dt)
    if seqs is None:
        out_q, out_v, sd_q, sd_v = out, out, sd(BF16), sd(BF16)
    else:
        B, S = seqs
        tiles = S // tm
        out_q = pl.BlockSpec((1, D_HEADS, tm), lambda i: (i // tiles, 0, i % tiles))
        out_v = pl.BlockSpec((1, H_ATT * VT_ROWS, tm), lambda i: (i // tiles, 0, i % tiles))
        sd_q = jax.ShapeDtypeStruct((B, D_HEADS, S), BF16)
        sd_v = jax.ShapeDtypeStruct((B, H_ATT * VT_ROWS, S), BF16)
    return pl.pallas_call(
        functools.partial(_qk_prep_kernel, transposed=seqs is not None),
        grid=(M // tm,),
        in_specs=[col(0), col(1), col(2), row, row, vec, vec,
                  pl.BlockSpec((HEAD_W, HEAD_W), lambda i: (0, 0))],
        out_specs=(out_q, out, out, out, out_v),
        out_shape=(sd_q, sd(F32), sd(BF16), sd(F32), sd_v),
        compiler_params=_cparams("parallel"),
    )(z, z, z, cos, sin, q_gain, k_gain, grp)


def _attn_prompt_kernel(lam_ref, qt_ref, k_ref, vt_ref, g_ref, o_ref, m_scr, acc_scr, *, tq, tk, out_scale):
    qi = pl.program_id(2)
    n_full = (qi * tq) // tk
    dim = lax.broadcasted_iota(jnp.int32, (HEAD_W, tq), 0)
    q2t = []
    for u in range(ATT_HEADS_PER_STEP):
        qt = qt_ref[0, u * HEAD_W:(u + 1) * HEAD_W, :]
        zero = jnp.zeros_like(qt)
        q2t.append(jnp.concatenate([jnp.where(dim < DK_ATT, qt, zero),
                                    jnp.where(dim >= DK_ATT, qt, zero)], axis=1))
    m_scr[...] = jnp.full(m_scr.shape, NEG_INF, F32)
    acc_scr[...] = jnp.zeros(acc_scr.shape, F32)

    def block(kb, masked):
        start = pl.multiple_of(kb * tk, tk)

        def scores(u):
            return _dot(k_ref[0, pl.ds(start, tk), u * HEAD_W:(u + 1) * HEAD_W], q2t[u])

        s_next = scores(0)
        for u in range(ATT_HEADS_PER_STEP):
            s = s_next
            if u + 1 < ATT_HEADS_PER_STEP:
                s_next = scores(u + 1)
            if masked:
                key = start + lax.broadcasted_iota(jnp.int32, s.shape, 0)
                qry = qi * tq + lax.broadcasted_iota(jnp.int32, s.shape, 1) % tq
                s = jnp.where(key <= qry, s, NEG_INF)
            m_prev = m_scr[u]
            m_new = jnp.maximum(m_prev, jnp.max(s, axis=0, keepdims=True))
            m_scr[u] = m_new
            p = jnp.exp2(s - m_new).astype(BF16)
            vt = vt_ref[0, u * VT_ROWS:(u + 1) * VT_ROWS, pl.ds(start, tk)]
            acc_scr[u] = jnp.exp2(m_prev - m_new) * acc_scr[u] + _dot(vt, p)

    def body(kb, carry):
        block(kb, False)
        return carry

    lax.fori_loop(0, n_full, body, 0)
    block(n_full, True)
    for u in range(ATT_HEADS_PER_STEP):
        acc = acc_scr[u]
        o = acc[:HEAD_W] / acc[HEAD_W:HEAD_W + 1]
        od = o[:, :tq] - lam_ref[:, :1] * o[:, tq:]
        y = od * lax.rsqrt(jnp.mean(od * od, axis=0, keepdims=True) + EPS) * g_ref[...] * out_scale
        o_ref[0, :, u * HEAD_W:(u + 1) * HEAD_W] = y.T.astype(BF16)


def attn_prompt(qt, k, vt, lam, att_gain, out_scale, tq_pref=256, tk_pref=512):
    B, S, _ = k.shape
    tq = _tile(S, tq_pref)
    tk = max(tq, _tile(S, tk_pref))
    U = ATT_HEADS_PER_STEP
    return pl.pallas_call(
        functools.partial(_attn_prompt_kernel, tq=tq, tk=tk, out_scale=out_scale),
        grid=(B, H_ATT // U, S // tq),
        in_specs=[pl.BlockSpec((1, HEAD_W), lambda b, h, i: (0, 0)),
                  pl.BlockSpec((1, U * HEAD_W, tq), lambda b, h, i: (b, h, i)),
                  pl.BlockSpec((1, S, U * HEAD_W), lambda b, h, i: (b, 0, h)),
                  pl.BlockSpec((1, U * VT_ROWS, S), lambda b, h, i: (b, h, 0)),
                  pl.BlockSpec((HEAD_W, 1), lambda b, h, i: (0, 0))],
        out_specs=pl.BlockSpec((1, tq, U * HEAD_W), lambda b, h, i: (b, i, h)),
        out_shape=jax.ShapeDtypeStruct(k.shape, BF16),
        scratch_shapes=[pltpu.VMEM((U, 1, 2 * tq), F32), pltpu.VMEM((U, VT_ROWS, 2 * tq), F32)],
        compiler_params=_cparams("parallel", "parallel", "arbitrary"),
    )(lam, qt, k, vt, att_gain.reshape(HEAD_W, 1))


PAGE_KEYS = PAGE_SIZE * H_ATT


def _decode_scores(q_ref, k_refs):
    rows = 2 * H_ATT
    q = q_ref[0]
    q16 = jnp.concatenate([q, q], axis=0)
    r = lax.broadcasted_iota(jnp.int32, (rows, HEAD_W), 0)
    c = lax.broadcasted_iota(jnp.int32, (rows, HEAD_W), 1)
    qm = jnp.where(c // DK_ATT == r // H_ATT, q16, jnp.zeros_like(q16))
    own_head = (lax.broadcasted_iota(jnp.int32, (rows, PAGE_KEYS), 1) % H_ATT
                == lax.broadcasted_iota(jnp.int32, (rows, PAGE_KEYS), 0) % H_ATT)
    s = [jnp.where(own_head, _dot_nt(qm, k_ref[0].reshape(PAGE_KEYS, HEAD_W).astype(BF16)), NEG_INF)
         for k_ref in k_refs]
    return qm, s


def _decode_output(qm, s, kn_ref, vn_ref, v_refs, lam_ref, g_ref, out_scale):
    k_new = jnp.concatenate([kn_ref[0], kn_ref[0]], axis=0)
    v_new = jnp.concatenate([vn_ref[0], vn_ref[0]], axis=0)
    s_new = jnp.sum(qm.astype(F32) * k_new, axis=-1, keepdims=True)
    m = jnp.maximum(jnp.max(functools.reduce(jnp.maximum, s), axis=-1, keepdims=True), s_new)
    p = [jnp.exp2(sj - m) for sj in s]
    p_new = jnp.exp2(s_new - m)
    l = jnp.sum(functools.reduce(jnp.add, p), axis=-1, keepdims=True) + p_new
    acc = p_new * v_new
    for pj, v_ref in zip(p, v_refs):
        acc = acc + _dot(pj.astype(BF16), v_ref[0].reshape(PAGE_KEYS, HEAD_W).astype(BF16))
    o = acc / l
    od = o[:H_ATT] - lam_ref[...] * o[H_ATT:]
    return _rms(od, g_ref[...]) * out_scale


def _attn_sample_kernel(pt_ref, lam_ref, q_ref, kn_ref, vn_ref, g_ref, *refs, n_pages, out_scale):
    del pt_ref
    k_refs, v_refs, o_ref = refs[:n_pages], refs[n_pages:2 * n_pages], refs[2 * n_pages]
    qm, s = _decode_scores(q_ref, k_refs)
    o_ref[0] = _decode_output(qm, s, kn_ref, vn_ref, v_refs, lam_ref, g_ref, out_scale)


def attn_sample(q, k_new, v_new, cache_k, cache_v, page_table, lam, att_gain, out_scale):
    DB = q.shape[0]
    n_pages = page_table.shape[1]
    tok =pl.BlockSpec((1, H_ATT, HEAD_W), lambda b, pt: (b, 0, 0))
    vec = pl.BlockSpec((1, HEAD_W), lambda b, pt: (0, 0))
    page = lambda j: pl.BlockSpec((1, PAGE_SIZE, H_ATT, HEAD_W),
                                  lambda b, pt: (pt[b * n_pages + j], 0, 0, 0))
    pages = [page(j) for j in range(n_pages)]
    return pl.pallas_call(
        functools.partial(_attn_sample_kernel, n_pages=n_pages, out_scale=out_scale),
        grid_spec=pltpu.PrefetchScalarGridSpec(
            num_scalar_prefetch=1,
            grid=(DB,),
            in_specs=[vec, tok, tok, tok, vec] + pages + pages,
            out_specs=tok,
        ),
        out_shape=jax.ShapeDtypeStruct((DB, H_ATT, HEAD_W), F32),
        compiler_params=_cparams("arbitrary"),
    )(page_table.reshape(-1), lam, q, k_new, v_new, att_gain,
      *([cache_k] * n_pages), *([cache_v] * n_pages))


def _gates(zf, lb):
    f = lb + (1.0 - lb) * jax.nn.sigmoid(zf)
    return jnp.log(f), (1.0 - lb) * jax.nn.sigmoid(-zf)


def _hgrn_prompt_kernel(q_ref, f_ref, i_ref, g_ref, lb_ref, gain_ref, tril_ref, o_ref, st_ref, s_scr,
                        *, rows, chunk):
    t = pl.program_id(1)

    @pl.when(t == 0)
    def _():
        s_scr[...] = jnp.zeros_like(s_scr)

    tril = tril_ref[...]
    r = lax.broadcasted_iota(jnp.int32, (rows, rows), 0)
    c = lax.broadcasted_iota(jnp.int32, (rows, rows), 1)
    band = (c <= r) & (c // chunk == r // chunk)
    mid = chunk // 2 - 1
    n_chunks = rows // chunk

    def head(h, _):
        sl = pl.ds(pl.multiple_of(h * HEAD_W, HEAD_W), HEAD_W)
        log_f, kk = _gates(f_ref[:, sl], lb_ref[:, sl])
        hi, md, lo = _split3(log_f)
        bc = _dot(tril, hi) + _dot(tril, md) + _dot(tril, lo)
        q, g = q_ref[:, sl], g_ref[:, sl]
        v = i_ref[:, sl].astype(BF16)
        q_mid, k_mid, q_dec, k_end, decay = [], [], [], [], []
        for c in range(n_chunks):
            r0 = c * chunk
            bl = bc[r0:r0 + chunk]
            if c:
                bl = bl - bc[r0 - 1:r0]
            qc, kc = q[r0:r0 + chunk], kk[r0:r0 + chunk]
            b_mid, b_last = bl[mid:mid + 1], bl[chunk - 1:chunk]
            q_mid.append(qc * jnp.exp(bl - b_mid))
            k_mid.append(kc * jnp.exp(b_mid - bl))
            q_dec.append((qc * jnp.exp(bl)).astype(BF16))
            k_end.append((kc * jnp.exp(b_last - bl)).astype(BF16))
            decay.append(jnp.exp(b_last))
        a = _dot_nt(jnp.concatenate(q_mid, axis=0).astype(BF16), jnp.concatenate(k_mid, axis=0).astype(BF16))
        intra = _dot(jnp.where(band, a, 0.0).astype(BF16), v)
        update = [_dot_tn(v[c * chunk:(c + 1) * chunk], k_end[c]) for c in range(n_chunks)]
        st = s_scr[h]
        for c in range(n_chunks):
            r0 = c * chunk
            o = intra[r0:r0 + chunk] + _dot_nt(q_dec[c], st.astype(BF16))
            st = st * decay[c] + update[c]
            y = _rms(o, gain_ref[...]) * (g[r0:r0 + chunk] * jax.nn.sigmoid(g[r0:r0 + chunk]))
            o_ref[r0:r0 + chunk, sl] = y.astype(BF16)
        s_scr[h] = st
        return 0

    lax.fori_loop(0, H_RNN, head, 0, unroll=2)

    @pl.when(t == pl.num_programs(1) - 1)
    def _():
        for h in range(H_RNN):
            st_ref[0, h] = s_scr[h].T


def hgrn_prompt(z, lb, rnn_gain, B, S):
    rows = _tile(S, RNN_ROWS)
    chunk = _tile(rows, RNN_CHUNK)
    nt = S // rows
    tril = (jnp.arange(rows)[:, None] >= jnp.arange(rows)[None, :]).astype(BF16)
    col = lambda c: pl.BlockSpec((rows, D_HEADS), lambda b, t: (b * nt + t, c))
    return pl.pallas_call(
        functools.partial(_hgrn_prompt_kernel, rows=rows, chunk=chunk),
        grid=(B, nt),
        in_specs=[col(3), col(4), col(5), col(6),
                  pl.BlockSpec((1, D_HEADS), lambda b, t: (0, 0)),
                  pl.BlockSpec((1, HEAD_W), lambda b, t: (0, 0)),
                  pl.BlockSpec((rows, rows), lambda b, t: (0, 0))],
        out_specs=(pl.BlockSpec((rows, D_HEADS), lambda b, t: (b * nt + t, 0)),
                   pl.BlockSpec((1, H_RNN, HEAD_W, HEAD_W), lambda b, t: (b, 0, 0, 0))),
        out_shape=(jax.ShapeDtypeStruct((B * S, D_HEADS), BF16),
                   jax.ShapeDtypeStruct((B, H_RNN, HEAD_W, HEAD_W), F32)),
        scratch_shapes=[pltpu.VMEM((H_RNN, HEAD_W, HEAD_W), F32)],
        compiler_params=_cparams("parallel", "arbitrary"),
    )(z, z, z, z, lb, rnn_gain, tril)


def _hgrn_step_kernel(q_ref, f_ref, i_ref, g_ref, lb_ref, gain_ref, s0_ref, o_ref, s1_ref, *, bs):
    def column(rowvec):
        return jnp.broadcast_to(rowvec, (HEAD_W, HEAD_W)).T

    def head(h, _):
        sl = pl.ds(pl.multiple_of(h * HEAD_W, HEAD_W), HEAD_W)
        log_f, kk = _gates(f_ref[:, sl], lb_ref[:, sl])
        q, v, g = q_ref[:, sl], i_ref[:, sl], g_ref[:, sl]
        decay = jnp.exp(log_f)
        q_dec = q * decay
        a = jnp.sum(q_dec * (kk * jnp.exp(-log_f)), axis=-1, keepdims=True)
        inter = []
        for b in range(bs):
            s0 = s0_ref[b, h]
            inter.append(jnp.sum(column(q_dec[b:b + 1]) * s0, axis=0, keepdims=True))
            s1_ref[b, h] = column(decay[b:b + 1]) * s0 + column(kk[b:b + 1]) * v[b:b + 1]
        o = a * v + jnp.concatenate(inter, axis=0)
        o_ref[:, sl] = (_rms(o, gain_ref[...]) * (g * jax.nn.sigmoid(g))).astype(BF16)
        return 0

    lax.fori_loop(0, H_RNN, head, 0)


def hgrn_step(z, lb, rnn_gain, s0, bs_pref=16):
    DB = z.shape[0]
    bs = _tile(DB, bs_pref)
    col = lambda c: pl.BlockSpec((bs, D_HEADS), lambda i: (i, c))
    st = pl.BlockSpec((bs, H_RNN, HEAD_W, HEAD_W), lambda i: (i, 0, 0, 0))
    return pl.pallas_call(
        functools.partial(_hgrn_step_kernel, bs=bs),
        grid=(DB // bs,),
        in_specs=[col(3), col(4), col(5), col(6),
                  pl.BlockSpec((1, D_HEADS), lambda i: (0, 0)),
                  pl.BlockSpec((1, HEAD_W), lambda i: (0, 0)), st],
        out_specs=(pl.BlockSpec((bs, D_HEADS), lambda i: (i, 0)), st),
        out_shape=(jax.ShapeDtypeStruct((DB, D_HEADS), BF16), jax.ShapeDtypeStruct(s0.shape, F32)),
        compiler_params=_cparams("parallel"),
    )(z, z, z, z, lb, rnn_gain, s0)


def _out_proj_kernel(a_ref, r_ref, wa_ref, wr_ref, x_ref, g_ref, x1_ref, h1_ref):
    x1 = x_ref[...] + _dot(a_ref[...], wa_ref[...]) + _dot(r_ref[...], wr_ref[...])
    x1_ref[...] = x1
    h1_ref[...] = _rms(x1, g_ref[...]).astype(BF16)


def out_proj(o_att, o_rnn, w_out, x, ffn_gain, tm_pref=512):
    M, D = x.shape
    tm = _tile(M, tm_pref)
    act = pl.BlockSpec((tm, D_HEADS), lambda i: (i, 0))
    full = pl.BlockSpec((tm, D), lambda i: (i, 0))
    return pl.pallas_call(
        _out_proj_kernel,
        grid=(M // tm,),
        in_specs=[act, act,
                  pl.BlockSpec((D_HEADS, D), lambda i: (0, 0)),
                  pl.BlockSpec((D_HEADS, D), lambda i: (1, 0)),
                  full, pl.BlockSpec((1, D), lambda i: (0, 0))],
        out_specs=(full, full),
        out_shape=(jax.ShapeDtypeStruct((M, D), F32), jax.ShapeDtypeStruct((M, D), BF16)),
        compiler_params=_cparams("parallel"),
    )(o_att, o_rnn, w_out, w_out, x, ffn_gain)


def _conv_gate(g, u, first_tile, cw_ref, cb_ref, act_ref, tail_ref, carry_ref):
    @pl.when(first_tile)
    def _():
        carry_ref[...] = jnp.zeros_like(carry_ref)

    tm = g.shape[0]
    r = lax.broadcasted_iota(jnp.int32, g.shape, 0)
    prev2, prev1 = carry_ref[0:1], carry_ref[1:2]
    g1 = jnp.where(r == 0, prev1, pltpu.roll(g, 1, 0))
    g2 = jnp.where(r == 0, prev2, jnp.where(r == 1, prev1, pltpu.roll(g, 2, 0)))
    cw = cw_ref[...]
    c = cb_ref[...] + g2 * cw[0:1] + g1 * cw[1:2] + g * cw[2:3]
    act_ref[...] = (c * jax.nn.sigmoid(c) * u).astype(BF16)
    tail = g[tm - (CONV_W - 1):]
    carry_ref[0:CONV_W - 1] = tail
    tail_ref[0] = tail


def _ffn_seq_kernel(h_ref, wg_ref, wu_ref, cw_ref, cb_ref, act_ref, tail_ref, carry_ref, *, tiles_per_seq):
    h = h_ref[...]
    g = _dot(h, wg_ref[...])
    u = _dot(h, wu_ref[...])
    _conv_gate(g, u, pl.program_id(1) % tiles_per_seq == 0, cw_ref, cb_ref, act_ref, tail_ref, carry_ref)


def _ffn_decode_kernel(pt_ref, h_ref, wg_ref, wu_ref, cw_ref, cb_ref, lam_ref, q_ref, kn_ref, vn_ref, g_ref,
                       *refs, n_pages, tiles_m, tiles_per_seq, out_scale):
    del pt_ref
    k_refs, v_refs = refs[:n_pages], refs[n_pages:2 * n_pages]
    act_ref, tail_ref, o_ref, carry_ref = refs[2 * n_pages:]
    qm, s = _decode_scores(q_ref, k_refs)
    h = h_ref[...]
    g = _dot(h, wg_ref[...])
    u = _dot(h, wu_ref[...])
    o_ref[0] = _decode_output(qm, s, kn_ref, vn_ref, v_refs, lam_ref, g_ref, out_scale)
    first_tile = (pl.program_id(0) % tiles_m) % tiles_per_seq == 0
    _conv_gate(g, u, first_tile, cw_ref, cb_ref, act_ref, tail_ref, carry_ref)


def ffn_seq_decode_tiles(M, S, F, DB, tm_pref=256, tn_pref=1408):
    tm, tn = _tile(S, tm_pref), _tile(F, tn_pref)
    return (tm, tn) if tn % HEAD_W == 0 and (M // tm) * (F // tn) == DB else None


def ffn_seq_decode(h, w_gate, w_up, conv_w, conv_b, B, S, tiles,
                   q, k_new, v_new, cache_k, cache_v, page_table, lam, att_gain, out_scale):
    M, D = h.shape
    F = w_gate.shape[1]
    DB, n_pages = page_table.shape
    tm, tn = tiles
    tiles_m, tiles_per_seq = M // tm, S // tm
    row = lambda i: i % tiles_m
    col = lambda i: i // tiles_m
    wspec = pl.BlockSpec((D, tn), lambda i, pt: (0, col(i)), pipeline_mode=pl.Buffered(1))
    tok = pl.BlockSpec((1, H_ATT, HEAD_W), lambda i, pt: (i, 0, 0))
    vec = pl.BlockSpec((1, HEAD_W), lambda i, pt: (0, 0))
    page = lambda j: pl.BlockSpec((1, PAGE_SIZE, H_ATT, HEAD_W),
                                  lambda i, pt: (pt[i * n_pages + j], 0, 0, 0))
    pages = [page(j) for j in range(n_pages)]
    return pl.pallas_call(
        functools.partial(_ffn_decode_kernel, n_pages=n_pages, tiles_m=tiles_m,
                          tiles_per_seq=tiles_per_seq, out_scale=out_scale),
        grid_spec=pltpu.PrefetchScalarGridSpec(
            num_scalar_prefetch=1,
            grid=(DB,),
            in_specs=[pl.BlockSpec((tm, D), lambda i, pt: (row(i), 0)), wspec, wspec,
                      pl.BlockSpec((CONV_W, tn), lambda i, pt: (0, col(i))),
                      pl.BlockSpec((1, tn), lambda i, pt: (0, col(i))),
                      vec, tok, tok, tok, vec] + pages + pages,
            out_specs=(pl.BlockSpec((tm, tn), lambda i, pt: (row(i), col(i))),
                       pl.BlockSpec((1, CONV_W - 1, tn), lambda i, pt: (row(i) // tiles_per_seq, 0, col(i))),
                       tok),
            scratch_shapes=[pltpu.VMEM((8, tn), F32)],
        ),
        out_shape=(jax.ShapeDtypeStruct((M, F), BF16), jax.ShapeDtypeStruct((B, CONV_W - 1, F), F32),
                   jax.ShapeDtypeStruct((DB, H_ATT, HEAD_W), F32)),
        compiler_params=_cparams("arbitrary"),
    )(page_table.reshape(-1), h, w_gate, w_up, conv_w, conv_b, lam, q, k_new, v_new, att_gain,
      *([cache_k] * n_pages), *([cache_v] * n_pages))


def ffn_seq(h, w_gate, w_up, conv_w, conv_b, B, S, tm_pref=512, tn_pref=1408):
    M, D = h.shape
    F = w_gate.shape[1]
    tm, tn = _tile(S, tm_pref), _tile(F, tn_pref)
    tiles_per_seq = S // tm
    wspec = pl.BlockSpec((D, tn), lambda n, m: (0, n))
    return pl.pallas_call(
        functools.partial(_ffn_seq_kernel, tiles_per_seq=tiles_per_seq),
        grid=(F // tn, M // tm),
        in_specs=[pl.BlockSpec((tm, D), lambda n, m: (m, 0)), wspec, wspec,
                  pl.BlockSpec((CONV_W, tn), lambda n, m: (0, n)),
                  pl.BlockSpec((1, tn), lambda n, m: (0, n))],
        out_specs=(pl.BlockSpec((tm, tn), lambda n, m: (m, n)),
                   pl.BlockSpec((1, CONV_W - 1, tn), lambda n, m: (m // tiles_per_seq, 0, n))),
        out_shape=(jax.ShapeDtypeStruct((M, F), BF16), jax.ShapeDtypeStruct((B, CONV_W - 1, F), F32)),
        scratch_shapes=[pltpu.VMEM((8, tn), F32)],
        compiler_params=_cparams("parallel", "arbitrary"),
    )(h, w_gate, w_up, conv_w, conv_b)


def _ffn_step_kernel(h_ref, wg_ref, wu_ref, cw_ref, cb_ref, p2_ref, p1_ref, act_ref, g_ref):
    h = h_ref[...]
    g = _dot(h, wg_ref[...])
    u = _dot(h, wu_ref[...])
    cw = cw_ref[...]
    c = cb_ref[...] + p2_ref[...] * cw[0:1] + p1_ref[...] * cw[1:2] + g * cw[2:3]
    act_ref[...] = (c * jax.nn.sigmoid(c) * u).astype(BF16)
    g_ref[...] = g


def ffn_step(h, w_gate, w_up, conv_w, conv_b, prev2, prev1, tn_pref=1408):
    M, D = h.shape
    F = w_gate.shape[1]
    tn = _tile(F, tn_pref)
    wspec = pl.BlockSpec((D, tn), lambda n: (0, n))
    tile = pl.BlockSpec((M, tn), lambda n: (0, n))
    return pl.pallas_call(
        _ffn_step_kernel,
        grid=(F // tn,),
        in_specs=[pl.BlockSpec((M, D), lambda n: (0, 0)), wspec, wspec,
                  pl.BlockSpec((CONV_W, tn), lambda n: (0, n)),
                  pl.BlockSpec((1, tn), lambda n: (0, n)), tile, tile],
        out_specs=(tile, tile),
        out_shape=(jax.ShapeDtypeStruct((M, F), BF16), jax.ShapeDtypeStruct((M, F), F32)),
        compiler_params=_cparams("parallel"),
    )(h, w_gate, w_up, conv_w, conv_b, prev2, prev1)


def _down_kernel(a_ref, w_ref, x_ref, o_ref):
    o_ref[...] = x_ref[...] + _dot(a_ref[...], w_ref[...])


def ffn_down(act, w_down, x, tm_pref=1024, tn_pref=512):
    M, F = act.shape
    D = w_down.shape[1]
    tm, tn = _tile(M, tm_pref), _tile(D, tn_pref)
    tile = pl.BlockSpec((tm, tn), lambda i, j: (i, j))
    return pl.pallas_call(
        _down_kernel,
        grid=(M // tm, D // tn),
        in_specs=[pl.BlockSpec((tm, F), lambda i, j: (i, 0)),
                  pl.BlockSpec((F, tn), lambda i, j: (0, j)), tile],
        out_specs=tile,
        out_shape=jax.ShapeDtypeStruct((M, D), F32),
        compiler_params=_cparams("parallel", "arbitrary"),
    )(act, w_down, x)


def _ple_kernel(x_ref, g_ref, wg_ref, p_ref, wp_ref, o_ref, h_ref, *, tn):
    j = pl.program_id(1)

    @pl.when(j == 0)
    def _():
        h_ref[...] = _rms(x_ref[...], g_ref[...]).astype(BF16)

    gate = jax.nn.sigmoid(_dot(h_ref[...], wg_ref[...]))
    emb = _dot(p_ref[...].astype(BF16), wp_ref[...])
    o_ref[...] = x_ref[:, pl.ds(pl.multiple_of(j * tn, tn), tn)] + gate * emb


def ple(x, gain, w_gate, p, w_proj, tm_pref=1024, tn_pref=1024):
    M, D = x.shape
    DP = p.shape[1]
    tm, tn = _tile(M, tm_pref), _tile(D, tn_pref)
    return pl.pallas_call(
        functools.partial(_ple_kernel, tn=tn),
        grid=(M // tm, D // tn),
        in_specs=[pl.BlockSpec((tm, D), lambda i, j: (i, 0)),
                  pl.BlockSpec((1, D), lambda i, j: (0, 0)),
                  pl.BlockSpec((D, tn), lambda i, j: (0, j)),
                  pl.BlockSpec((tm, DP), lambda i, j: (i, 0)),
                  pl.BlockSpec((DP, tn), lambda i, j: (0, j))],
        out_specs=pl.BlockSpec((tm, tn), lambda i, j: (i, j)),
        out_shape=jax.ShapeDtypeStruct((M, D), F32),
        scratch_shapes=[pltpu.VMEM((tm, D), BF16)],
        compiler_params=_cparams("parallel", "arbitrary"),
    )(x, gain, w_gate, p, w_proj)


def _rope_tables(pos):
    half = DK_ATT // 2
    inv_freq = 1.0 / (ROPE_THETA ** (jnp.arange(half, dtype=F32) * (2.0 / DK_ATT)))
    ang = pos.astype(F32)[:, None] * inv_freq[None, :]
    cos, sin = jnp.cos(ang), jnp.sin(ang)
    return (jnp.concatenate([cos] * 4, axis=-1), jnp.concatenate([-sin, sin] * 2, axis=-1))


def _row(v):
    return v.reshape(1, -1).astype(F32)


def kernel(x_prompt, x_sample, p_prompt, p_sample, cache_k, cache_v, state_rnn, state_ffn_conv, page_table,
           norm_mix, w_in, q_norm, k_norm, lam_q1, lam_k1, lam_q2, lam_k2, att_out_norm, rnn_out_norm,
           lower_bounds, w_out, norm_ffn, w_gate, w_up, conv_w, conv_b, w_down, norm_ple, w_ple_gate,
           w_ple_proj):
    B, S, D = x_prompt.shape
    DB, T, _ = x_sample.shape
    assert T == 1, "the sample group is implemented for one new token per sequence"
    depth = w_in.shape[0]
    past = page_table.shape[1] * PAGE_SIZE
    cos_p, sin_p = _rope_tables(jnp.arange(S, dtype=jnp.int32))
    cos_p, sin_p = jnp.tile(cos_p, (B, 1)), jnp.tile(sin_p, (B, 1))
    cos_s, sin_s = _rope_tables(jnp.full((DB,), past, dtype=jnp.int32))

    yp = x_prompt.reshape(B * S, D)
    ys = x_sample.reshape(DB, D)
    outs = [[] for _ in range(8)]
    for i in range(depth):
        lam_init = 0.8 - 0.6 * math.exp(-0.3 * i)
        out_scale = 1.0 - lam_init
        lam, lb = layer_params(jnp.stack([lam_q1[i], lam_k1[i], lam_q2[i], lam_k2[i]]).astype(F32),
                               lower_bounds.astype(F32), i, lam_init)
        w_in_b, w_out_b = w_in[i].astype(BF16), w_out[i].astype(BF16)
        w_gate_b, w_up_b, w_down_b = w_gate[i].astype(BF16), w_up[i].astype(BF16), w_down[i].astype(BF16)
        w_pg_b, w_pp_b = w_ple_gate[i].astype(BF16), w_ple_proj[i].astype(BF16)
        q_gain = jnp.tile(_row(q_norm[i]), (1, 2))
        k_gain = jnp.tile(_row(k_norm[i]), (1, 2))
        att_gain, rnn_gain = _row(att_out_norm[i]), _row(rnn_out_norm[i])

        def mix_in(x, cos, sin, seqs=None):
            z = norm_matmul(x, _row(norm_mix[i]), w_in_b)
            return (z,) + tuple(qk_prep(z, cos, sin, q_gain, k_gain, seqs))

        def mix_out(x, o_att, o_rnn):
            return out_proj(o_att, o_rnn, w_out_b, x, _row(norm_ffn[i]))

        def embed(x, act, p):
            x = ffn_down(act, w_down_b, x)
            return ple(x, _row(norm_ple[i]), w_pg_b, p, w_pp_b)

        z, qt, k, kb, v, vt = mix_in(yp, cos_p, sin_p, (B, S))
        o_att = attn_prompt(qt, kb.reshape(B, S, -1), vt, lam, att_gain, out_scale).reshape(B * S, -1)
        o_rnn, rnn_p = hgrn_prompt(z, lb, rnn_gain, B, S)
        x1, h1 = mix_out(yp, o_att, o_rnn)
        outs[0].append(k.reshape(B, S // PAGE_SIZE, PAGE_SIZE, H_ATT, HEAD_W))
        outs[1].append(v.reshape(B, S // PAGE_SIZE, PAGE_SIZE, H_ATT, HEAD_W))
        outs[2].append(rnn_p.astype(state_rnn.dtype))

        z, q, k, kb, v, vb = mix_in(ys, cos_s, sin_s)
        per_head = lambda a: a.reshape(DB, H_ATT, HEAD_W)
        decode_args = (per_head(q), per_head(k), per_head(v), cache_k[i], cache_v[i], page_table,
                       lam, att_gain, out_scale)
        ffn_args = (h1, w_gate_b, w_up_b, conv_w[i], _row(conv_b[i]), B, S)
        tiles = ffn_seq_decode_tiles(B * S, S, w_gate_b.shape[1], DB)
        if tiles is not None:
            act, conv_p, o_att = ffn_seq_decode(*ffn_args, tiles, *decode_args)
        else:
            act, conv_p = ffn_seq(*ffn_args)
            o_att = attn_sample(*decode_args)
        o_att = o_att.reshape(DB, D_HEADS).astype(BF16)
        yp = embed(x1, act, p_prompt[i].reshape(B * S, -1))
        outs[3].append(conv_p)
        o_rnn, rnn_s = hgrn_step(z, lb, rnn_gain, state_rnn[i].astype(F32))
        x1, h1 = mix_out(ys, o_att, o_rnn)
        buf = state_ffn_conv[i]
        act, g_new = ffn_step(h1, w_gate_b, w_up_b, conv_w[i], _row(conv_b[i]), buf[:, 0], buf[:, 1])
        ys = embed(x1, act, p_sample[i].reshape(DB, -1))
        outs[4].append(k.reshape(DB, T, H_ATT, HEAD_W))
        outs[5].append(v.reshape(DB, T, H_ATT, HEAD_W))
        outs[6].append(rnn_s.astype(state_rnn.dtype))
        outs[7].append(jnp.stack([buf[:, 1], g_new], axis=1))

    return (yp.reshape(B, S, D), ys.reshape(DB, T, D)) + tuple(jnp.stack(o) for o in outs)
```

```python
import functools
import math

import jax
import jax.numpy as jnp
from jax import lax
from jax.experimental import pallas as pl
from jax.experimental.pallas import tpu as pltpu

H_ATT = 8
DK_ATT = 64
H_RNN = 8
HEAD_W = 128
D_HEADS = H_ATT * HEAD_W
PAGE_SIZE = 128
CONV_W = 3
ROPE_THETA = 10000.0
EPS = 1e-6
NEG_INF = -1e30
RNN_CHUNK = 32
RNN_ROWS = 256
ATT_HEADS_PER_STEP = 4
VT_ROWS = HEAD_W + 16
Q_SCALE = DK_ATT ** -0.5 * math.log2(math.e)
VMEM_LIMIT_V7X = 56 * 1024 * 1024

F32 = jnp.float32
BF16 = jnp.bfloat16


def _cparams(*sem):
    return pltpu.CompilerParams(dimension_semantics=sem, vmem_limit_bytes=VMEM_LIMIT_V7X)


def _tile(n, pref):
    t = min(n, pref)
    while n % t:
        t //= 2
    return t


def _split3(x):
    hi = x.astype(BF16)
    r1 = x - hi.astype(F32)
    mid = r1.astype(BF16)
    lo = (r1 - mid.astype(F32)).astype(BF16)
    return hi, mid, lo


def _dot(a, b):
    return jnp.dot(a, b, preferred_element_type=F32)


def _dot_nt(a, b):
    return lax.dot_general(a, b, (((1,), (1,)), ((), ())), preferred_element_type=F32)


def _dot_tn(a, b):
    return lax.dot_general(a, b, (((0,), (0,)), ((), ())), preferred_element_type=F32)


def _rms(x, gain):
    return x * lax.rsqrt(jnp.mean(x * x, axis=-1, keepdims=True) + EPS) * gain


def _params_kernel(lam_ref, lb_in_ref, lam_out_ref, lb_out_ref, *, layer, lam_init):
    v = lam_ref[...]
    s1 = jnp.sum(v[0:1] * v[1:2], axis=-1, keepdims=True)
    s2 = jnp.sum(v[2:3] * v[3:4], axis=-1, keepdims=True)
    lam = jnp.exp(s1) - jnp.exp(s2) + lam_init
    lam_out_ref[...] = jnp.broadcast_to(lam, lam_out_ref.shape)
    lb = lb_in_ref[...]
    e = jnp.exp(lb - jnp.max(lb, axis=0, keepdims=True))
    sm = e / jnp.sum(e, axis=0, keepdims=True)
    lb_out_ref[...] = jnp.sum(sm[0:layer + 1], axis=0, keepdims=True)


def layer_params(lam_vecs, lower_bounds, layer, lam_init):
    return pl.pallas_call(
        functools.partial(_params_kernel, layer=layer, lam_init=lam_init),
        out_shape=(jax.ShapeDtypeStruct((1, HEAD_W), F32),
                   jax.ShapeDtypeStruct((1, lower_bounds.shape[1]), F32)),
    )(lam_vecs, lower_bounds)


def _norm_matmul_kernel(x_ref, g_ref, w_ref, o_ref, h_ref):
    @pl.when(pl.program_id(1) == 0)
    def _():
        h_ref[...] = _rms(x_ref[...], g_ref[...]).astype(BF16)

    o_ref[...] = _dot(h_ref[...], w_ref[...].astype(BF16))


def norm_matmul(x, gain, w, tm_pref=1024, tn_pref=1024):
    M, K = x.shape
    N = w.shape[1]
    tm, tn = _tile(M, tm_pref), _tile(N, tn_pref)
    return pl.pallas_call(
        _norm_matmul_kernel,
        grid=(M // tm, N // tn),
        in_specs=[pl.BlockSpec((tm, K), lambda i, j: (i, 0)),
                  pl.BlockSpec((1, K), lambda i, j: (0, 0)),
                  pl.BlockSpec((K, tn), lambda i, j: (0, j))],
        out_specs=pl.BlockSpec((tm, tn), lambda i, j: (i, j)),
        out_shape=jax.ShapeDtypeStruct((M, N), F32),
        scratch_shapes=[pltpu.VMEM((tm, K), BF16)],
        compiler_params=_cparams("parallel", "arbitrary"),
    )(x, gain, w)


def _qk_prep_kernel(zq_ref, zk_ref, zv_ref, cos_ref, sin_ref, qg_ref, kg_ref, grp_ref,
                    q_ref, k_ref, kb_ref, v_ref, vb_ref, *, transposed):
    cos = cos_ref[...]
    sin = sin_ref[...]
    lane = lax.broadcasted_iota(jnp.int32, cos.shape, 1)
    first_half = (lane % DK_ATT) < (DK_ATT // 2)
    grp = grp_ref[...]

    def norm_rope(x, gain):
        hi, mid, lo = _split3(x * x)
        ss = (_dot(hi, grp) + _dot(mid, grp) + _dot(lo, grp)) * (1.0 / DK_ATT)
        y = x * lax.rsqrt(ss + EPS) * gain
        swapped = jnp.where(first_half, pltpu.roll(y, HEAD_W - DK_ATT // 2, 1),
                            pltpu.roll(y, DK_ATT // 2, 1))
        return y * cos + swapped * sin

    for h in range(H_ATT):
        sl = slice(h * HEAD_W, (h + 1) * HEAD_W)
        q = norm_rope(zq_ref[:, sl], qg_ref[...]) * Q_SCALE
        k = norm_rope(zk_ref[:, sl], kg_ref[...])
        k_ref[:, sl] = k
        kb_ref[:, sl] = k.astype(BF16)
        v = zv_ref[:, sl]
        v_ref[:, sl] = v
        if transposed:
            q_ref[0, sl, :] = q.T.astype(BF16)
            r0 = h * VT_ROWS
            vb_ref[0, r0:r0 + HEAD_W, :] = v.T.astype(BF16)
            pad = (VT_ROWS - HEAD_W, v.shape[0])
            vb_ref[0, r0 + HEAD_W:r0 + VT_ROWS, :] = (lax.broadcasted_iota(jnp.int32, pad, 0) == 0).astype(BF16)
        else:
            q_ref[:, sl] = q.astype(BF16)
            vb_ref[:, sl] = v.astype(BF16)


def qk_prep(z, cos, sin, q_gain, k_gain, seqs=None, tm_pref=512):
    M = z.shape[0]
    tm = _tile(M if seqs is None else seqs[1], tm_pref)
    lane = jnp.arange(HEAD_W)
    grp = (lane[:, None] // DK_ATT == lane[None, :] // DK_ATT).astype(BF16)
    col = lambda c: pl.BlockSpec((tm, D_HEADS), lambda i: (i, c))
    row = pl.BlockSpec((tm, HEAD_W), lambda i: (i, 0))
    vec = pl.BlockSpec((1, HEAD_W), lambda i: (0, 0))
    out = pl.BlockSpec((tm, D_HEADS), lambda i: (i, 0))
    sd = lambda dt: jax.ShapeDtypeStruct((M, D_HEADS), dt)
    if seqs is None:
        out_q, out_v, sd_q, sd_v = out, out, sd(BF16), sd(BF16)
    else:
        B, S = seqs
        tiles = S // tm
        out_q = pl.BlockSpec((1, D_HEADS, tm), lambda i: (i // tiles, 0, i % tiles))
        out_v = pl.BlockSpec((1, H_ATT * VT_ROWS, tm), lambda i: (i // tiles, 0, i % tiles))
        sd_q = jax.ShapeDtypeStruct((B, D_HEADS, S), BF16)
        sd_v = jax.ShapeDtypeStruct((B, H_ATT * VT_ROWS, S), BF16)
    return pl.pallas_call(
        functools.partial(_qk_prep_kernel, transposed=seqs is not None),
        grid=(M // tm,),
        in_specs=[col(0), col(1), col(2), row, row, vec, vec,
                  pl.BlockSpec((HEAD_W, HEAD_W), lambda i: (0, 0))],
        out_specs=(out_q, out, out, out, out_v),
        out_shape=(sd_q, sd(F32), sd(BF16), sd(F32), sd_v),
        compiler_params=_cparams("parallel"),
    )(z, z, z, cos, sin, q_gain, k_gain, grp)


def _attn_prompt_kernel(lam_ref, qt_ref, k_ref, vt_ref, g_ref, o_ref, m_scr, acc_scr, *, tq, tk, out_scale):
    qi = pl.program_id(2)
    n_full = (qi * tq) // tk
    dim = lax.broadcasted_iota(jnp.int32, (HEAD_W, tq), 0)
    q2t = []
    for u in range(ATT_HEADS_PER_STEP):
        qt = qt_ref[0, u * HEAD_W:(u + 1) * HEAD_W, :]
        zero = jnp.zeros_like(qt)
        q2t.append(jnp.concatenate([jnp.where(dim < DK_ATT, qt, zero),
                                    jnp.where(dim >= DK_ATT, qt, zero)], axis=1))
    m_scr[...] = jnp.full(m_scr.shape, NEG_INF, F32)
    acc_scr[...] = jnp.zeros(acc_scr.shape, F32)

    def block(kb, masked):
        start = pl.multiple_of(kb * tk, tk)

        def scores(u):
            return _dot(k_ref[0, pl.ds(start, tk), u * HEAD_W:(u + 1) * HEAD_W], q2t[u])

        s_next = scores(0)
        for u in range(ATT_HEADS_PER_STEP):
            s = s_next
            if u + 1 < ATT_HEADS_PER_STEP:
                s_next = scores(u + 1)
            if masked:
                key = start + lax.broadcasted_iota(jnp.int32, s.shape, 0)
                qry = qi * tq + lax.broadcasted_iota(jnp.int32, s.shape, 1) % tq
                s = jnp.where(key <= qry, s, NEG_INF)
            m_prev = m_scr[u]
            m_new = jnp.maximum(m_prev, jnp.max(s, axis=0, keepdims=True))
            m_scr[u] = m_new
            p = jnp.exp2(s - m_new).astype(BF16)
            vt = vt_ref[0, u * VT_ROWS:(u + 1) * VT_ROWS, pl.ds(start, tk)]
            acc_scr[u] = jnp.exp2(m_prev - m_new) * acc_scr[u] + _dot(vt, p)

    def body(kb, carry):
        block(kb, False)
        return carry

    lax.fori_loop(0, n_full, body, 0)
    block(n_full, True)
    for u in range(ATT_HEADS_PER_STEP):
        acc = acc_scr[u]
        o = acc[:HEAD_W] / acc[HEAD_W:HEAD_W + 1]
        od = o[:, :tq] - lam_ref[:, :1] * o[:, tq:]
        y = od * lax.rsqrt(jnp.mean(od * od, axis=0, keepdims=True) + EPS) * g_ref[...] * out_scale
        o_ref[0, :, u * HEAD_W:(u + 1) * HEAD_W] = y.T.astype(BF16)


def attn_prompt(qt, k, vt, lam, att_gain, out_scale, tq_pref=256, tk_pref=512):
    B, S, _ = k.shape
    tq = _tile(S, tq_pref)
    tk = max(tq, _tile(S, tk_pref))
    U = ATT_HEADS_PER_STEP
    return pl.pallas_call(
        functools.partial(_attn_prompt_kernel, tq=tq, tk=tk, out_scale=out_scale),
        grid=(B, H_ATT // U, S // tq),
        in_specs=[pl.BlockSpec((1, HEAD_W), lambda b, h, i: (0, 0)),
                  pl.BlockSpec((1, U * HEAD_W, tq), lambda b, h, i: (b, h, i)),
                  pl.BlockSpec((1, S, U * HEAD_W), lambda b, h, i: (b, 0, h)),
                  pl.BlockSpec((1, U * VT_ROWS, S), lambda b, h, i: (b, h, 0)),
                  pl.BlockSpec((HEAD_W, 1), lambda b, h, i: (0, 0))],
        out_specs=pl.BlockSpec((1, tq, U * HEAD_W), lambda b, h, i: (b, i, h)),
        out_shape=jax.ShapeDtypeStruct(k.shape, BF16),
        scratch_shapes=[pltpu.VMEM((U, 1, 2 * tq), F32), pltpu.VMEM((U, VT_ROWS, 2 * tq), F32)],
        compiler_params=_cparams("parallel", "parallel", "arbitrary"),
    )(lam, qt, k, vt, att_gain.reshape(HEAD_W, 1))


PAGE_KEYS = PAGE_SIZE * H_ATT


def _decode_query(q_ref):
    rows = 2 * H_ATT
    q = q_ref[0]
    q16 = jnp.concatenate([q, q], axis=0)
    r = lax.broadcasted_iota(jnp.int32, (rows, HEAD_W), 0)
    c = lax.broadcasted_iota(jnp.int32, (rows, HEAD_W), 1)
    qm = jnp.where(c // DK_ATT == r // H_ATT, q16, jnp.zeros_like(q16))
    own_head = (lax.broadcasted_iota(jnp.int32, (rows, PAGE_KEYS), 1) % H_ATT
                == lax.broadcasted_iota(jnp.int32, (rows, PAGE_KEYS), 0) % H_ATT)
    return qm, own_head


def _page_scores(qm, own_head, k_ref):
    return jnp.where(own_head, _dot_nt(qm, k_ref[0].reshape(PAGE_KEYS, HEAD_W).astype(BF16)), NEG_INF)


def _decode_weights(qm, s, kn_ref):
    k_new = jnp.concatenate([kn_ref[0], kn_ref[0]], axis=0)
    s_new = jnp.sum(qm.astype(F32) * k_new, axis=-1, keepdims=True)
    m = jnp.maximum(jnp.max(functools.reduce(jnp.maximum, s), axis=-1, keepdims=True), s_new)
    p = [jnp.exp2(sj - m) for sj in s]
    p_new = jnp.exp2(s_new - m)
    l = jnp.sum(functools.reduce(jnp.add, p), axis=-1, keepdims=True) + p_new
    return p, p_new, l


def _page_values(p, v_ref):
    return _dot(p.astype(BF16), v_ref[0].reshape(PAGE_KEYS, HEAD_W).astype(BF16))


def _decode_finish(values, p_new, l, vn_ref, lam_ref, g_ref, out_scale):
    v_new = jnp.concatenate([vn_ref[0], vn_ref[0]], axis=0)
    o = (functools.reduce(jnp.add, values) + p_new * v_new) / l
    od = o[:H_ATT] - lam_ref[...] * o[H_ATT:]
    return _rms(od, g_ref[...]) * out_scale


def _attn_sample_kernel(pt_ref, lam_ref, q_ref, kn_ref, vn_ref, g_ref, *refs, n_pages, out_scale):
    del pt_ref
    k_refs, v_refs, o_ref = refs[:n_pages], refs[n_pages:2 * n_pages], refs[2 * n_pages]
    qm, own_head = _decode_query(q_ref)
    p, p_new, l = _decode_weights(qm, [_page_scores(qm, own_head, k_ref) for k_ref in k_refs], kn_ref)
    values = [_page_values(pj, v_ref) for pj, v_ref in zip(p, v_refs)]
    o_ref[0] = _decode_finish(values, p_new, l, vn_ref, lam_ref, g_ref, out_scale)


def attn_sample(q, k_new, v_new, cache_k, cache_v, page_table, lam, att_gain, out_scale):
    DB = q.shape[0]
    n_pages = page_table.shape[1]
    tok =pl.BlockSpec((1, H_ATT, HEAD_W), lambda b, pt: (b, 0, 0))
    vec = pl.BlockSpec((1, HEAD_W), lambda b, pt: (0, 0))
    page = lambda j: pl.BlockSpec((1, PAGE_SIZE, H_ATT, HEAD_W),
                                  lambda b, pt: (pt[b * n_pages + j], 0, 0, 0))
    pages = [page(j) for j in range(n_pages)]
    return pl.pallas_call(
        functools.partial(_attn_sample_kernel, n_pages=n_pages, out_scale=out_scale),
        grid_spec=pltpu.PrefetchScalarGridSpec(
            num_scalar_prefetch=1,
            grid=(DB,),
            in_specs=[vec, tok, tok, tok, vec] + pages + pages,
            out_specs=tok,
        ),
        out_shape=jax.ShapeDtypeStruct((DB, H_ATT, HEAD_W), F32),
        compiler_params=_cparams("arbitrary"),
    )(page_table.reshape(-1), lam, q, k_new, v_new, att_gain,
      *([cache_k] * n_pages), *([cache_v] * n_pages))


def _gates(zf, lb):
    f = lb + (1.0 - lb) * jax.nn.sigmoid(zf)
    return jnp.log(f), (1.0 - lb) * jax.nn.sigmoid(-zf)


def _hgrn_prompt_kernel(q_ref, f_ref, i_ref, g_ref, lb_ref, gain_ref, tril_ref, o_ref, st_ref, s_scr,
                        *, rows, chunk):
    t = pl.program_id(1)

    @pl.when(t == 0)
    def _():
        s_scr[...] = jnp.zeros_like(s_scr)

    tril = tril_ref[...]
    r = lax.broadcasted_iota(jnp.int32, (rows, rows), 0)
    c = lax.broadcasted_iota(jnp.int32, (rows, rows), 1)
    band = (c <= r) & (c // chunk == r // chunk)
    mid = chunk // 2 - 1
    n_chunks = rows // chunk

    def head(h, _):
        sl = pl.ds(pl.multiple_of(h * HEAD_W, HEAD_W), HEAD_W)
        log_f, kk = _gates(f_ref[:, sl], lb_ref[:, sl])
        hi, md, lo = _split3(log_f)
        bc = _dot(tril, hi) + _dot(tril, md) + _dot(tril, lo)
        q, g = q_ref[:, sl], g_ref[:, sl]
        v = i_ref[:, sl].astype(BF16)
        q_mid, k_mid, q_dec, k_end, decay = [], [], [], [], []
        for c in range(n_chunks):
            r0 = c * chunk
            bl = bc[r0:r0 + chunk]
            if c:
                bl = bl - bc[r0 - 1:r0]
            qc, kc = q[r0:r0 + chunk], kk[r0:r0 + chunk]
            b_mid, b_last = bl[mid:mid + 1], bl[chunk - 1:chunk]
            q_mid.append(qc * jnp.exp(bl - b_mid))
            k_mid.append(kc * jnp.exp(b_mid - bl))
            q_dec.append((qc * jnp.exp(bl)).astype(BF16))
            k_end.append((kc * jnp.exp(b_last - bl)).astype(BF16))
            decay.append(jnp.exp(b_last))
        a = _dot_nt(jnp.concatenate(q_mid, axis=0).astype(BF16), jnp.concatenate(k_mid, axis=0).astype(BF16))
        intra = _dot(jnp.where(band, a, 0.0).astype(BF16), v)
        update = [_dot_tn(v[c * chunk:(c + 1) * chunk], k_end[c]) for c in range(n_chunks)]
        st = s_scr[h]
        for c in range(n_chunks):
            r0 = c * chunk
            o = intra[r0:r0 + chunk] + _dot_nt(q_dec[c], st.astype(BF16))
            st = st * decay[c] + update[c]
            y = _rms(o, gain_ref[...]) * (g[r0:r0 + chunk] * jax.nn.sigmoid(g[r0:r0 + chunk]))
            o_ref[r0:r0 + chunk, sl] = y.astype(BF16)
        s_scr[h] = st
        return 0

    lax.fori_loop(0, H_RNN, head, 0, unroll=4)

    @pl.when(t == pl.num_programs(1) - 1)
    def _():
        for h in range(H_RNN):
            st_ref[0, h] = s_scr[h].T


def hgrn_prompt(z, lb, rnn_gain, B, S):
    rows = _tile(S, RNN_ROWS)
    chunk = _tile(rows, RNN_CHUNK)
    nt = S // rows
    tril = (jnp.arange(rows)[:, None] >= jnp.arange(rows)[None, :]).astype(BF16)
    col = lambda c: pl.BlockSpec((rows, D_HEADS), lambda b, t: (b * nt + t, c))
    return pl.pallas_call(
        functools.partial(_hgrn_prompt_kernel, rows=rows, chunk=chunk),
        grid=(B, nt),
        in_specs=[col(3), col(4), col(5), col(6),
                  pl.BlockSpec((1, D_HEADS), lambda b, t: (0, 0)),
                  pl.BlockSpec((1, HEAD_W), lambda b, t: (0, 0)),
                  pl.BlockSpec((rows, rows), lambda b, t: (0, 0))],
        out_specs=(pl.BlockSpec((rows, D_HEADS), lambda b, t: (b * nt + t, 0)),
                   pl.BlockSpec((1, H_RNN, HEAD_W, HEAD_W), lambda b, t: (b, 0, 0, 0))),
        out_shape=(jax.ShapeDtypeStruct((B * S, D_HEADS), BF16),
                   jax.ShapeDtypeStruct((B, H_RNN, HEAD_W, HEAD_W), F32)),
        scratch_shapes=[pltpu.VMEM((H_RNN, HEAD_W, HEAD_W), F32)],
        compiler_params=_cparams("parallel", "arbitrary"),
    )(z, z, z, z, lb, rnn_gain, tril)


def _hgrn_step_kernel(q_ref, f_ref, i_ref, g_ref, lb_ref, gain_ref, s0_ref, o_ref, s1_ref, *, bs):
    def column(rowvec):
        return jnp.broadcast_to(rowvec, (HEAD_W, HEAD_W)).T

    def head(h, _):
        sl = pl.ds(pl.multiple_of(h * HEAD_W, HEAD_W), HEAD_W)
        log_f, kk = _gates(f_ref[:, sl], lb_ref[:, sl])
        q, v, g = q_ref[:, sl], i_ref[:, sl], g_ref[:, sl]
        decay = jnp.exp(log_f)
        q_dec = q * decay
        a = jnp.sum(q_dec * (kk * jnp.exp(-log_f)), axis=-1, keepdims=True)
        inter = []
        for b in range(bs):
            s0 = s0_ref[b, h]
            inter.append(jnp.sum(column(q_dec[b:b + 1]) * s0, axis=0, keepdims=True))
            s1_ref[b, h] = column(decay[b:b + 1]) * s0 + column(kk[b:b + 1]) * v[b:b + 1]
        o = a * v + jnp.concatenate(inter, axis=0)
        o_ref[:, sl] = (_rms(o, gain_ref[...]) * (g * jax.nn.sigmoid(g))).astype(BF16)
        return 0

    lax.fori_loop(0, H_RNN, head, 0)


def hgrn_step(z, lb, rnn_gain, s0, bs_pref=16):
    DB = z.shape[0]
    bs = _tile(DB, bs_pref)
    col = lambda c: pl.BlockSpec((bs, D_HEADS), lambda i: (i, c))
    st = pl.BlockSpec((bs, H_RNN, HEAD_W, HEAD_W), lambda i: (i, 0, 0, 0))
    return pl.pallas_call(
        functools.partial(_hgrn_step_kernel, bs=bs),
        grid=(DB // bs,),
        in_specs=[col(3), col(4), col(5), col(6),
                  pl.BlockSpec((1, D_HEADS), lambda i: (0, 0)),
                  pl.BlockSpec((1, HEAD_W), lambda i: (0, 0)), st],
        out_specs=(pl.BlockSpec((bs, D_HEADS), lambda i: (i, 0)), st),
        out_shape=(jax.ShapeDtypeStruct((DB, D_HEADS), BF16), jax.ShapeDtypeStruct(s0.shape, F32)),
        compiler_params=_cparams("parallel"),
    )(z, z, z, z, lb, rnn_gain, s0)


def _out_proj_kernel(a_ref, r_ref, wa_ref, wr_ref, x_ref, g_ref, x1_ref, h1_ref):
    x1 = x_ref[...] + _dot(a_ref[...], wa_ref[...]) + _dot(r_ref[...], wr_ref[...])
    x1_ref[...] = x1
    h1_ref[...] = _rms(x1, g_ref[...]).astype(BF16)


def out_proj(o_att, o_rnn, w_out, x, ffn_gain, tm_pref=512):
    M, D = x.shape
    tm = _tile(M, tm_pref)
    act = pl.BlockSpec((tm, D_HEADS), lambda i: (i, 0))
    full = pl.BlockSpec((tm, D), lambda i: (i, 0))
    return pl.pallas_call(
        _out_proj_kernel,
        grid=(M // tm,),
        in_specs=[act, act,
                  pl.BlockSpec((D_HEADS, D), lambda i: (0, 0)),
                  pl.BlockSpec((D_HEADS, D), lambda i: (1, 0)),
                  full, pl.BlockSpec((1, D), lambda i: (0, 0))],
        out_specs=(full, full),
        out_shape=(jax.ShapeDtypeStruct((M, D), F32), jax.ShapeDtypeStruct((M, D), BF16)),
        compiler_params=_cparams("parallel"),
    )(o_att, o_rnn, w_out, w_out, x, ffn_gain)


def _conv_gate(g, u, first_tile, cw_ref, cb_ref, act_ref, tail_ref, carry_ref):
    @pl.when(first_tile)
    def _():
        carry_ref[...] = jnp.zeros_like(carry_ref)

    tm = g.shape[0]
    r = lax.broadcasted_iota(jnp.int32, g.shape, 0)
    prev2, prev1 = carry_ref[0:1], carry_ref[1:2]
    g1 = jnp.where(r == 0, prev1, pltpu.roll(g, 1, 0))
    g2 = jnp.where(r == 0, prev2, jnp.where(r == 1, prev1, pltpu.roll(g, 2, 0)))
    cw = cw_ref[...]
    c = cb_ref[...] + g2 * cw[0:1] + g1 * cw[1:2] + g * cw[2:3]
    act_ref[...] = (c * jax.nn.sigmoid(c) * u).astype(BF16)
    tail = g[tm - (CONV_W - 1):]
    carry_ref[0:CONV_W - 1] = tail
    tail_ref[0] = tail


def _ffn_seq_kernel(h_ref, wg_ref, wu_ref, cw_ref, cb_ref, act_ref, tail_ref, carry_ref, *, tiles_per_seq):
    h = h_ref[...]
    g = _dot(h, wg_ref[...])
    u = _dot(h, wu_ref[...])
    _conv_gate(g, u, pl.program_id(1) % tiles_per_seq == 0, cw_ref, cb_ref, act_ref, tail_ref, carry_ref)


def _ffn_decode_kernel(pt_ref, h_ref, wg_ref, wu_ref, cw_ref, cb_ref, lam_ref, q_ref, kn_ref, vn_ref, g_ref,
                       *refs, n_pages, tiles_m, tiles_per_seq, out_scale):
    del pt_ref
    k_refs, v_refs = refs[:n_pages], refs[n_pages:2 * n_pages]
    act_ref, tail_ref, o_ref, carry_ref = refs[2 * n_pages:]
    first_tile = (pl.program_id(0) % tiles_m) % tiles_per_seq == 0
    qm, own_head = _decode_query(q_ref)
    s = [_page_scores(qm, own_head, k_ref) for k_ref in k_refs]
    h = h_ref[...]
    g = _dot(h, wg_ref[...])
    u = _dot(h, wu_ref[...])
    p, p_new, l = _decode_weights(qm, s, kn_ref)
    values = [_page_values(pj, v_ref) for pj, v_ref in zip(p, v_refs)]
    o_ref[0] = _decode_finish(values, p_new, l, vn_ref, lam_ref, g_ref, out_scale)
    _conv_gate(g, u, first_tile, cw_ref, cb_ref, act_ref, tail_ref, carry_ref)


def ffn_seq_decode_tiles(M, S, F, DB, tm_pref=256, tn_pref=1408):
    tm, tn = _tile(S, tm_pref), _tile(F, tn_pref)
    return (tm, tn) if tn % HEAD_W == 0 and (M // tm) * (F // tn) == DB else None


def ffn_seq_decode(h, w_gate, w_up, conv_w, conv_b, B, S, tiles,
                   q, k_new, v_new, cache_k, cache_v, page_table, lam, att_gain, out_scale):
    M, D = h.shape
    F = w_gate.shape[1]
    DB, n_pages = page_table.shape
    tm, tn = tiles
    tiles_m, tiles_per_seq = M // tm, S // tm
    row = lambda i: i % tiles_m
    col = lambda i: i // tiles_m
    wspec = pl.BlockSpec((D, tn), lambda i, pt: (0, col(i)), pipeline_mode=pl.Buffered(1))
    tok = pl.BlockSpec((1, H_ATT, HEAD_W), lambda i, pt: (i, 0, 0))
    vec = pl.BlockSpec((1, HEAD_W), lambda i, pt: (0, 0))
    page = lambda j: pl.BlockSpec((1, PAGE_SIZE, H_ATT, HEAD_W),
                                  lambda i, pt: (pt[i * n_pages + j], 0, 0, 0))
    pages = [page(j) for j in range(n_pages)]
    return pl.pallas_call(
        functools.partial(_ffn_decode_kernel, n_pages=n_pages, tiles_m=tiles_m,
                          tiles_per_seq=tiles_per_seq, out_scale=out_scale),
        grid_spec=pltpu.PrefetchScalarGridSpec(
            num_scalar_prefetch=1,
            grid=(DB,),
            in_specs=[pl.BlockSpec((tm, D), lambda i, pt: (row(i), 0)), wspec, wspec,
                      pl.BlockSpec((CONV_W, tn), lambda i, pt: (0, col(i))),
                      pl.BlockSpec((1, tn), lambda i, pt: (0, col(i))),
                      vec, tok, tok, tok, vec] + pages + pages,
            out_specs=(pl.BlockSpec((tm, tn), lambda i, pt: (row(i), col(i))),
                       pl.BlockSpec((1, CONV_W - 1, tn), lambda i, pt: (row(i) // tiles_per_seq, 0, col(i))),
                       tok),
            scratch_shapes=[pltpu.VMEM((8, tn), F32)],
        ),
        out_shape=(jax.ShapeDtypeStruct((M, F), BF16), jax.ShapeDtypeStruct((B, CONV_W - 1, F), F32),
                   jax.ShapeDtypeStruct((DB, H_ATT, HEAD_W), F32)),
        compiler_params=_cparams("arbitrary"),
    )(page_table.reshape(-1), h, w_gate, w_up, conv_w, conv_b, lam, q, k_new, v_new, att_gain,
      *([cache_k] * n_pages), *([cache_v] * n_pages))


def ffn_seq(h, w_gate, w_up, conv_w, conv_b, B, S, tm_pref=512, tn_pref=1408):
    M, D = h.shape
    F = w_gate.shape[1]
    tm, tn = _tile(S, tm_pref), _tile(F, tn_pref)
    tiles_per_seq = S // tm
    wspec = pl.BlockSpec((D, tn), lambda n, m: (0, n))
    return pl.pallas_call(
        functools.partial(_ffn_seq_kernel, tiles_per_seq=tiles_per_seq),
        grid=(F // tn, M // tm),
        in_specs=[pl.BlockSpec((tm, D), lambda n, m: (m, 0)), wspec, wspec,
                  pl.BlockSpec((CONV_W, tn), lambda n, m: (0, n)),
                  pl.BlockSpec((1, tn), lambda n, m: (0, n))],
        out_specs=(pl.BlockSpec((tm, tn), lambda n, m: (m, n)),
                   pl.BlockSpec((1, CONV_W - 1, tn), lambda n, m: (m // tiles_per_seq, 0, n))),
        out_shape=(jax.ShapeDtypeStruct((M, F), BF16), jax.ShapeDtypeStruct((B, CONV_W - 1, F), F32)),
        scratch_shapes=[pltpu.VMEM((8, tn), F32)],
        compiler_params=_cparams("parallel", "arbitrary"),
    )(h, w_gate, w_up, conv_w, conv_b)


def _ffn_step_kernel(h_ref, wg_ref, wu_ref, cw_ref, cb_ref, p2_ref, p1_ref, act_ref, g_ref):
    h = h_ref[...]
    g = _dot(h, wg_ref[...])
    u = _dot(h, wu_ref[...])
    cw = cw_ref[...]
    c = cb_ref[...] + p2_ref[...] * cw[0:1] + p1_ref[...] * cw[1:2] + g * cw[2:3]
    act_ref[...] = (c * jax.nn.sigmoid(c) * u).astype(BF16)
    g_ref[...] = g


def ffn_step(h, w_gate, w_up, conv_w, conv_b, prev2, prev1, tn_pref=1408):
    M, D = h.shape
    F = w_gate.shape[1]
    tn = _tile(F, tn_pref)
    wspec = pl.BlockSpec((D, tn), lambda n: (0, n))
    tile = pl.BlockSpec((M, tn), lambda n: (0, n))
    return pl.pallas_call(
        _ffn_step_kernel,
        grid=(F // tn,),
        in_specs=[pl.BlockSpec((M, D), lambda n: (0, 0)), wspec, wspec,
                  pl.BlockSpec((CONV_W, tn), lambda n: (0, n)),
                  pl.BlockSpec((1, tn), lambda n: (0, n)), tile, tile],
        out_specs=(tile, tile),
        out_shape=(jax.ShapeDtypeStruct((M, F), BF16), jax.ShapeDtypeStruct((M, F), F32)),
        compiler_params=_cparams("parallel"),
    )(h, w_gate, w_up, conv_w, conv_b, prev2, prev1)


def _down_kernel(a_ref, w_ref, x_ref, o_ref):
    o_ref[...] = x_ref[...] + _dot(a_ref[...], w_ref[...])


def ffn_down(act, w_down, x, tm_pref=1024, tn_pref=512):
    M, F = act.shape
    D = w_down.shape[1]
    tm, tn = _tile(M, tm_pref), _tile(D, tn_pref)
    tile = pl.BlockSpec((tm, tn), lambda i, j: (i, j))
    return pl.pallas_call(
        _down_kernel,
        grid=(M // tm, D // tn),
        in_specs=[pl.BlockSpec((tm, F), lambda i, j: (i, 0)),
                  pl.BlockSpec((F, tn), lambda i, j: (0, j)), tile],
        out_specs=tile,
        out_shape=jax.ShapeDtypeStruct((M, D), F32),
        compiler_params=_cparams("parallel", "arbitrary"),
    )(act, w_down, x)


def _ple_kernel(x_ref, g_ref, wg_ref, p_ref, wp_ref, o_ref, h_ref, *, tn):
    j = pl.program_id(1)

    @pl.when(j == 0)
    def _():
        h_ref[...] = _rms(x_ref[...], g_ref[...]).astype(BF16)

    gate = jax.nn.sigmoid(_dot(h_ref[...], wg_ref[...]))
    emb = _dot(p_ref[...].astype(BF16), wp_ref[...])
    o_ref[...] = x_ref[:, pl.ds(pl.multiple_of(j * tn, tn), tn)] + gate * emb


def ple(x, gain, w_gate, p, w_proj, tm_pref=1024, tn_pref=1024):
    M, D = x.shape
    DP = p.shape[1]
    tm, tn = _tile(M, tm_pref), _tile(D, tn_pref)
    return pl.pallas_call(
        functools.partial(_ple_kernel, tn=tn),
        grid=(M // tm, D // tn),
        in_specs=[pl.BlockSpec((tm, D), lambda i, j: (i, 0)),
                  pl.BlockSpec((1, D), lambda i, j: (0, 0)),
                  pl.BlockSpec((D, tn), lambda i, j: (0, j)),
                  pl.BlockSpec((tm, DP), lambda i, j: (i, 0)),
                  pl.BlockSpec((DP, tn), lambda i, j: (0, j))],
        out_specs=pl.BlockSpec((tm, tn), lambda i, j: (i, j)),
        out_shape=jax.ShapeDtypeStruct((M, D), F32),
        scratch_shapes=[pltpu.VMEM((tm, D), BF16)],
        compiler_params=_cparams("parallel", "arbitrary"),
    )(x, gain, w_gate, p, w_proj)


def _rope_tables(pos):
    half = DK_ATT // 2
    inv_freq = 1.0 / (ROPE_THETA ** (jnp.arange(half, dtype=F32) * (2.0 / DK_ATT)))
    ang = pos.astype(F32)[:, None] * inv_freq[None, :]
    cos, sin = jnp.cos(ang), jnp.sin(ang)
    return (jnp.concatenate([cos] * 4, axis=-1), jnp.concatenate([-sin, sin] * 2, axis=-1))


def _row(v):
    return v.reshape(1, -1).astype(F32)


def kernel(x_prompt, x_sample, p_prompt, p_sample, cache_k, cache_v, state_rnn, state_ffn_conv, page_table,
           norm_mix, w_in, q_norm, k_norm, lam_q1, lam_k1, lam_q2, lam_k2, att_out_norm, rnn_out_norm,
           lower_bounds, w_out, norm_ffn, w_gate, w_up, conv_w, conv_b, w_down, norm_ple, w_ple_gate,
           w_ple_proj):
    B, S, D = x_prompt.shape
    DB, T, _ = x_sample.shape
    assert T == 1, "the sample group is implemented for one new token per sequence"
    depth = w_in.shape[0]
    past = page_table.shape[1] * PAGE_SIZE
    cos_p, sin_p = _rope_tables(jnp.arange(S, dtype=jnp.int32))
    cos_p, sin_p = jnp.tile(cos_p, (B, 1)), jnp.tile(sin_p, (B, 1))
    cos_s, sin_s = _rope_tables(jnp.full((DB,), past, dtype=jnp.int32))

    yp = x_prompt.reshape(B * S, D)
    ys = x_sample.reshape(DB, D)
    outs = [[] for _ in range(8)]
    for i in range(depth):
        lam_init = 0.8 - 0.6 * math.exp(-0.3 * i)
        out_scale = 1.0 - lam_init
        lam, lb = layer_params(jnp.stack([lam_q1[i], lam_k1[i], lam_q2[i], lam_k2[i]]).astype(F32),
                               lower_bounds.astype(F32), i, lam_init)
        w_out_b = w_out[i].astype(BF16)
        w_gate_b, w_up_b, w_down_b = w_gate[i].astype(BF16), w_up[i].astype(BF16), w_down[i].astype(BF16)
        w_pg_b, w_pp_b = w_ple_gate[i].astype(BF16), w_ple_proj[i].astype(BF16)
        q_gain = jnp.tile(_row(q_norm[i]), (1, 2))
        k_gain = jnp.tile(_row(k_norm[i]), (1, 2))
        att_gain, rnn_gain = _row(att_out_norm[i]), _row(rnn_out_norm[i])

        def mix_in(x, cos, sin, seqs=None):
            z = norm_matmul(x, _row(norm_mix[i]), w_in[i])
            return (z,) + tuple(qk_prep(z, cos, sin, q_gain, k_gain, seqs))

        def mix_out(x, o_att, o_rnn):
            return out_proj(o_att, o_rnn, w_out_b, x, _row(norm_ffn[i]))

        def embed(x, act, p):
            x = ffn_down(act, w_down_b, x)
            return ple(x, _row(norm_ple[i]), w_pg_b, p, w_pp_b)

        z, qt, k, kb, v, vt = mix_in(yp, cos_p, sin_p, (B, S))
        o_att = attn_prompt(qt, kb.reshape(B, S, -1), vt, lam, att_gain, out_scale).reshape(B * S, -1)
        o_rnn, rnn_p = hgrn_prompt(z, lb, rnn_gain, B, S)
        x1, h1 = mix_out(yp, o_att, o_rnn)
        outs[0].append(k.reshape(B, S // PAGE_SIZE, PAGE_SIZE, H_ATT, HEAD_W))
        outs[1].append(v.reshape(B, S // PAGE_SIZE, PAGE_SIZE, H_ATT, HEAD_W))
        outs[2].append(rnn_p.astype(state_rnn.dtype))

        z, q, k, kb, v, vb = mix_in(ys, cos_s, sin_s)
        per_head = lambda a: a.reshape(DB, H_ATT, HEAD_W)
        decode_args = (per_head(q), per_head(k), per_head(v), cache_k[i], cache_v[i], page_table,
                       lam, att_gain, out_scale)
        ffn_args = (h1, w_gate_b, w_up_b, conv_w[i], _row(conv_b[i]), B, S)
        tiles = ffn_seq_decode_tiles(B * S, S, w_gate_b.shape[1], DB)
        if tiles is not None:
            act, conv_p, o_att = ffn_seq_decode(*ffn_args, tiles, *decode_args)
        else:
            act, conv_p = ffn_seq(*ffn_args)
            o_att = attn_sample(*decode_args)
        o_att = o_att.reshape(DB, D_HEADS).astype(BF16)
        yp = embed(x1, act, p_prompt[i].reshape(B * S, -1))
        outs[3].append(conv_p)
        o_rnn, rnn_s = hgrn_step(z, lb, rnn_gain, state_rnn[i].astype(F32))
        x1, h1 = mix_out(ys, o_att, o_rnn)
        buf = state_ffn_conv[i]
        act, g_new = ffn_step(h1, w_gate_b, w_up_b, conv_w[i], _row(conv_b[i]), buf[:, 0], buf[:, 1])
        ys = embed(x1, act, p_sample[i].reshape(DB, -1))
        outs[4].append(k.reshape(DB, T, H_ATT, HEAD_W))
        outs[5].append(v.reshape(DB, T, H_ATT, HEAD_W))
        outs[6].append(rnn_s.astype(state_rnn.dtype))
        outs[7].append(jnp.stack([buf[:, 1], g_new], axis=1))

    return (yp.reshape(B, S, D), ys.reshape(DB, T, D)) + tuple(jnp.stack(o) for o in outs)
```

```python
import functools
import math

import jax
import jax.numpy as jnp
from jax import lax
from jax.experimental import pallas as pl
from jax.experimental.pallas import tpu as pltpu

H_ATT = 8
DK_ATT = 64
H_RNN = 8
HEAD_W = 128
D_HEADS = H_ATT * HEAD_W
PAGE_SIZE = 128
CONV_W = 3
ROPE_THETA = 10000.0
EPS = 1e-6
NEG_INF = -1e30
RNN_CHUNK = 32
RNN_ROWS = 256
ATT_HEADS_PER_STEP = 8
VT_ROWS = HEAD_W + 16
Q_SCALE = DK_ATT ** -0.5 * math.log2(math.e)
VMEM_LIMIT_V7X = 56 * 1024 * 1024

F32 = jnp.float32
BF16 = jnp.bfloat16


def _cparams(*sem):
    return pltpu.CompilerParams(dimension_semantics=sem, vmem_limit_bytes=VMEM_LIMIT_V7X)


def _tile(n, pref):
    t = min(n, pref)
    while n % t:
        t //= 2
    return t


def _split3(x):
    hi = x.astype(BF16)
    r1 = x - hi.astype(F32)
    mid = r1.astype(BF16)
    lo = (r1 - mid.astype(F32)).astype(BF16)
    return hi, mid, lo


def _dot(a, b):
    return jnp.dot(a, b, preferred_element_type=F32)


def _dot_nt(a, b):
    return lax.dot_general(a, b, (((1,), (1,)), ((), ())), preferred_element_type=F32)


def _dot_tn(a, b):
    return lax.dot_general(a, b, (((0,), (0,)), ((), ())), preferred_element_type=F32)


def _rms(x, gain):
    return x * lax.rsqrt(jnp.mean(x * x, axis=-1, keepdims=True) + EPS) * gain


def _params_kernel(lam_ref, lb_in_ref, lam_out_ref, lb_out_ref, *, layer, lam_init):
    v = lam_ref[...]
    s1 = jnp.sum(v[0:1] * v[1:2], axis=-1, keepdims=True)
    s2 = jnp.sum(v[2:3] * v[3:4], axis=-1, keepdims=True)
    lam = jnp.exp(s1) - jnp.exp(s2) + lam_init
    lam_out_ref[...] = jnp.broadcast_to(lam, lam_out_ref.shape)
    lb = lb_in_ref[...]
    e = jnp.exp(lb - jnp.max(lb, axis=0, keepdims=True))
    sm = e / jnp.sum(e, axis=0, keepdims=True)
    lb_out_ref[...] = jnp.sum(sm[0:layer + 1], axis=0, keepdims=True)


def layer_params(lam_vecs, lower_bounds, layer, lam_init):
    return pl.pallas_call(
        functools.partial(_params_kernel, layer=layer, lam_init=lam_init),
        out_shape=(jax.ShapeDtypeStruct((1, HEAD_W), F32),
                   jax.ShapeDtypeStruct((1, lower_bounds.shape[1]), F32)),
    )(lam_vecs, lower_bounds)


def _norm_matmul_kernel(x_ref, g_ref, w_ref, o_ref, h_ref):
    @pl.when(pl.program_id(1) == 0)
    def _():
        h_ref[...] = _rms(x_ref[...], g_ref[...]).astype(BF16)

    o_ref[...] = _dot(h_ref[...], w_ref[...])


def norm_matmul(x, gain, w, tm_pref=1024, tn_pref=1024):
    M, K = x.shape
    N = w.shape[1]
    tm, tn = _tile(M, tm_pref), _tile(N, tn_pref)
    return pl.pallas_call(
        _norm_matmul_kernel,
        grid=(M // tm, N // tn),
        in_specs=[pl.BlockSpec((tm, K), lambda i, j: (i, 0)),
                  pl.BlockSpec((1, K), lambda i, j: (0, 0)),
                  pl.BlockSpec((K, tn), lambda i, j: (0, j))],
        out_specs=pl.BlockSpec((tm, tn), lambda i, j: (i, j)),
        out_shape=jax.ShapeDtypeStruct((M, N), F32),
        scratch_shapes=[pltpu.VMEM((tm, K), BF16)],
        compiler_params=_cparams("parallel", "arbitrary"),
    )(x, gain, w)


def _qk_prep_kernel(zq_ref, zk_ref, zv_ref, cos_ref, sin_ref, qg_ref, kg_ref, grp_ref,
                    q_ref, k_ref, kb_ref, v_ref, vb_ref, *, transposed):
    cos = cos_ref[...]
    sin = sin_ref[...]
    lane = lax.broadcasted_iota(jnp.int32, cos.shape, 1)
    first_half = (lane % DK_ATT) < (DK_ATT // 2)
    grp = grp_ref[...]

    def norm_rope(x, gain):
        hi, mid, lo = _split3(x * x)
        ss = (_dot(hi, grp) + _dot(mid, grp) + _dot(lo, grp)) * (1.0 / DK_ATT)
        y = x * lax.rsqrt(ss + EPS) * gain
        swapped = jnp.where(first_half, pltpu.roll(y, HEAD_W - DK_ATT // 2, 1),
                            pltpu.roll(y, DK_ATT // 2, 1))
        return y * cos + swapped * sin

    for h in range(H_ATT):
        sl = slice(h * HEAD_W, (h + 1) * HEAD_W)
        q = norm_rope(zq_ref[:, sl], qg_ref[...]) * Q_SCALE
        k = norm_rope(zk_ref[:, sl], kg_ref[...])
        k_ref[:, sl] = k
        kb_ref[:, sl] = k.astype(BF16)
        v = zv_ref[:, sl]
        v_ref[:, sl] = v
        if transposed:
            q_ref[0, sl, :] = q.T.astype(BF16)
            r0 = h * VT_ROWS
            vb_ref[0, r0:r0 + HEAD_W, :] = v.T.astype(BF16)
            pad = (VT_ROWS - HEAD_W, v.shape[0])
            vb_ref[0, r0 + HEAD_W:r0 + VT_ROWS, :] = (lax.broadcasted_iota(jnp.int32, pad, 0) == 0).astype(BF16)
        else:
            q_ref[:, sl] = q.astype(BF16)
            vb_ref[:, sl] = v.astype(BF16)


def qk_prep(z, cos, sin, q_gain, k_gain, seqs=None, tm_pref=512):
    M = z.shape[0]
    tm = _tile(M if seqs is None else seqs[1], tm_pref)
    lane = jnp.arange(HEAD_W)
    grp = (lane[:, None] // DK_ATT == lane[None, :] // DK_ATT).astype(BF16)
    col = lambda c: pl.BlockSpec((tm, D_HEADS), lambda i: (i, c))
    row = pl.BlockSpec((tm, HEAD_W), lambda i: (i, 0))
    vec = pl.BlockSpec((1, HEAD_W), lambda i: (0, 0))
    out = pl.BlockSpec((tm, D_HEADS), lambda i: (i, 0))
    sd = lambda dt: jax.ShapeDtypeStruct((M, D_HEADS), dt)
    if seqs is None:
        out_q, out_v, sd_q, sd_v = out, out, sd(BF16), sd(BF16)
    else:
        B, S = seqs
        tiles = S // tm
        out_q = pl.BlockSpec((1, D_HEADS, tm), lambda i: (i // tiles, 0, i % tiles))
        out_v = pl.BlockSpec((1, H_ATT * VT_ROWS, tm), lambda i: (i // tiles, 0, i % tiles))
        sd_q = jax.ShapeDtypeStruct((B, D_HEADS, S), BF16)
        sd_v = jax.ShapeDtypeStruct((B, H_ATT * VT_ROWS, S), BF16)
    return pl.pallas_call(
        functools.partial(_qk_prep_kernel, transposed=seqs is not None),
        grid=(M // tm,),
        in_specs=[col(0), col(1), col(2), row, row, vec, vec,
                  pl.BlockSpec((HEAD_W, HEAD_W), lambda i: (0, 0))],
        out_specs=(out_q, out, out, out, out_v),
        out_shape=(sd_q, sd(F32), sd(BF16), sd(F32), sd_v),
        compiler_params=_cparams("parallel"),
    )(z, z, z, cos, sin, q_gain, k_gain, grp)


def _attn_prompt_kernel(lam_ref, qt_ref, k_ref, vt_ref, g_ref, o_ref, m_scr, acc_scr, *, tq, tk, out_scale):
    qi = pl.program_id(2)
    n_full = (qi * tq) // tk
    dim = lax.broadcasted_iota(jnp.int32, (HEAD_W, tq), 0)
    q2t = []
    for u in range(ATT_HEADS_PER_STEP):
        qt = qt_ref[0, u * HEAD_W:(u + 1) * HEAD_W, :]
        zero = jnp.zeros_like(qt)
        q2t.append(jnp.concatenate([jnp.where(dim < DK_ATT, qt, zero),
                                    jnp.where(dim >= DK_ATT, qt, zero)], axis=1))
    m_scr[...] = jnp.full(m_scr.shape, NEG_INF, F32)
    acc_scr[...] = jnp.zeros(acc_scr.shape, F32)

    def block(kb, masked):
        start = pl.multiple_of(kb * tk, tk)

        def scores(u):
            return _dot(k_ref[0, pl.ds(start, tk), u * HEAD_W:(u + 1) * HEAD_W], q2t[u])

        s_next = scores(0)
        for u in range(ATT_HEADS_PER_STEP):
            s = s_next
            if u + 1 < ATT_HEADS_PER_STEP:
                s_next = scores(u + 1)
            if masked:
                key = start + lax.broadcasted_iota(jnp.int32, s.shape, 0)
                qry = qi * tq + lax.broadcasted_iota(jnp.int32, s.shape, 1) % tq
                s = jnp.where(key <= qry, s, NEG_INF)
            m_prev = m_scr[u]
            m_new = jnp.maximum(m_prev, jnp.max(s, axis=0, keepdims=True))
            m_scr[u] = m_new
            p = jnp.exp2(s - m_new).astype(BF16)
            vt = vt_ref[0, u * VT_ROWS:(u + 1) * VT_ROWS, pl.ds(start, tk)]
            acc_scr[u] = jnp.exp2(m_prev - m_new) * acc_scr[u] + _dot(vt, p)

    def body(kb, carry):
        block(kb, False)
        return carry

    lax.fori_loop(0, n_full, body, 0)
    block(n_full, True)
    for u in range(ATT_HEADS_PER_STEP):
        acc = acc_scr[u]
        o = acc[:HEAD_W] / acc[HEAD_W:HEAD_W + 1]
        od = o[:, :tq] - lam_ref[:, :1] * o[:, tq:]
        y = od * lax.rsqrt(jnp.mean(od * od, axis=0, keepdims=True) + EPS) * g_ref[...] * out_scale
        o_ref[0, :, u * HEAD_W:(u + 1) * HEAD_W] = y.T.astype(BF16)


def attn_prompt(qt, k, vt, lam, att_gain, out_scale, tq_pref=256, tk_pref=256):
    B, S, _ = k.shape
    tq = _tile(S, tq_pref)
    tk = max(tq, _tile(S, tk_pref))
    U = ATT_HEADS_PER_STEP
    return pl.pallas_call(
        functools.partial(_attn_prompt_kernel, tq=tq, tk=tk, out_scale=out_scale),
        grid=(B, H_ATT // U, S // tq),
        in_specs=[pl.BlockSpec((1, HEAD_W), lambda b, h, i: (0, 0)),
                  pl.BlockSpec((1, U * HEAD_W, tq), lambda b, h, i: (b, h, i)),
                  pl.BlockSpec((1, S, U * HEAD_W), lambda b, h, i: (b, 0, h)),
                  pl.BlockSpec((1, U * VT_ROWS, S), lambda b, h, i: (b, h, 0)),
                  pl.BlockSpec((HEAD_W, 1), lambda b, h, i: (0, 0))],
        out_specs=pl.BlockSpec((1, tq, U * HEAD_W), lambda b, h, i: (b, i, h)),
        out_shape=jax.ShapeDtypeStruct(k.shape, BF16),
        scratch_shapes=[pltpu.VMEM((U, 1, 2 * tq), F32), pltpu.VMEM((U, VT_ROWS, 2 * tq), F32)],
        compiler_params=_cparams("parallel", "parallel", "arbitrary"),
    )(lam, qt, k, vt, att_gain.reshape(HEAD_W, 1))


PAGE_KEYS = PAGE_SIZE * H_ATT


def _decode_query(q_ref):
    rows = 2 * H_ATT
    q = q_ref[0]
    q16 = jnp.concatenate([q, q], axis=0)
    r = lax.broadcasted_iota(jnp.int32, (rows, HEAD_W), 0)
    c = lax.broadcasted_iota(jnp.int32, (rows, HEAD_W), 1)
    qm = jnp.where(c // DK_ATT == r // H_ATT, q16, jnp.zeros_like(q16))
    own_head = (lax.broadcasted_iota(jnp.int32, (rows, PAGE_KEYS), 1) % H_ATT
                == lax.broadcasted_iota(jnp.int32, (rows, PAGE_KEYS), 0) % H_ATT)
    return qm, own_head


def _page_scores(qm, own_head, k_ref):
    return jnp.where(own_head, _dot_nt(qm, k_ref[0].reshape(PAGE_KEYS, HEAD_W).astype(BF16)), NEG_INF)


def _decode_weights(qm, s, kn_ref):
    k_new = jnp.concatenate([kn_ref[0], kn_ref[0]], axis=0)
    s_new = jnp.sum(qm.astype(F32) * k_new, axis=-1, keepdims=True)
    m = jnp.maximum(jnp.max(functools.reduce(jnp.maximum, s), axis=-1, keepdims=True), s_new)
    p = [jnp.exp2(sj - m) for sj in s]
    p_new = jnp.exp2(s_new - m)
    l = jnp.sum(functools.reduce(jnp.add, p), axis=-1, keepdims=True) + p_new
    return p, p_new, l


def _page_values(p, v_ref):
    return _dot(p.astype(BF16), v_ref[0].reshape(PAGE_KEYS, HEAD_W).astype(BF16))


def _decode_finish(values, p_new, l, vn_ref, lam_ref, g_ref, out_scale):
    v_new = jnp.concatenate([vn_ref[0], vn_ref[0]], axis=0)
    o = (functools.reduce(jnp.add, values) + p_new * v_new) / l
    od = o[:H_ATT] - lam_ref[...] * o[H_ATT:]
    return _rms(od, g_ref[...]) * out_scale


def _attn_sample_kernel(pt_ref, lam_ref, q_ref, kn_ref, vn_ref, g_ref, *refs, n_pages, out_scale):
    del pt_ref
    k_refs, v_refs, o_ref = refs[:n_pages], refs[n_pages:2 * n_pages], refs[2 * n_pages]
    qm, own_head = _decode_query(q_ref)
    p, p_new, l = _decode_weights(qm, [_page_scores(qm, own_head, k_ref) for k_ref in k_refs], kn_ref)
    values = [_page_values(pj, v_ref) for pj, v_ref in zip(p, v_refs)]
    o_ref[0] = _decode_finish(values, p_new, l, vn_ref, lam_ref, g_ref, out_scale)


def attn_sample(q, k_new, v_new, cache_k, cache_v, page_table, lam, att_gain, out_scale):
    DB = q.shape[0]
    n_pages = page_table.shape[1]
    tok =pl.BlockSpec((1, H_ATT, HEAD_W), lambda b, pt: (b, 0, 0))
    vec = pl.BlockSpec((1, HEAD_W), lambda b, pt: (0, 0))
    page = lambda j: pl.BlockSpec((1, PAGE_SIZE, H_ATT, HEAD_W),
                                  lambda b, pt: (pt[b * n_pages + j], 0, 0, 0))
    pages = [page(j) for j in range(n_pages)]
    return pl.pallas_call(
        functools.partial(_attn_sample_kernel, n_pages=n_pages, out_scale=out_scale),
        grid_spec=pltpu.PrefetchScalarGridSpec(
            num_scalar_prefetch=1,
            grid=(DB,),
            in_specs=[vec, tok, tok, tok, vec] + pages + pages,
            out_specs=tok,
        ),
        out_shape=jax.ShapeDtypeStruct((DB, H_ATT, HEAD_W), F32),
        compiler_params=_cparams("arbitrary"),
    )(page_table.reshape(-1), lam, q, k_new, v_new, att_gain,
      *([cache_k] * n_pages), *([cache_v] * n_pages))


def _gates(zf, lb):
    sig = jax.nn.sigmoid(zf)
    f = lb + (1.0 - lb) * sig
    return jnp.log(f), (1.0 - lb) * (1.0 - sig)


def _hgrn_prompt_kernel(q_ref, f_ref, i_ref, g_ref, lb_ref, gain_ref, tril_ref, o_ref, st_ref, s_scr,
                        *, rows, chunk):
    t = pl.program_id(1)

    @pl.when(t == 0)
    def _():
        s_scr[...] = jnp.zeros_like(s_scr)

    tril = tril_ref[...]
    r = lax.broadcasted_iota(jnp.int32, (rows, rows), 0)
    c = lax.broadcasted_iota(jnp.int32, (rows, rows), 1)
    band = (c <= r) & (c // chunk == r // chunk)
    mid = chunk // 2 - 1
    n_chunks = rows // chunk

    def head(h, _):
        sl = pl.ds(pl.multiple_of(h * HEAD_W, HEAD_W), HEAD_W)
        log_f, kk = _gates(f_ref[:, sl], lb_ref[:, sl])
        hi, md, lo = _split3(log_f)
        bc = _dot(tril, hi) + _dot(tril, md) + _dot(tril, lo)
        q, g = q_ref[:, sl], g_ref[:, sl]
        v = i_ref[:, sl].astype(BF16)
        q_mid, k_mid, q_dec, k_end, decay = [], [], [], [], []
        for c in range(n_chunks):
            r0 = c * chunk
            bl = bc[r0:r0 + chunk]
            if c:
                bl = bl - bc[r0 - 1:r0]
            qc, kc = q[r0:r0 + chunk], kk[r0:r0 + chunk]
            b_mid, b_last = bl[mid:mid + 1], bl[chunk - 1:chunk]
            q_mid.append(qc * jnp.exp(bl - b_mid))
            k_mid.append(kc * jnp.exp(b_mid - bl))
            q_dec.append((qc * jnp.exp(bl)).astype(BF16))
            k_end.append((kc * jnp.exp(b_last - bl)).astype(BF16))
            decay.append(jnp.exp(b_last))
        a = _dot_nt(jnp.concatenate(q_mid, axis=0).astype(BF16), jnp.concatenate(k_mid, axis=0).astype(BF16))
        intra = _dot(jnp.where(band, a, 0.0).astype(BF16), v)
        update = [_dot_tn(v[c * chunk:(c + 1) * chunk], k_end[c]) for c in range(n_chunks)]
        st = s_scr[h]
        for c in range(n_chunks):
            r0 = c * chunk
            o = intra[r0:r0 + chunk] + _dot_nt(q_dec[c], st.astype(BF16))
            st = st * decay[c] + update[c]
            y = _rms(o, gain_ref[...]) * (g[r0:r0 + chunk] * jax.nn.sigmoid(g[r0:r0 + chunk]))
            o_ref[r0:r0 + chunk, sl] = y.astype(BF16)
        s_scr[h] = st
        return 0

    lax.fori_loop(0, H_RNN, head, 0, unroll=4)

    @pl.when(t == pl.num_programs(1) - 1)
    def _():
        for h in range(H_RNN):
            st_ref[0, h] = s_scr[h].T


def hgrn_prompt(z, lb, rnn_gain, B, S):
    rows = _tile(S, RNN_ROWS)
    chunk = _tile(rows, RNN_CHUNK)
    nt = S // rows
    tril = (jnp.arange(rows)[:, None] >= jnp.arange(rows)[None, :]).astype(BF16)
    col = lambda c: pl.BlockSpec((rows, D_HEADS), lambda b, t: (b * nt + t, c))
    return pl.pallas_call(
        functools.partial(_hgrn_prompt_kernel, rows=rows, chunk=chunk),
        grid=(B, nt),
        in_specs=[col(3), col(4), col(5), col(6),
                  pl.BlockSpec((1, D_HEADS), lambda b, t: (0, 0)),
                  pl.BlockSpec((1, HEAD_W), lambda b, t: (0, 0)),
                  pl.BlockSpec((rows, rows), lambda b, t: (0, 0))],
        out_specs=(pl.BlockSpec((rows, D_HEADS), lambda b, t: (b * nt + t, 0)),
                   pl.BlockSpec((1, H_RNN, HEAD_W, HEAD_W), lambda b, t: (b, 0, 0, 0))),
        out_shape=(jax.ShapeDtypeStruct((B * S, D_HEADS), BF16),
                   jax.ShapeDtypeStruct((B, H_RNN, HEAD_W, HEAD_W), F32)),
        scratch_shapes=[pltpu.VMEM((H_RNN, HEAD_W, HEAD_W), F32)],
        compiler_params=_cparams("parallel", "arbitrary"),
    )(z, z, z, z, lb, rnn_gain, tril)


def _hgrn_step_kernel(q_ref, f_ref, i_ref, g_ref, lb_ref, gain_ref, s0_ref, o_ref, s1_ref, *, bs):
    def column(rowvec):
        return jnp.broadcast_to(rowvec, (HEAD_W, HEAD_W)).T

    def head(h, _):
        sl = pl.ds(pl.multiple_of(h * HEAD_W, HEAD_W), HEAD_W)
        log_f, kk = _gates(f_ref[:, sl], lb_ref[:, sl])
        q, v, g = q_ref[:, sl], i_ref[:, sl], g_ref[:, sl]
        decay = jnp.exp(log_f)
        q_dec = q * decay
        a = jnp.sum(q_dec * (kk * jnp.exp(-log_f)), axis=-1, keepdims=True)
        inter = []
        for b in range(bs):
            s0 = s0_ref[b, h]
            q_rows = jnp.broadcast_to(q_dec[b:b + 1], (16, HEAD_W)).astype(BF16)
            inter.append(_dot(q_rows, s0.astype(BF16))[0:1])
            s1_ref[b, h] = column(decay[b:b + 1]) * s0 + column(kk[b:b + 1]) * v[b:b + 1]
        o = a * v + jnp.concatenate(inter, axis=0)
        o_ref[:, sl] = (_rms(o, gain_ref[...]) * (g * jax.nn.sigmoid(g))).astype(BF16)
        return 0

    lax.fori_loop(0, H_RNN, head, 0)


def hgrn_step(z, lb, rnn_gain, s0, bs_pref=16):
    DB = z.shape[0]
    bs = _tile(DB, bs_pref)
    col = lambda c: pl.BlockSpec((bs, D_HEADS), lambda i: (i, c))
    st = pl.BlockSpec((bs, H_RNN, HEAD_W, HEAD_W), lambda i: (i, 0, 0, 0))
    return pl.pallas_call(
        functools.partial(_hgrn_step_kernel, bs=bs),
        grid=(DB // bs,),
        in_specs=[col(3), col(4), col(5), col(6),
                  pl.BlockSpec((1, D_HEADS), lambda i: (0, 0)),
                  pl.BlockSpec((1, HEAD_W), lambda i: (0, 0)), st],
        out_specs=(pl.BlockSpec((bs, D_HEADS), lambda i: (i, 0)), st),
        out_shape=(jax.ShapeDtypeStruct((DB, D_HEADS), BF16), jax.ShapeDtypeStruct(s0.shape, F32)),
        compiler_params=_cparams("parallel"),
    )(z, z, z, z, lb, rnn_gain, s0)


def _out_proj_kernel(a_ref, r_ref, wa_ref, wr_ref, x_ref, g_ref, x1_ref, h1_ref):
    x1 = x_ref[...] + _dot(a_ref[...], wa_ref[...]) + _dot(r_ref[...], wr_ref[...])
    x1_ref[...] = x1
    h1_ref[...] = _rms(x1, g_ref[...]).astype(BF16)


def out_proj(o_att, o_rnn, w_out, x, ffn_gain, tm_pref=512):
    M, D = x.shape
    tm = _tile(M, tm_pref)
    act = pl.BlockSpec((tm, D_HEADS), lambda i: (i, 0))
    full = pl.BlockSpec((tm, D), lambda i: (i, 0))
    return pl.pallas_call(
        _out_proj_kernel,
        grid=(M // tm,),
        in_specs=[act, act,
                  pl.BlockSpec((D_HEADS, D), lambda i: (0, 0)),
                  pl.BlockSpec((D_HEADS, D), lambda i: (1, 0)),
                  full, pl.BlockSpec((1, D), lambda i: (0, 0))],
        out_specs=(full, full),
        out_shape=(jax.ShapeDtypeStruct((M, D), F32), jax.ShapeDtypeStruct((M, D), BF16)),
        compiler_params=_cparams("parallel"),
    )(o_att, o_rnn, w_out, w_out, x, ffn_gain)


def _conv_gate(g, u, first_tile, cw_ref, cb_ref, act_ref, tail_ref, carry_ref):
    @pl.when(first_tile)
    def _():
        carry_ref[...] = jnp.zeros_like(carry_ref)

    tm = g.shape[0]
    r = lax.broadcasted_iota(jnp.int32, g.shape, 0)
    prev2, prev1 = carry_ref[0:1], carry_ref[1:2]
    g1 = jnp.where(r == 0, prev1, pltpu.roll(g, 1, 0))
    g2 = jnp.where(r == 0, prev2, jnp.where(r == 1, prev1, pltpu.roll(g, 2, 0)))
    cw = cw_ref[...]
    c = cb_ref[...] + g2 * cw[0:1] + g1 * cw[1:2] + g * cw[2:3]
    act_ref[...] = (c * jax.nn.sigmoid(c) * u).astype(BF16)
    tail = g[tm - (CONV_W - 1):]
    carry_ref[0:CONV_W - 1] = tail
    tail_ref[0] = tail


def _ffn_seq_kernel(h_ref, wg_ref, wu_ref, cw_ref, cb_ref, act_ref, tail_ref, carry_ref, *, tiles_per_seq):
    h = h_ref[...]
    g = _dot(h, wg_ref[...])
    u = _dot(h, wu_ref[...])
    _conv_gate(g, u, pl.program_id(1) % tiles_per_seq == 0, cw_ref, cb_ref, act_ref, tail_ref, carry_ref)


def _ffn_decode_kernel(pt_ref, h_ref, wg_ref, wu_ref, cw_ref, cb_ref, lam_ref, q_ref, kn_ref, vn_ref, g_ref,
                       *refs, n_pages, tiles_m, tiles_per_seq, out_scale):
    del pt_ref
    k_refs, v_refs = refs[:n_pages], refs[n_pages:2 * n_pages]
    act_ref, tail_ref, o_ref, carry_ref = refs[2 * n_pages:]
    first_tile = (pl.program_id(0) % tiles_m) % tiles_per_seq == 0
    qm, own_head = _decode_query(q_ref)
    s = [_page_scores(qm, own_head, k_ref) for k_ref in k_refs]
    h = h_ref[...]
    g = _dot(h, wg_ref[...])
    u = _dot(h, wu_ref[...])
    p, p_new, l = _decode_weights(qm, s, kn_ref)
    values = [_page_values(pj, v_ref) for pj, v_ref in zip(p, v_refs)]
    o_ref[0] = _decode_finish(values, p_new, l, vn_ref, lam_ref, g_ref, out_scale)
    _conv_gate(g, u, first_tile, cw_ref, cb_ref, act_ref, tail_ref, carry_ref)


def ffn_seq_decode_tiles(M, S, F, DB, tm_pref=256, tn_pref=1408):
    tm, tn = _tile(S, tm_pref), _tile(F, tn_pref)
    return (tm, tn) if tn % HEAD_W == 0 and (M // tm) * (F // tn) == DB else None


def ffn_seq_decode(h, w_gate, w_up, conv_w, conv_b, B, S, tiles,
                   q, k_new, v_new, cache_k, cache_v, page_table, lam, att_gain, out_scale):
    M, D = h.shape
    F = w_gate.shape[1]
    DB, n_pages = page_table.shape
    tm, tn = tiles
    tiles_m, tiles_per_seq = M // tm, S // tm
    row = lambda i: i % tiles_m
    col = lambda i: i // tiles_m
    wspec = pl.BlockSpec((D, tn), lambda i, pt: (0, col(i)), pipeline_mode=pl.Buffered(1))
    tok = pl.BlockSpec((1, H_ATT, HEAD_W), lambda i, pt: (i, 0, 0))
    vec = pl.BlockSpec((1, HEAD_W), lambda i, pt: (0, 0))
    page = lambda j: pl.BlockSpec((1, PAGE_SIZE, H_ATT, HEAD_W),
                                  lambda i, pt: (pt[i * n_pages + j], 0, 0, 0))
    pages = [page(j) for j in range(n_pages)]
    return pl.pallas_call(
        functools.partial(_ffn_decode_kernel, n_pages=n_pages, tiles_m=tiles_m,
                          tiles_per_seq=tiles_per_seq, out_scale=out_scale),
        grid_spec=pltpu.PrefetchScalarGridSpec(
            num_scalar_prefetch=1,
            grid=(DB,),
            in_specs=[pl.BlockSpec((tm, D), lambda i, pt: (row(i), 0)), wspec, wspec,
                      pl.BlockSpec((CONV_W, tn), lambda i, pt: (0, col(i))),
                      pl.BlockSpec((1, tn), lambda i, pt: (0, col(i))),
                      vec, tok, tok, tok, vec] + pages + pages,
            out_specs=(pl.BlockSpec((tm, tn), lambda i, pt: (row(i), col(i))),
                       pl.BlockSpec((1, CONV_W - 1, tn), lambda i, pt: (row(i) // tiles_per_seq, 0, col(i))),
                       tok),
            scratch_shapes=[pltpu.VMEM((8, tn), F32)],
        ),
        out_shape=(jax.ShapeDtypeStruct((M, F), BF16), jax.ShapeDtypeStruct((B, CONV_W - 1, F), F32),
                   jax.ShapeDtypeStruct((DB, H_ATT, HEAD_W), F32)),
        compiler_params=_cparams("arbitrary"),
    )(page_table.reshape(-1), h, w_gate, w_up, conv_w, conv_b, lam, q, k_new, v_new, att_gain,
      *([cache_k] * n_pages), *([cache_v] * n_pages))


def ffn_seq(h, w_gate, w_up, conv_w, conv_b, B, S, tm_pref=512, tn_pref=1408):
    M, D = h.shape
    F = w_gate.shape[1]
    tm, tn = _tile(S, tm_pref), _tile(F, tn_pref)
    tiles_per_seq = S // tm
    wspec = pl.BlockSpec((D, tn), lambda n, m: (0, n))
    return pl.pallas_call(
        functools.partial(_ffn_seq_kernel, tiles_per_seq=tiles_per_seq),
        grid=(F // tn, M // tm),
        in_specs=[pl.BlockSpec((tm, D), lambda n, m: (m, 0)), wspec, wspec,
                  pl.BlockSpec((CONV_W, tn), lambda n, m: (0, n)),
                  pl.BlockSpec((1, tn), lambda n, m: (0, n))],
        out_specs=(pl.BlockSpec((tm, tn), lambda n, m: (m, n)),
                   pl.BlockSpec((1, CONV_W - 1, tn), lambda n, m: (m // tiles_per_seq, 0, n))),
        out_shape=(jax.ShapeDtypeStruct((M, F), BF16), jax.ShapeDtypeStruct((B, CONV_W - 1, F), F32)),
        scratch_shapes=[pltpu.VMEM((8, tn), F32)],
        compiler_params=_cparams("parallel", "arbitrary"),
    )(h, w_gate, w_up, conv_w, conv_b)


def _ffn_step_kernel(h_ref, wg_ref, wu_ref, cw_ref, cb_ref, p2_ref, p1_ref, act_ref, g_ref):
    h = h_ref[...]
    g = _dot(h, wg_ref[...])
    u = _dot(h, wu_ref[...])
    cw = cw_ref[...]
    c = cb_ref[...] + p2_ref[...] * cw[0:1] + p1_ref[...] * cw[1:2] + g * cw[2:3]
    act_ref[...] = (c * jax.nn.sigmoid(c) * u).astype(BF16)
    g_ref[...] = g


def ffn_step(h, w_gate, w_up, conv_w, conv_b, prev2, prev1, tn_pref=1408):
    M, D = h.shape
    F = w_gate.shape[1]
    tn = _tile(F, tn_pref)
    wspec = pl.BlockSpec((D, tn), lambda n: (0, n))
    tile = pl.BlockSpec((M, tn), lambda n: (0, n))
    return pl.pallas_call(
        _ffn_step_kernel,
        grid=(F // tn,),
        in_specs=[pl.BlockSpec((M, D), lambda n: (0, 0)), wspec, wspec,
                  pl.BlockSpec((CONV_W, tn), lambda n: (0, n)),
                  pl.BlockSpec((1, tn), lambda n: (0, n)), tile, tile],
        out_specs=(tile, tile),
        out_shape=(jax.ShapeDtypeStruct((M, F), BF16), jax.ShapeDtypeStruct((M, F), F32)),
        compiler_params=_cparams("parallel"),
    )(h, w_gate, w_up, conv_w, conv_b, prev2, prev1)


def _down_kernel(a_ref, w_ref, x_ref, o_ref):
    o_ref[...] = x_ref[...] + _dot(a_ref[...], w_ref[...])


def ffn_down(act, w_down, x, tm_pref=1024, tn_pref=512):
    M, F = act.shape
    D = w_down.shape[1]
    tm, tn = _tile(M, tm_pref), _tile(D, tn_pref)
    tile = pl.BlockSpec((tm, tn), lambda i, j: (i, j))
    return pl.pallas_call(
        _down_kernel,
        grid=(M // tm, D // tn),
        in_specs=[pl.BlockSpec((tm, F), lambda i, j: (i, 0)),
                  pl.BlockSpec((F, tn), lambda i, j: (0, j)), tile],
        out_specs=tile,
        out_shape=jax.ShapeDtypeStruct((M, D), F32),
        compiler_params=_cparams("parallel", "arbitrary"),
    )(act, w_down, x)


def _ple_kernel(x_ref, g_ref, wg_ref, p_ref, wp_ref, o_ref, h_ref, *, tn):
    j = pl.program_id(1)

    @pl.when(j == 0)
    def _():
        h_ref[...] = _rms(x_ref[...], g_ref[...]).astype(BF16)

    gate = jax.nn.sigmoid(_dot(h_ref[...], wg_ref[...]))
    emb = _dot(p_ref[...].astype(BF16), wp_ref[...])
    o_ref[...] = x_ref[:, pl.ds(pl.multiple_of(j * tn, tn), tn)] + gate * emb


def ple(x, gain, w_gate, p, w_proj, tm_pref=1024, tn_pref=1024):
    M, D = x.shape
    DP = p.shape[1]
    tm, tn = _tile(M, tm_pref), _tile(D, tn_pref)
    return pl.pallas_call(
        functools.partial(_ple_kernel, tn=tn),
        grid=(M // tm, D // tn),
        in_specs=[pl.BlockSpec((tm, D), lambda i, j: (i, 0)),
                  pl.BlockSpec((1, D), lambda i, j: (0, 0)),
                  pl.BlockSpec((D, tn), lambda i, j: (0, j)),
                  pl.BlockSpec((tm, DP), lambda i, j: (i, 0)),
                  pl.BlockSpec((DP, tn), lambda i, j: (0, j))],
        out_specs=pl.BlockSpec((tm, tn), lambda i, j: (i, j)),
        out_shape=jax.ShapeDtypeStruct((M, D), F32),
        scratch_shapes=[pltpu.VMEM((tm, D), BF16)],
        compiler_params=_cparams("parallel", "arbitrary"),
    )(x, gain, w_gate, p, w_proj)


def _rope_tables(pos):
    half = DK_ATT // 2
    inv_freq = 1.0 / (ROPE_THETA ** (jnp.arange(half, dtype=F32) * (2.0 / DK_ATT)))
    ang = pos.astype(F32)[:, None] * inv_freq[None, :]
    cos, sin = jnp.cos(ang), jnp.sin(ang)
    return (jnp.concatenate([cos] * 4, axis=-1), jnp.concatenate([-sin, sin] * 2, axis=-1))


def _row(v):
    return v.reshape(1, -1).astype(F32)


def kernel(x_prompt, x_sample, p_prompt, p_sample, cache_k, cache_v, state_rnn, state_ffn_conv, page_table,
           norm_mix, w_in, q_norm, k_norm, lam_q1, lam_k1, lam_q2, lam_k2, att_out_norm, rnn_out_norm,
           lower_bounds, w_out, norm_ffn, w_gate, w_up, conv_w, conv_b, w_down, norm_ple, w_ple_gate,
           w_ple_proj):
    B, S, D = x_prompt.shape
    DB, T, _ = x_sample.shape
    assert T == 1, "the sample group is implemented for one new token per sequence"
    depth = w_in.shape[0]
    past = page_table.shape[1] * PAGE_SIZE
    cos_p, sin_p = _rope_tables(jnp.arange(S, dtype=jnp.int32))
    cos_p, sin_p = jnp.tile(cos_p, (B, 1)), jnp.tile(sin_p, (B, 1))
    cos_s, sin_s = _rope_tables(jnp.full((DB,), past, dtype=jnp.int32))

    yp = x_prompt.reshape(B * S, D)
    ys = x_sample.reshape(DB, D)
    outs = [[] for _ in range(8)]
    for i in range(depth):
        lam_init = 0.8 - 0.6 * math.exp(-0.3 * i)
        out_scale = 1.0 - lam_init
        lam, lb = layer_params(jnp.stack([lam_q1[i], lam_k1[i], lam_q2[i], lam_k2[i]]).astype(F32),
                               lower_bounds.astype(F32), i, lam_init)
        w_in_b, w_out_b = w_in[i].astype(BF16), w_out[i].astype(BF16)
        w_gate_b, w_up_b, w_down_b = w_gate[i].astype(BF16), w_up[i].astype(BF16), w_down[i].astype(BF16)
        w_pg_b, w_pp_b = w_ple_gate[i].astype(BF16), w_ple_proj[i].astype(BF16)
        q_gain = jnp.tile(_row(q_norm[i]), (1, 2))
        k_gain = jnp.tile(_row(k_norm[i]), (1, 2))
        att_gain, rnn_gain = _row(att_out_norm[i]), _row(rnn_out_norm[i])

        def mix_in(x, cos, sin, seqs=None):
            z = norm_matmul(x, _row(norm_mix[i]), w_in_b)
            return (z,) + tuple(qk_prep(z, cos, sin, q_gain, k_gain, seqs))

        def mix_out(x, o_att, o_rnn):
            return out_proj(o_att, o_rnn, w_out_b, x, _row(norm_ffn[i]))

        def embed(x, act, p):
            x = ffn_down(act, w_down_b, x)
            return ple(x, _row(norm_ple[i]), w_pg_b, p, w_pp_b)

        z, qt, k, kb, v, vt = mix_in(yp, cos_p, sin_p, (B, S))
        o_att = attn_prompt(qt, kb.reshape(B, S, -1), vt, lam, att_gain, out_scale).reshape(B * S, -1)
        o_rnn, rnn_p = hgrn_prompt(z, lb, rnn_gain, B, S)
        x1, h1 = mix_out(yp, o_att, o_rnn)
        outs[0].append(k.reshape(B, S // PAGE_SIZE, PAGE_SIZE, H_ATT, HEAD_W))
        outs[1].append(v.reshape(B, S // PAGE_SIZE, PAGE_SIZE, H_ATT, HEAD_W))
        outs[2].append(rnn_p.astype(state_rnn.dtype))

        z, q, k, kb, v, vb = mix_in(ys, cos_s, sin_s)
        per_head = lambda a: a.reshape(DB, H_ATT, HEAD_W)
        decode_args = (per_head(q), per_head(k), per_head(v), cache_k[i], cache_v[i], page_table,
                       lam, att_gain, out_scale)
        ffn_args = (h1, w_gate_b, w_up_b, conv_w[i], _row(conv_b[i]), B, S)
        tiles = ffn_seq_decode_tiles(B * S, S, w_gate_b.shape[1], DB)
        if tiles is not None:
            act, conv_p, o_att = ffn_seq_decode(*ffn_args, tiles, *decode_args)
        else:
            act, conv_p = ffn_seq(*ffn_args)
            o_att = attn_sample(*decode_args)
        o_att = o_att.reshape(DB, D_HEADS).astype(BF16)
        yp = embed(x1, act, p_prompt[i].reshape(B * S, -1))
        outs[3].append(conv_p)
        o_rnn, rnn_s = hgrn_step(z, lb, rnn_gain, state_rnn[i].astype(F32))
        x1, h1 = mix_out(ys, o_att, o_rnn)
        buf = state_ffn_conv[i]
        act, g_new = ffn_step(h1, w_gate_b, w_up_b, conv_w[i], _row(conv_b[i]), buf[:, 0], buf[:, 1])
        ys = embed(x1, act, p_sample[i].reshape(DB, -1))
        outs[4].append(k.reshape(DB, T, H_ATT, HEAD_W))
        outs[5].append(v.reshape(DB, T, H_ATT, HEAD_W))
        outs[6].append(rnn_s.astype(state_rnn.dtype))
        outs[7].append(jnp.stack([buf[:, 1], g_new], axis=1))

    return (yp.reshape(B, S, D), ys.reshape(DB, T, D)) + tuple(jnp.stack(o) for o in outs)
```

```python
import functools
import math

import jax
import jax.numpy as jnp
from jax import lax
from jax.experimental import pallas as pl
from jax.experimental.pallas import tpu as pltpu

H_ATT = 8
DK_ATT = 64
H_RNN = 8
HEAD_W = 128
D_HEADS = H_ATT * HEAD_W
PAGE_SIZE = 128
CONV_W = 3
ROPE_THETA = 10000.0
EPS = 1e-6
NEG_INF = -1e30
RNN_CHUNK = 32
RNN_ROWS = 256
ATT_HEADS_PER_STEP = 4
VT_ROWS = HEAD_W + 16
Q_SCALE = DK_ATT ** -0.5 * math.log2(math.e)
VMEM_LIMIT_V7X = 56 * 1024 * 1024

F32 = jnp.float32
BF16 = jnp.bfloat16


def _cparams(*sem):
    return pltpu.CompilerParams(dimension_semantics=sem, vmem_limit_bytes=VMEM_LIMIT_V7X)


def _tile(n, pref):
    t = min(n, pref)
    while n % t:
        t //= 2
    return t


def _split3(x):
    hi = x.astype(BF16)
    r1 = x - hi.astype(F32)
    mid = r1.astype(BF16)
    lo = (r1 - mid.astype(F32)).astype(BF16)
    return hi, mid, lo


def _dot(a, b):
    return jnp.dot(a, b, preferred_element_type=F32)


def _dot_nt(a, b):
    return lax.dot_general(a, b, (((1,), (1,)), ((), ())), preferred_element_type=F32)


def _dot_tn(a, b):
    return lax.dot_general(a, b, (((0,), (0,)), ((), ())), preferred_element_type=F32)


def _rms(x, gain):
    return x * lax.rsqrt(jnp.mean(x * x, axis=-1, keepdims=True) + EPS) * gain


def _params_kernel(lam_ref, lb_in_ref, lam_out_ref, lb_out_ref, *, layer, lam_init):
    v = lam_ref[...]
    s1 = jnp.sum(v[0:1] * v[1:2], axis=-1, keepdims=True)
    s2 = jnp.sum(v[2:3] * v[3:4], axis=-1, keepdims=True)
    lam = jnp.exp(s1) - jnp.exp(s2) + lam_init
    lam_out_ref[...] = jnp.broadcast_to(lam, lam_out_ref.shape)
    lb = lb_in_ref[...]
    e = jnp.exp(lb - jnp.max(lb, axis=0, keepdims=True))
    sm = e / jnp.sum(e, axis=0, keepdims=True)
    lb_out_ref[...] = jnp.sum(sm[0:layer + 1], axis=0, keepdims=True)


def layer_params(lam_vecs, lower_bounds, layer, lam_init):
    return pl.pallas_call(
        functools.partial(_params_kernel, layer=layer, lam_init=lam_init),
        out_shape=(jax.ShapeDtypeStruct((1, HEAD_W), F32),
                   jax.ShapeDtypeStruct((1, lower_bounds.shape[1]), F32)),
    )(lam_vecs, lower_bounds)


def _norm_matmul_kernel(x_ref, g_ref, w_ref, o_ref, h_ref):
    @pl.when(pl.program_id(1) == 0)
    def _():
        h_ref[...] = _rms(x_ref[...], g_ref[...]).astype(BF16)

    o_ref[...] = _dot(h_ref[...], w_ref[...])


def norm_matmul(x, gain, w, tm_pref=1024, tn_pref=1024):
    M, K = x.shape
    N = w.shape[1]
    tm, tn = _tile(M, tm_pref), _tile(N, tn_pref)
    return pl.pallas_call(
        _norm_matmul_kernel,
        grid=(M // tm, N // tn),
        in_specs=[pl.BlockSpec((tm, K), lambda i, j: (i, 0)),
                  pl.BlockSpec((1, K), lambda i, j: (0, 0)),
                  pl.BlockSpec((K, tn), lambda i, j: (0, j))],
        out_specs=pl.BlockSpec((tm, tn), lambda i, j: (i, j)),
        out_shape=jax.ShapeDtypeStruct((M, N), F32),
        scratch_shapes=[pltpu.VMEM((tm, K), BF16)],
        compiler_params=_cparams("parallel", "arbitrary"),
    )(x, gain, w)


def _qk_prep_kernel(zq_ref, zk_ref, zv_ref, cos_ref, sin_ref, qg_ref, kg_ref, grp_ref,
                    q_ref, k_ref, kb_ref, v_ref, vb_ref, *, transposed):
    cos = cos_ref[...]
    sin = sin_ref[...]
    lane = lax.broadcasted_iota(jnp.int32, cos.shape, 1)
    first_half = (lane % DK_ATT) < (DK_ATT // 2)
    grp = grp_ref[...]

    def norm_rope(x, gain):
        hi, mid, lo = _split3(x * x)
        ss = (_dot(hi, grp) + _dot(mid, grp) + _dot(lo, grp)) * (1.0 / DK_ATT)
        y = x * lax.rsqrt(ss + EPS) * gain
        swapped = jnp.where(first_half, pltpu.roll(y, HEAD_W - DK_ATT // 2, 1),
                            pltpu.roll(y, DK_ATT // 2, 1))
        return y * cos + swapped * sin

    for h in range(H_ATT):
        sl = slice(h * HEAD_W, (h + 1) * HEAD_W)
        q = norm_rope(zq_ref[:, sl], qg_ref[...]) * Q_SCALE
        k = norm_rope(zk_ref[:, sl], kg_ref[...])
        k_ref[:, sl] = k
        kb_ref[:, sl] = k.astype(BF16)
        v = zv_ref[:, sl]
        v_ref[:, sl] = v
        if transposed:
            q_ref[0, sl, :] = q.T.astype(BF16)
            r0 = h * VT_ROWS
            vb_ref[0, r0:r0 + HEAD_W, :] = v.T.astype(BF16)
            pad = (VT_ROWS - HEAD_W, v.shape[0])
            vb_ref[0, r0 + HEAD_W:r0 + VT_ROWS, :] = (lax.broadcasted_iota(jnp.int32, pad, 0) == 0).astype(BF16)
        else:
            q_ref[:, sl] = q.astype(BF16)
            vb_ref[:, sl] = v.astype(BF16)


def qk_prep(z, cos, sin, q_gain, k_gain, seqs=None, tm_pref=512):
    M = z.shape[0]
    tm = _tile(M if seqs is None else seqs[1], tm_pref)
    lane = jnp.arange(HEAD_W)
    grp = (lane[:, None] // DK_ATT == lane[None, :] // DK_ATT).astype(BF16)
    col = lambda c: pl.BlockSpec((tm, D_HEADS), lambda i: (i, c))
    row = pl.BlockSpec((tm, HEAD_W), lambda i: (i, 0))
    vec = pl.BlockSpec((1, HEAD_W), lambda i: (0, 0))
    out = pl.BlockSpec((tm, D_HEADS), lambda i: (i, 0))
    sd = lambda dt: jax.ShapeDtypeStruct((M, D_HEADS), dt)
    if seqs is None:
        out_q, out_v, sd_q, sd_v = out, out, sd(BF16), sd(BF16)
    else:
        B, S = seqs
        tiles = S // tm
        out_q = pl.BlockSpec((1, D_HEADS, tm), lambda i: (i // tiles, 0, i % tiles))
        out_v = pl.BlockSpec((1, H_ATT * VT_ROWS, tm), lambda i: (i // tiles, 0, i % tiles))
        sd_q = jax.ShapeDtypeStruct((B, D_HEADS, S), BF16)
        sd_v = jax.ShapeDtypeStruct((B, H_ATT * VT_ROWS, S), BF16)
    return pl.pallas_call(
        functools.partial(_qk_prep_kernel, transposed=seqs is not None),
        grid=(M // tm,),
        in_specs=[col(0), col(1), col(2), row, row, vec, vec,
                  pl.BlockSpec((HEAD_W, HEAD_W), lambda i: (0, 0))],
        out_specs=(out_q, out, out, out, out_v),
        out_shape=(sd_q, sd(F32), sd(BF16), sd(F32), sd_v),
        compiler_params=_cparams("parallel"),
    )(z, z, z, cos, sin, q_gain, k_gain, grp)


def _attn_prompt_kernel(lam_ref, qt_ref, k_ref, vt_ref, g_ref, o_ref, m_scr, acc_scr, *, tq, tk, out_scale):
    qi = pl.program_id(2)
    n_full = (qi * tq) // tk
    dim = lax.broadcasted_iota(jnp.int32, (HEAD_W, tq), 0)
    q2t = []
    for u in range(ATT_HEADS_PER_STEP):
        qt = qt_ref[0, u * HEAD_W:(u + 1) * HEAD_W, :]
        zero = jnp.zeros_like(qt)
        q2t.append(jnp.concatenate([jnp.where(dim < DK_ATT, qt, zero),
                                    jnp.where(dim >= DK_ATT, qt, zero)], axis=1))
    m_scr[...] = jnp.full(m_scr.shape, NEG_INF, F32)
    acc_scr[...] = jnp.zeros(acc_scr.shape, F32)

    def block(kb, masked):
        start = pl.multiple_of(kb * tk, tk)

        def scores(u):
            return _dot(k_ref[0, pl.ds(start, tk), u * HEAD_W:(u + 1) * HEAD_W], q2t[u])

        s_next = scores(0)
        if masked:
            key = start + lax.broadcasted_iota(jnp.int32, s_next.shape, 0)
            qry = qi * tq + lax.broadcasted_iota(jnp.int32, s_next.shape, 1) % tq
            causal = key <= qry
        for u in range(ATT_HEADS_PER_STEP):
            s = s_next
            if u + 1 < ATT_HEADS_PER_STEP:
                s_next = scores(u + 1)
            if masked:
                s = jnp.where(causal, s, NEG_INF)
            m_prev = m_scr[u]
            m_new = jnp.maximum(m_prev, jnp.max(s, axis=0, keepdims=True))
            m_scr[u] = m_new
            p = jnp.exp2(s - m_new).astype(BF16)
            vt = vt_ref[0, u * VT_ROWS:(u + 1) * VT_ROWS, pl.ds(start, tk)]
            acc_scr[u] = jnp.exp2(m_prev - m_new) * acc_scr[u] + _dot(vt, p)

    def body(kb, carry):
        block(kb, False)
        return carry

    lax.fori_loop(0, n_full, body, 0)
    block(n_full, True)
    for u in range(ATT_HEADS_PER_STEP):
        acc = acc_scr[u]
        o = acc[:HEAD_W] / acc[HEAD_W:HEAD_W + 1]
        od = o[:, :tq] - lam_ref[:, :1] * o[:, tq:]
        y = od * lax.rsqrt(jnp.mean(od * od, axis=0, keepdims=True) + EPS) * g_ref[...] * out_scale
        o_ref[0, :, u * HEAD_W:(u + 1) * HEAD_W] = y.T.astype(BF16)


def attn_prompt(qt, k, vt, lam, att_gain, out_scale, tq_pref=256, tk_pref=512):
    B, S, _ = k.shape
    tq = _tile(S, tq_pref)
    tk = max(tq, _tile(S, tk_pref))
    U = ATT_HEADS_PER_STEP
    return pl.pallas_call(
        functools.partial(_attn_prompt_kernel, tq=tq, tk=tk, out_scale=out_scale),
        grid=(B, H_ATT // U, S // tq),
        in_specs=[pl.BlockSpec((1, HEAD_W), lambda b, h, i: (0, 0)),
                  pl.BlockSpec((1, U * HEAD_W, tq), lambda b, h, i: (b, h, i)),
                  pl.BlockSpec((1, S, U * HEAD_W), lambda b, h, i: (b, 0, h)),
                  pl.BlockSpec((1, U * VT_ROWS, S), lambda b, h, i: (b, h, 0)),
                  pl.BlockSpec((HEAD_W, 1), lambda b, h, i: (0, 0))],
        out_specs=pl.BlockSpec((1, tq, U * HEAD_W), lambda b, h, i: (b, i, h)),
        out_shape=jax.ShapeDtypeStruct(k.shape, BF16),
        scratch_shapes=[pltpu.VMEM((U, 1, 2 * tq), F32), pltpu.VMEM((U, VT_ROWS, 2 * tq), F32)],
        compiler_params=_cparams("parallel", "parallel", "arbitrary"),
    )(lam, qt, k, vt, att_gain.reshape(HEAD_W, 1))


PAGE_KEYS = PAGE_SIZE * H_ATT


def _decode_query(q_ref):
    rows = 2 * H_ATT
    q = q_ref[0]
    q16 = jnp.concatenate([q, q], axis=0)
    r = lax.broadcasted_iota(jnp.int32, (rows, HEAD_W), 0)
    c = lax.broadcasted_iota(jnp.int32, (rows, HEAD_W), 1)
    qm = jnp.where(c // DK_ATT == r // H_ATT, q16, jnp.zeros_like(q16))
    own_head = (lax.broadcasted_iota(jnp.int32, (rows, PAGE_KEYS), 1) % H_ATT
                == lax.broadcasted_iota(jnp.int32, (rows, PAGE_KEYS), 0) % H_ATT)
    return qm, own_head


def _page_scores(qm, own_head, k_ref):
    return jnp.where(own_head, _dot_nt(qm, k_ref[0].reshape(PAGE_KEYS, HEAD_W).astype(BF16)), NEG_INF)


def _decode_weights(qm, s, kn_ref):
    k_new = jnp.concatenate([kn_ref[0], kn_ref[0]], axis=0)
    s_new = jnp.sum(qm.astype(F32) * k_new, axis=-1, keepdims=True)
    m = jnp.maximum(jnp.max(functools.reduce(jnp.maximum, s), axis=-1, keepdims=True), s_new)
    p = [jnp.exp2(sj - m) for sj in s]
    p_new = jnp.exp2(s_new - m)
    l = jnp.sum(functools.reduce(jnp.add, p), axis=-1, keepdims=True) + p_new
    return p, p_new, l


def _page_values(p, v_ref):
    return _dot(p.astype(BF16), v_ref[0].reshape(PAGE_KEYS, HEAD_W).astype(BF16))


def _decode_finish(values, p_new, l, vn_ref, lam_ref, g_ref, out_scale):
    v_new = jnp.concatenate([vn_ref[0], vn_ref[0]], axis=0)
    o = (functools.reduce(jnp.add, values) + p_new * v_new) / l
    od = o[:H_ATT] - lam_ref[...] * o[H_ATT:]
    return _rms(od, g_ref[...]) * out_scale


def _attn_sample_kernel(pt_ref, lam_ref, q_ref, kn_ref, vn_ref, g_ref, *refs, n_pages, out_scale):
    del pt_ref
    k_refs, v_refs, o_ref = refs[:n_pages], refs[n_pages:2 * n_pages], refs[2 * n_pages]
    qm, own_head = _decode_query(q_ref)
    p, p_new, l = _decode_weights(qm, [_page_scores(qm, own_head, k_ref) for k_ref in k_refs], kn_ref)
    values = [_page_values(pj, v_ref) for pj, v_ref in zip(p, v_refs)]
    o_ref[0] = _decode_finish(values, p_new, l, vn_ref, lam_ref, g_ref, out_scale)


def attn_sample(q, k_new, v_new, cache_k, cache_v, page_table, lam, att_gain, out_scale):
    DB = q.shape[0]
    n_pages = page_table.shape[1]
    tok =pl.BlockSpec((1, H_ATT, HEAD_W), lambda b, pt: (b, 0, 0))
    vec = pl.BlockSpec((1, HEAD_W), lambda b, pt: (0, 0))
    page = lambda j: pl.BlockSpec((1, PAGE_SIZE, H_ATT, HEAD_W),
                                  lambda b, pt: (pt[b * n_pages + j], 0, 0, 0))
    pages = [page(j) for j in range(n_pages)]
    return pl.pallas_call(
        functools.partial(_attn_sample_kernel, n_pages=n_pages, out_scale=out_scale),
        grid_spec=pltpu.PrefetchScalarGridSpec(
            num_scalar_prefetch=1,
            grid=(DB,),
            in_specs=[vec, tok, tok, tok, vec] + pages + pages,
            out_specs=tok,
        ),
        out_shape=jax.ShapeDtypeStruct((DB, H_ATT, HEAD_W), F32),
        compiler_params=_cparams("arbitrary"),
    )(page_table.reshape(-1), lam, q, k_new, v_new, att_gain,
      *([cache_k] * n_pages), *([cache_v] * n_pages))


def _gates(zf, lb):
    sig = jax.nn.sigmoid(zf)
    f = lb + (1.0 - lb) * sig
    return jnp.log(f), (1.0 - lb) * (1.0 - sig)


def _hgrn_prompt_kernel(q_ref, f_ref, i_ref, g_ref, lb_ref, gain_ref, tril_ref, o_ref, st_ref, s_scr,
                        *, rows, chunk):
    t = pl.program_id(1)

    @pl.when(t == 0)
    def _():
        s_scr[...] = jnp.zeros_like(s_scr)

    tril = tril_ref[...]
    r = lax.broadcasted_iota(jnp.int32, (rows, rows), 0)
    c = lax.broadcasted_iota(jnp.int32, (rows, rows), 1)
    band = (c <= r) & (c // chunk == r // chunk)
    mid = chunk // 2 - 1
    n_chunks = rows // chunk

    def head(h, _):
        sl = pl.ds(pl.multiple_of(h * HEAD_W, HEAD_W), HEAD_W)
        log_f, kk = _gates(f_ref[:, sl], lb_ref[:, sl])
        hi, md, lo = _split3(log_f)
        bc = _dot(tril, hi) + _dot(tril, md) + _dot(tril, lo)
        q, g = q_ref[:, sl], g_ref[:, sl]
        v = i_ref[:, sl].astype(BF16)
        q_mid, k_mid, q_dec, k_end, decay = [], [], [], [], []
        for c in range(n_chunks):
            r0 = c * chunk
            bl = bc[r0:r0 + chunk]
            if c:
                bl = bl - bc[r0 - 1:r0]
            qc, kc = q[r0:r0 + chunk], kk[r0:r0 + chunk]
            b_mid, b_last = bl[mid:mid + 1], bl[chunk - 1:chunk]
            q_mid.append(qc * jnp.exp(bl - b_mid))
            k_mid.append(kc * jnp.exp(b_mid - bl))
            q_dec.append((qc * jnp.exp(bl)).astype(BF16))
            k_end.append((kc * jnp.exp(b_last - bl)).astype(BF16))
            decay.append(jnp.exp(b_last))
        a = _dot_nt(jnp.concatenate(q_mid, axis=0).astype(BF16), jnp.concatenate(k_mid, axis=0).astype(BF16))
        intra = _dot(jnp.where(band, a, 0.0).astype(BF16), v)
        update = [_dot_tn(v[c * chunk:(c + 1) * chunk], k_end[c]) for c in range(n_chunks)]
        st = s_scr[h]
        for c in range(n_chunks):
            r0 = c * chunk
            o = intra[r0:r0 + chunk] + _dot_nt(q_dec[c], st.astype(BF16))
            st = st * decay[c] + update[c]
            y = _rms(o, gain_ref[...]) * (g[r0:r0 + chunk] * jax.nn.sigmoid(g[r0:r0 + chunk]))
            o_ref[r0:r0 + chunk, sl] = y.astype(BF16)
        s_scr[h] = st
        return 0

    lax.fori_loop(0, H_RNN, head, 0, unroll=8)

    @pl.when(t == pl.num_programs(1) - 1)
    def _():
        for h in range(H_RNN):
            st_ref[0, h] = s_scr[h].T


def hgrn_prompt(z, lb, rnn_gain, B, S):
    rows = _tile(S, RNN_ROWS)
    chunk = _tile(rows, RNN_CHUNK)
    nt = S // rows
    tril = (jnp.arange(rows)[:, None] >= jnp.arange(rows)[None, :]).astype(BF16)
    col = lambda c: pl.BlockSpec((rows, D_HEADS), lambda b, t: (b * nt + t, c))
    return pl.pallas_call(
        functools.partial(_hgrn_prompt_kernel, rows=rows, chunk=chunk),
        grid=(B, nt),
        in_specs=[col(3), col(4), col(5), col(6),
                  pl.BlockSpec((1, D_HEADS), lambda b, t: (0, 0)),
                  pl.BlockSpec((1, HEAD_W), lambda b, t: (0, 0)),
                  pl.BlockSpec((rows, rows), lambda b, t: (0, 0))],
        out_specs=(pl.BlockSpec((rows, D_HEADS), lambda b, t: (b * nt + t, 0)),
                   pl.BlockSpec((1, H_RNN, HEAD_W, HEAD_W), lambda b, t: (b, 0, 0, 0))),
        out_shape=(jax.ShapeDtypeStruct((B * S, D_HEADS), BF16),
                   jax.ShapeDtypeStruct((B, H_RNN, HEAD_W, HEAD_W), F32)),
        scratch_shapes=[pltpu.VMEM((H_RNN, HEAD_W, HEAD_W), F32)],
        compiler_params=_cparams("parallel", "arbitrary"),
    )(z, z, z, z, lb, rnn_gain, tril)


def _hgrn_step_kernel(q_ref, f_ref, i_ref, g_ref, lb_ref, gain_ref, s0_ref, o_ref, s1_ref, *, bs):
    def column(rowvec):
        return jnp.broadcast_to(rowvec, (HEAD_W, HEAD_W)).T

    def head(h, _):
        sl = pl.ds(pl.multiple_of(h * HEAD_W, HEAD_W), HEAD_W)
        log_f, kk = _gates(f_ref[:, sl], lb_ref[:, sl])
        q, v, g = q_ref[:, sl], i_ref[:, sl], g_ref[:, sl]
        decay = jnp.exp(log_f)
        q_dec = q * decay
        a = jnp.sum(q_dec * (kk * jnp.exp(-log_f)), axis=-1, keepdims=True)
        inter = []
        for b in range(bs):
            s0 = s0_ref[b, h]
            q_rows = jnp.broadcast_to(q_dec[b:b + 1], (16, HEAD_W)).astype(BF16)
            inter.append(_dot(q_rows, s0.astype(BF16))[0:1])
            s1_ref[b, h] = column(decay[b:b + 1]) * s0 + column(kk[b:b + 1]) * v[b:b + 1]
        o = a * v + jnp.concatenate(inter, axis=0)
        o_ref[:, sl] = (_rms(o, gain_ref[...]) * (g * jax.nn.sigmoid(g))).astype(BF16)
        return 0

    lax.fori_loop(0, H_RNN, head, 0)


def hgrn_step(z, lb, rnn_gain, s0, bs_pref=16):
    DB = z.shape[0]
    bs = _tile(DB, bs_pref)
    col = lambda c: pl.BlockSpec((bs, D_HEADS), lambda i: (i, c))
    st = pl.BlockSpec((bs, H_RNN, HEAD_W, HEAD_W), lambda i: (i, 0, 0, 0))
    return pl.pallas_call(
        functools.partial(_hgrn_step_kernel, bs=bs),
        grid=(DB // bs,),
        in_specs=[col(3), col(4), col(5), col(6),
                  pl.BlockSpec((1, D_HEADS), lambda i: (0, 0)),
                  pl.BlockSpec((1, HEAD_W), lambda i: (0, 0)), st],
        out_specs=(pl.BlockSpec((bs, D_HEADS), lambda i: (i, 0)), st),
        out_shape=(jax.ShapeDtypeStruct((DB, D_HEADS), BF16), jax.ShapeDtypeStruct(s0.shape, F32)),
        compiler_params=_cparams("parallel"),
    )(z, z, z, z, lb, rnn_gain, s0)


def _out_proj_kernel(a_ref, r_ref, wa_ref, wr_ref, x_ref, g_ref, x1_ref, h1_ref):
    x1 = x_ref[...] + _dot(a_ref[...], wa_ref[...]) + _dot(r_ref[...], wr_ref[...])
    x1_ref[...] = x1
    h1_ref[...] = _rms(x1, g_ref[...]).astype(BF16)


def out_proj(o_att, o_rnn, w_out, x, ffn_gain, tm_pref=512):
    M, D = x.shape
    tm = _tile(M, tm_pref)
    act = pl.BlockSpec((tm, D_HEADS), lambda i: (i, 0))
    full = pl.BlockSpec((tm, D), lambda i: (i, 0))
    return pl.pallas_call(
        _out_proj_kernel,
        grid=(M // tm,),
        in_specs=[act, act,
                  pl.BlockSpec((D_HEADS, D), lambda i: (0, 0)),
                  pl.BlockSpec((D_HEADS, D), lambda i: (1, 0)),
                  full, pl.BlockSpec((1, D), lambda i: (0, 0))],
        out_specs=(full, full),
        out_shape=(jax.ShapeDtypeStruct((M, D), F32), jax.ShapeDtypeStruct((M, D), BF16)),
        compiler_params=_cparams("parallel"),
    )(o_att, o_rnn, w_out, w_out, x, ffn_gain)


def _conv_gate(g, u, first_tile, cw_ref, cb_ref, act_ref, tail_ref, carry_ref):
    @pl.when(first_tile)
    def _():
        carry_ref[...] = jnp.zeros_like(carry_ref)

    tm = g.shape[0]
    r = lax.broadcasted_iota(jnp.int32, g.shape, 0)
    prev2, prev1 = carry_ref[0:1], carry_ref[1:2]
    g1 = jnp.where(r == 0, prev1, pltpu.roll(g, 1, 0))
    g2 = jnp.where(r == 0, prev2, jnp.where(r == 1, prev1, pltpu.roll(g, 2, 0)))
    cw = cw_ref[...]
    c = cb_ref[...] + g2 * cw[0:1] + g1 * cw[1:2] + g * cw[2:3]
    act_ref[...] = (c * jax.nn.sigmoid(c) * u).astype(BF16)
    tail = g[tm - (CONV_W - 1):]
    carry_ref[0:CONV_W - 1] = tail
    tail_ref[0] = tail


def _ffn_seq_kernel(h_ref, wg_ref, wu_ref, cw_ref, cb_ref, act_ref, tail_ref, carry_ref, *, tiles_per_seq):
    h = h_ref[...]
    g = _dot(h, wg_ref[...])
    u = _dot(h, wu_ref[...])
    _conv_gate(g, u, pl.program_id(1) % tiles_per_seq == 0, cw_ref, cb_ref, act_ref, tail_ref, carry_ref)


def _ffn_decode_kernel(pt_ref, h_ref, wg_ref, wu_ref, cw_ref, cb_ref, lam_ref, q_ref, kn_ref, vn_ref, g_ref,
                       *refs, n_pages, tiles_m, tiles_per_seq, out_scale):
    del pt_ref
    k_refs, v_refs = refs[:n_pages], refs[n_pages:2 * n_pages]
    act_ref, tail_ref, o_ref, carry_ref = refs[2 * n_pages:]
    first_tile = (pl.program_id(0) % tiles_m) % tiles_per_seq == 0
    qm, own_head = _decode_query(q_ref)
    s = [_page_scores(qm, own_head, k_ref) for k_ref in k_refs]
    h = h_ref[...]
    g = _dot(h, wg_ref[...])
    u = _dot(h, wu_ref[...])
    p, p_new, l = _decode_weights(qm, s, kn_ref)
    values = [_page_values(pj, v_ref) for pj, v_ref in zip(p, v_refs)]
    o_ref[0] = _decode_finish(values, p_new, l, vn_ref, lam_ref, g_ref, out_scale)
    _conv_gate(g, u, first_tile, cw_ref, cb_ref, act_ref, tail_ref, carry_ref)


def ffn_seq_decode_tiles(M, S, F, DB, tm_pref=256, tn_pref=1408):
    tm, tn = _tile(S, tm_pref), _tile(F, tn_pref)
    return (tm, tn) if tn % HEAD_W == 0 and (M // tm) * (F // tn) == DB else None


def ffn_seq_decode(h, w_gate, w_up, conv_w, conv_b, B, S, tiles,
                   q, k_new, v_new, cache_k, cache_v, page_table, lam, att_gain, out_scale):
    M, D = h.shape
    F = w_gate.shape[1]
    DB, n_pages = page_table.shape
    tm, tn = tiles
    tiles_m, tiles_per_seq = M // tm, S // tm
    row = lambda i: i % tiles_m
    col = lambda i: i // tiles_m
    wspec = pl.BlockSpec((D, tn), lambda i, pt: (0, col(i)), pipeline_mode=pl.Buffered(1))
    tok = pl.BlockSpec((1, H_ATT, HEAD_W), lambda i, pt: (i, 0, 0))
    vec = pl.BlockSpec((1, HEAD_W), lambda i, pt: (0, 0))
    page = lambda j: pl.BlockSpec((1, PAGE_SIZE, H_ATT, HEAD_W),
                                  lambda i, pt: (pt[i * n_pages + j], 0, 0, 0))
    pages = [page(j) for j in range(n_pages)]
    return pl.pallas_call(
        functools.partial(_ffn_decode_kernel, n_pages=n_pages, tiles_m=tiles_m,
                          tiles_per_seq=tiles_per_seq, out_scale=out_scale),
        grid_spec=pltpu.PrefetchScalarGridSpec(
            num_scalar_prefetch=1,
            grid=(DB,),
            in_specs=[pl.BlockSpec((tm, D), lambda i, pt: (row(i), 0)), wspec, wspec,
                      pl.BlockSpec((CONV_W, tn), lambda i, pt: (0, col(i))),
                      pl.BlockSpec((1, tn), lambda i, pt: (0, col(i))),
                      vec, tok, tok, tok, vec] + pages + pages,
            out_specs=(pl.BlockSpec((tm, tn), lambda i, pt: (row(i), col(i))),
                       pl.BlockSpec((1, CONV_W - 1, tn), lambda i, pt: (row(i) // tiles_per_seq, 0, col(i))),
                       tok),
            scratch_shapes=[pltpu.VMEM((8, tn), F32)],
        ),
        out_shape=(jax.ShapeDtypeStruct((M, F), BF16), jax.ShapeDtypeStruct((B, CONV_W - 1, F), F32),
                   jax.ShapeDtypeStruct((DB, H_ATT, HEAD_W), F32)),
        compiler_params=_cparams("arbitrary"),
    )(page_table.reshape(-1), h, w_gate, w_up, conv_w, conv_b, lam, q, k_new, v_new, att_gain,
      *([cache_k] * n_pages), *([cache_v] * n_pages))


def ffn_seq(h, w_gate, w_up, conv_w, conv_b, B, S, tm_pref=512, tn_pref=1408):
    M, D = h.shape
    F = w_gate.shape[1]
    tm, tn = _tile(S, tm_pref), _tile(F, tn_pref)
    tiles_per_seq = S // tm
    wspec = pl.BlockSpec((D, tn), lambda n, m: (0, n))
    return pl.pallas_call(
        functools.partial(_ffn_seq_kernel, tiles_per_seq=tiles_per_seq),
        grid=(F // tn, M // tm),
        in_specs=[pl.BlockSpec((tm, D), lambda n, m: (m, 0)), wspec, wspec,
                  pl.BlockSpec((CONV_W, tn), lambda n, m: (0, n)),
                  pl.BlockSpec((1, tn), lambda n, m: (0, n))],
        out_specs=(pl.BlockSpec((tm, tn), lambda n, m: (m, n)),
                   pl.BlockSpec((1, CONV_W - 1, tn), lambda n, m: (m // tiles_per_seq, 0, n))),
        out_shape=(jax.ShapeDtypeStruct((M, F), BF16), jax.ShapeDtypeStruct((B, CONV_W - 1, F), F32)),
        scratch_shapes=[pltpu.VMEM((8, tn), F32)],
        compiler_params=_cparams("parallel", "arbitrary"),
    )(h, w_gate, w_up, conv_w, conv_b)


def _ffn_step_kernel(h_ref, wg_ref, wu_ref, cw_ref, cb_ref, p2_ref, p1_ref, act_ref, g_ref):
    h = h_ref[...]
    g = _dot(h, wg_ref[...])
    u = _dot(h, wu_ref[...])
    cw = cw_ref[...]
    c = cb_ref[...] + p2_ref[...] * cw[0:1] + p1_ref[...] * cw[1:2] + g * cw[2:3]
    act_ref[...] = (c * jax.nn.sigmoid(c) * u).astype(BF16)
    g_ref[...] = g


def ffn_step(h, w_gate, w_up, conv_w, conv_b, prev2, prev1, tn_pref=1408):
    M, D = h.shape
    F = w_gate.shape[1]
    tn = _tile(F, tn_pref)
    wspec = pl.BlockSpec((D, tn), lambda n: (0, n))
    tile = pl.BlockSpec((M, tn), lambda n: (0, n))
    return pl.pallas_call(
        _ffn_step_kernel,
        grid=(F // tn,),
        in_specs=[pl.BlockSpec((M, D), lambda n: (0, 0)), wspec, wspec,
                  pl.BlockSpec((CONV_W, tn), lambda n: (0, n)),
                  pl.BlockSpec((1, tn), lambda n: (0, n)), tile, tile],
        out_specs=(tile, tile),
        out_shape=(jax.ShapeDtypeStruct((M, F), BF16), jax.ShapeDtypeStruct((M, F), F32)),
        compiler_params=_cparams("parallel"),
    )(h, w_gate, w_up, conv_w, conv_b, prev2, prev1)


def _down_kernel(a_ref, w_ref, x_ref, o_ref):
    o_ref[...] = x_ref[...] + _dot(a_ref[...], w_ref[...])


def ffn_down(act, w_down, x, tm_pref=1024, tn_pref=512):
    M, F = act.shape
    D = w_down.shape[1]
    tm, tn = _tile(M, tm_pref), _tile(D, tn_pref)
    tile = pl.BlockSpec((tm, tn), lambda i, j: (i, j))
    return pl.pallas_call(
        _down_kernel,
        grid=(M // tm, D // tn),
        in_specs=[pl.BlockSpec((tm, F), lambda i, j: (i, 0)),
                  pl.BlockSpec((F, tn), lambda i, j: (0, j)), tile],
        out_specs=tile,
        out_shape=jax.ShapeDtypeStruct((M, D), F32),
        compiler_params=_cparams("parallel", "arbitrary"),
    )(act, w_down, x)


def _ple_kernel(x_ref, g_ref, wg_ref, p_ref, wp_ref, o_ref, h_ref, *, tn):
    j = pl.program_id(1)

    @pl.when(j == 0)
    def _():
        h_ref[...] = _rms(x_ref[...], g_ref[...]).astype(BF16)

    gate = jax.nn.sigmoid(_dot(h_ref[...], wg_ref[...]))
    emb = _dot(p_ref[...].astype(BF16), wp_ref[...])
    o_ref[...] = x_ref[:, pl.ds(pl.multiple_of(j * tn, tn), tn)] + gate * emb


def ple(x, gain, w_gate, p, w_proj, tm_pref=1024, tn_pref=1024):
    M, D = x.shape
    DP = p.shape[1]
    tm, tn = _tile(M, tm_pref), _tile(D, tn_pref)
    return pl.pallas_call(
        functools.partial(_ple_kernel, tn=tn),
        grid=(M // tm, D // tn),
        in_specs=[pl.BlockSpec((tm, D), lambda i, j: (i, 0)),
                  pl.BlockSpec((1, D), lambda i, j: (0, 0)),
                  pl.BlockSpec((D, tn), lambda i, j: (0, j)),
                  pl.BlockSpec((tm, DP), lambda i, j: (i, 0)),
                  pl.BlockSpec((DP, tn), lambda i, j: (0, j))],
        out_specs=pl.BlockSpec((tm, tn), lambda i, j: (i, j)),
        out_shape=jax.ShapeDtypeStruct((M, D), F32),
        scratch_shapes=[pltpu.VMEM((tm, D), BF16)],
        compiler_params=_cparams("parallel", "arbitrary"),
    )(x, gain, w_gate, p, w_proj)


def _rope_tables(pos):
    half = DK_ATT // 2
    inv_freq = 1.0 / (ROPE_THETA ** (jnp.arange(half, dtype=F32) * (2.0 / DK_ATT)))
    ang = pos.astype(F32)[:, None] * inv_freq[None, :]
    cos, sin = jnp.cos(ang), jnp.sin(ang)
    return (jnp.concatenate([cos] * 4, axis=-1), jnp.concatenate([-sin, sin] * 2, axis=-1))


def _row(v):
    return v.reshape(1, -1).astype(F32)


def kernel(x_prompt, x_sample, p_prompt, p_sample, cache_k, cache_v, state_rnn, state_ffn_conv, page_table,
           norm_mix, w_in, q_norm, k_norm, lam_q1, lam_k1, lam_q2, lam_k2, att_out_norm, rnn_out_norm,
           lower_bounds, w_out, norm_ffn, w_gate, w_up, conv_w, conv_b, w_down, norm_ple, w_ple_gate,
           w_ple_proj):
    B, S, D = x_prompt.shape
    DB, T, _ = x_sample.shape
    assert T == 1, "the sample group is implemented for one new token per sequence"
    depth = w_in.shape[0]
    past = page_table.shape[1] * PAGE_SIZE
    cos_p, sin_p = _rope_tables(jnp.arange(S, dtype=jnp.int32))
    cos_p, sin_p = jnp.tile(cos_p, (B, 1)), jnp.tile(sin_p, (B, 1))
    cos_s, sin_s = _rope_tables(jnp.full((DB,), past, dtype=jnp.int32))

    yp = x_prompt.reshape(B * S, D)
    ys = x_sample.reshape(DB, D)
    outs = [[] for _ in range(8)]
    for i in range(depth):
        lam_init = 0.8 - 0.6 * math.exp(-0.3 * i)
        out_scale = 1.0 - lam_init
        lam, lb = layer_params(jnp.stack([lam_q1[i], lam_k1[i], lam_q2[i], lam_k2[i]]).astype(F32),
                               lower_bounds.astype(F32), i, lam_init)
        w_in_b, w_out_b = w_in[i].astype(BF16), w_out[i].astype(BF16)
        w_gate_b, w_up_b, w_down_b = w_gate[i].astype(BF16), w_up[i].astype(BF16), w_down[i].astype(BF16)
        w_pg_b, w_pp_b = w_ple_gate[i].astype(BF16), w_ple_proj[i].astype(BF16)
        q_gain = jnp.tile(_row(q_norm[i]), (1, 2))
        k_gain = jnp.tile(_row(k_norm[i]), (1, 2))
        att_gain, rnn_gain = _row(att_out_norm[i]), _row(rnn_out_norm[i])

        def mix_in(x, cos, sin, seqs=None):
            z = norm_matmul(x, _row(norm_mix[i]), w_in_b)
            return (z,) + tuple(qk_prep(z, cos, sin, q_gain, k_gain, seqs))

        def mix_out(x, o_att, o_rnn):
            return out_proj(o_att, o_rnn, w_out_b, x, _row(norm_ffn[i]))

        def embed(x, act, p):
            x = ffn_down(act, w_down_b, x)
            return ple(x, _row(norm_ple[i]), w_pg_b, p, w_pp_b)

        z, qt, k, kb, v, vt = mix_in(yp, cos_p, sin_p, (B, S))
        o_att = attn_prompt(qt, kb.reshape(B, S, -1), vt, lam, att_gain, out_scale).reshape(B * S, -1)
        o_rnn, rnn_p = hgrn_prompt(z, lb, rnn_gain, B, S)
        x1, h1 = mix_out(yp, o_att, o_rnn)
        outs[0].append(k.reshape(B, S // PAGE_SIZE, PAGE_SIZE, H_ATT, HEAD_W))
        outs[1].append(v.reshape(B, S // PAGE_SIZE, PAGE_SIZE, H_ATT, HEAD_W))
        outs[2].append(rnn_p.astype(state_rnn.dtype))

        z, q, k, kb, v, vb = mix_in(ys, cos_s, sin_s)
        per_head = lambda a: a.reshape(DB, H_ATT, HEAD_W)
        decode_args = (per_head(q), per_head(k), per_head(v), cache_k[i], cache_v[i], page_table,
                       lam, att_gain, out_scale)
        ffn_args = (h1, w_gate_b, w_up_b, conv_w[i], _row(conv_b[i]), B, S)
        tiles = ffn_seq_decode_tiles(B * S, S, w_gate_b.shape[1], DB)
        if tiles is not None:
            act, conv_p, o_att = ffn_seq_decode(*ffn_args, tiles, *decode_args)
        else:
            act, conv_p = ffn_seq(*ffn_args)
            o_att = attn_sample(*decode_args)
        o_att = o_att.reshape(DB, D_HEADS).astype(BF16)
        yp = embed(x1, act, p_prompt[i].reshape(B * S, -1))
        outs[3].append(conv_p)
        o_rnn, rnn_s = hgrn_step(z, lb, rnn_gain, state_rnn[i].astype(F32))
        x1, h1 = mix_out(ys, o_att, o_rnn)
        buf = state_ffn_conv[i]
        act, g_new = ffn_step(h1, w_gate_b, w_up_b, conv_w[i], _row(conv_b[i]), buf[:, 0], buf[:, 1])
        ys = embed(x1, act, p_sample[i].reshape(DB, -1))
        outs[4].append(k.reshape(DB, T, H_ATT, HEAD_W))
        outs[5].append(v.reshape(DB, T, H_ATT, HEAD_W))
        outs[6].append(rnn_s.astype(state_rnn.dtype))
        outs[7].append(jnp.stack([buf[:, 1], g_new], axis=1))

    return (yp.reshape(B, S, D), ys.reshape(DB, T, D)) + tuple(jnp.stack(o) for o in outs)
```

```python
import functools
import math

import jax
import jax.numpy as jnp
from jax import lax
from jax.experimental import pallas as pl
from jax.experimental.pallas import tpu as pltpu

H_ATT = 8
DK_ATT = 64
H_RNN = 8
HEAD_W = 128
D_HEADS = H_ATT * HEAD_W
PAGE_SIZE = 128
CONV_W = 3
ROPE_THETA = 10000.0
EPS = 1e-6
NEG_INF = -1e30
RNN_CHUNK = 32
RNN_ROWS = 256
ATT_HEADS_PER_STEP = 4
VT_ROWS = HEAD_W + 16
Q_SCALE = DK_ATT ** -0.5 * math.log2(math.e)
VMEM_LIMIT_V7X = 56 * 1024 * 1024

F32 = jnp.float32
BF16 = jnp.bfloat16


def _cparams(*sem):
    return pltpu.CompilerParams(dimension_semantics=sem, vmem_limit_bytes=VMEM_LIMIT_V7X)


def _tile(n, pref):
    t = min(n, pref)
    while n % t:
        t //= 2
    return t


def _split3(x):
    hi = x.astype(BF16)
    r1 = x - hi.astype(F32)
    mid = r1.astype(BF16)
    lo = (r1 - mid.astype(F32)).astype(BF16)
    return hi, mid, lo


def _dot(a, b):
    return jnp.dot(a, b, preferred_element_type=F32)


def _dot_nt(a, b):
    return lax.dot_general(a, b, (((1,), (1,)), ((), ())), preferred_element_type=F32)


def _dot_tn(a, b):
    return lax.dot_general(a, b, (((0,), (0,)), ((), ())), preferred_element_type=F32)


def _rms(x, gain):
    return x * lax.rsqrt(jnp.mean(x * x, axis=-1, keepdims=True) + EPS) * gain


def _params_kernel(lam_ref, lb_in_ref, lam_out_ref, lb_out_ref, *, layer, lam_init):
    v = lam_ref[...]
    s1 = jnp.sum(v[0:1] * v[1:2], axis=-1, keepdims=True)
    s2 = jnp.sum(v[2:3] * v[3:4], axis=-1, keepdims=True)
    lam = jnp.exp(s1) - jnp.exp(s2) + lam_init
    lam_out_ref[...] = jnp.broadcast_to(lam, lam_out_ref.shape)
    lb = lb_in_ref[...]
    e = jnp.exp(lb - jnp.max(lb, axis=0, keepdims=True))
    sm = e / jnp.sum(e, axis=0, keepdims=True)
    lb_out_ref[...] = jnp.sum(sm[0:layer + 1], axis=0, keepdims=True)


def layer_params(lam_vecs, lower_bounds, layer, lam_init):
    return pl.pallas_call(
        functools.partial(_params_kernel, layer=layer, lam_init=lam_init),
        out_shape=(jax.ShapeDtypeStruct((1, HEAD_W), F32),
                   jax.ShapeDtypeStruct((1, lower_bounds.shape[1]), F32)),
    )(lam_vecs, lower_bounds)


def _norm_matmul_kernel(x_ref, g_ref, w_ref, o_ref, h_ref):
    @pl.when(pl.program_id(1) == 0)
    def _():
        h_ref[...] = _rms(x_ref[...], g_ref[...]).astype(BF16)

    o_ref[...] = _dot(h_ref[...], w_ref[...])


def norm_matmul(x, gain, w, tm_pref=1024, tn_pref=1024):
    M, K = x.shape
    N = w.shape[1]
    tm, tn = _tile(M, tm_pref), _tile(N, tn_pref)
    return pl.pallas_call(
        _norm_matmul_kernel,
        grid=(M // tm, N // tn),
        in_specs=[pl.BlockSpec((tm, K), lambda i, j: (i, 0)),
                  pl.BlockSpec((1, K), lambda i, j: (0, 0)),
                  pl.BlockSpec((K, tn), lambda i, j: (0, j))],
        out_specs=pl.BlockSpec((tm, tn), lambda i, j: (i, j)),
        out_shape=jax.ShapeDtypeStruct((M, N), F32),
        scratch_shapes=[pltpu.VMEM((tm, K), BF16)],
        compiler_params=_cparams("parallel", "arbitrary"),
    )(x, gain, w)


def _qk_prep_kernel(zq_ref, zk_ref, zv_ref, cos_ref, sin_ref, qg_ref, kg_ref, grp_ref,
                    q_ref, k_ref, kb_ref, v_ref, vb_ref, *, transposed):
    cos = cos_ref[...]
    sin = sin_ref[...]
    lane = lax.broadcasted_iota(jnp.int32, cos.shape, 1)
    first_half = (lane % DK_ATT) < (DK_ATT // 2)
    grp = grp_ref[...]

    def norm_rope(x, gain):
        hi, mid, lo = _split3(x * x)
        ss = (_dot(hi, grp) + _dot(mid, grp) + _dot(lo, grp)) * (1.0 / DK_ATT)
        y = x * lax.rsqrt(ss + EPS) * gain
        swapped = jnp.where(first_half, pltpu.roll(y, HEAD_W - DK_ATT // 2, 1),
                            pltpu.roll(y, DK_ATT // 2, 1))
        return y * cos + swapped * sin

    for h in range(H_ATT):
        sl = slice(h * HEAD_W, (h + 1) * HEAD_W)
        q = norm_rope(zq_ref[:, sl], qg_ref[...]) * Q_SCALE
        k = norm_rope(zk_ref[:, sl], kg_ref[...])
        k_ref[:, sl] = k
        kb_ref[:, sl] = k.astype(BF16)
        v = zv_ref[:, sl]
        v_ref[:, sl] = v
        if transposed:
            q_ref[0, sl, :] = q.T.astype(BF16)
            r0 = h * VT_ROWS
            vb_ref[0, r0:r0 + HEAD_W, :] = v.T.astype(BF16)
            pad = (VT_ROWS - HEAD_W, v.shape[0])
            vb_ref[0, r0 + HEAD_W:r0 + VT_ROWS, :] = (lax.broadcasted_iota(jnp.int32, pad, 0) == 0).astype(BF16)
        else:
            q_ref[:, sl] = q.astype(BF16)
            vb_ref[:, sl] = v.astype(BF16)


def qk_prep(z, cos, sin, q_gain, k_gain, seqs=None, tm_pref=512):
    M = z.shape[0]
    tm = _tile(M if seqs is None else seqs[1], tm_pref)
    lane = jnp.arange(HEAD_W)
    grp = (lane[:, None] // DK_ATT == lane[None, :] // DK_ATT).astype(BF16)
    col = lambda c: pl.BlockSpec((tm, D_HEADS), lambda i: (i, c))
    row = pl.BlockSpec((tm, HEAD_W), lambda i: (i, 0))
    vec = pl.BlockSpec((1, HEAD_W), lambda i: (0, 0))
    out = pl.BlockSpec((tm, D_HEADS), lambda i: (i, 0))
    sd = lambda dt: jax.ShapeDtypeStruct((M, D_HEADS), dt)
    if seqs is None:
        out_q, out_v, sd_q, sd_v = out, out, sd(BF16), sd(BF16)
    else:
        B, S = seqs
        tiles = S // tm
        out_q = pl.BlockSpec((1, D_HEADS, tm), lambda i: (i // tiles, 0, i % tiles))
        out_v = pl.BlockSpec((1, H_ATT * VT_ROWS, tm), lambda i: (i // tiles, 0, i % tiles))
        sd_q = jax.ShapeDtypeStruct((B, D_HEADS, S), BF16)
        sd_v = jax.ShapeDtypeStruct((B, H_ATT * VT_ROWS, S), BF16)
    return pl.pallas_call(
        functools.partial(_qk_prep_kernel, transposed=seqs is not None),
        grid=(M // tm,),
        in_specs=[col(0), col(1), col(2), row, row, vec, vec,
                  pl.BlockSpec((HEAD_W, HEAD_W), lambda i: (0, 0))],
        out_specs=(out_q, out, out, out, out_v),
        out_shape=(sd_q, sd(F32), sd(BF16), sd(F32), sd_v),
        compiler_params=_cparams("parallel"),
    )(z, z, z, cos, sin, q_gain, k_gain, grp)


def _attn_prompt_kernel(lam_ref, qt_ref, k_ref, vt_ref, g_ref, o_ref, m_scr, acc_scr, *, tq, tk, out_scale):
    qi = pl.program_id(2)
    n_full = (qi * tq) // tk
    dim = lax.broadcasted_iota(jnp.int32, (HEAD_W, tq), 0)
    q2t = []
    for u in range(ATT_HEADS_PER_STEP):
        qt = qt_ref[0, u * HEAD_W:(u + 1) * HEAD_W, :]
        zero = jnp.zeros_like(qt)
        q2t.append(jnp.concatenate([jnp.where(dim < DK_ATT, qt, zero),
                                    jnp.where(dim >= DK_ATT, qt, zero)], axis=1))
    m_scr[...] = jnp.full(m_scr.shape, NEG_INF, F32)
    acc_scr[...] = jnp.zeros(acc_scr.shape, F32)

    def block(kb, masked):
        start = pl.multiple_of(kb * tk, tk)

        def scores(u):
            return _dot(k_ref[0, pl.ds(start, tk), u * HEAD_W:(u + 1) * HEAD_W], q2t[u])

        s_next = scores(0)
        if masked:
            key = start + lax.broadcasted_iota(jnp.int32, s_next.shape, 0)
            qry = qi * tq + lax.broadcasted_iota(jnp.int32, s_next.shape, 1) % tq
            causal = key <= qry
        for u in range(ATT_HEADS_PER_STEP):
            s = s_next
            if u + 1 < ATT_HEADS_PER_STEP:
                s_next = scores(u + 1)
            if masked:
                s = jnp.where(causal, s, NEG_INF)
            m_prev = m_scr[u]
            m_new = jnp.maximum(m_prev, jnp.max(s, axis=0, keepdims=True))
            m_scr[u] = m_new
            p = jnp.exp2(s - m_new).astype(BF16)
            vt = vt_ref[0, u * VT_ROWS:(u + 1) * VT_ROWS, pl.ds(start, tk)]
            acc_scr[u] = jnp.exp2(m_prev - m_new) * acc_scr[u] + _dot(vt, p)

    def body(kb, carry):
        block(kb, False)
        return carry

    lax.fori_loop(0, n_full, body, 0)
    block(n_full, True)
    for u in range(ATT_HEADS_PER_STEP):
        acc = acc_scr[u]
        o = acc[:HEAD_W] / acc[HEAD_W:HEAD_W + 1]
        od = o[:, :tq] - lam_ref[:, :1] * o[:, tq:]
        y = od * lax.rsqrt(jnp.mean(od * od, axis=0, keepdims=True) + EPS) * g_ref[...] * out_scale
        o_ref[0, :, u * HEAD_W:(u + 1) * HEAD_W] = y.T.astype(BF16)


def attn_prompt(qt, k, vt, lam, att_gain, out_scale, tq_pref=256, tk_pref=512):
    B, S, _ = k.shape
    tq = _tile(S, tq_pref)
    tk = max(tq, _tile(S, tk_pref))
    U = ATT_HEADS_PER_STEP
    return pl.pallas_call(
        functools.partial(_attn_prompt_kernel, tq=tq, tk=tk, out_scale=out_scale),
        grid=(B, H_ATT // U, S // tq),
        in_specs=[pl.BlockSpec((1, HEAD_W), lambda b, h, i: (0, 0)),
                  pl.BlockSpec((1, U * HEAD_W, tq), lambda b, h, i: (b, h, i)),
                  pl.BlockSpec((1, S, U * HEAD_W), lambda b, h, i: (b, 0, h)),
                  pl.BlockSpec((1, U * VT_ROWS, S), lambda b, h, i: (b, h, 0)),
                  pl.BlockSpec((HEAD_W, 1), lambda b, h, i: (0, 0))],
        out_specs=pl.BlockSpec((1, tq, U * HEAD_W), lambda b, h, i: (b, i, h)),
        out_shape=jax.ShapeDtypeStruct(k.shape, BF16),
        scratch_shapes=[pltpu.VMEM((U, 1, 2 * tq), F32), pltpu.VMEM((U, VT_ROWS, 2 * tq), F32)],
        compiler_params=_cparams("parallel", "parallel", "arbitrary"),
    )(lam, qt, k, vt, att_gain.reshape(HEAD_W, 1))


PAGE_KEYS = PAGE_SIZE * H_ATT


def _decode_query(q_ref):
    rows = 2 * H_ATT
    q = q_ref[0]
    q16 = jnp.concatenate([q, q], axis=0)
    r = lax.broadcasted_iota(jnp.int32, (rows, HEAD_W), 0)
    c = lax.broadcasted_iota(jnp.int32, (rows, HEAD_W), 1)
    qm = jnp.where(c // DK_ATT == r // H_ATT, q16, jnp.zeros_like(q16))
    own_head = (lax.broadcasted_iota(jnp.int32, (rows, PAGE_KEYS), 1) % H_ATT
                == lax.broadcasted_iota(jnp.int32, (rows, PAGE_KEYS), 0) % H_ATT)
    return qm, own_head


def _page_scores(qm, own_head, k_ref):
    return jnp.where(own_head, _dot_nt(qm, k_ref[0].reshape(PAGE_KEYS, HEAD_W).astype(BF16)), NEG_INF)


def _decode_weights(qm, s, kn_ref):
    k_new = jnp.concatenate([kn_ref[0], kn_ref[0]], axis=0)
    s_new = jnp.sum(qm.astype(F32) * k_new, axis=-1, keepdims=True)
    m = jnp.maximum(jnp.max(functools.reduce(jnp.maximum, s), axis=-1, keepdims=True), s_new)
    p = [jnp.exp2(sj - m) for sj in s]
    p_new = jnp.exp2(s_new - m)
    l = jnp.sum(functools.reduce(jnp.add, p), axis=-1, keepdims=True) + p_new
    return p, p_new, l


def _page_values(p, v_ref):
    return _dot(p.astype(BF16), v_ref[0].reshape(PAGE_KEYS, HEAD_W).astype(BF16))


def _decode_finish(values, p_new, l, vn_ref, lam_ref, g_ref, out_scale):
    v_new = jnp.concatenate([vn_ref[0], vn_ref[0]], axis=0)
    o = (functools.reduce(jnp.add, values) + p_new * v_new) / l
    od = o[:H_ATT] - lam_ref[...] * o[H_ATT:]
    return _rms(od, g_ref[...]) * out_scale


def _attn_sample_kernel(pt_ref, lam_ref, q_ref, kn_ref, vn_ref, g_ref, *refs, n_pages, out_scale):
    del pt_ref
    k_refs, v_refs, o_ref = refs[:n_pages], refs[n_pages:2 * n_pages], refs[2 * n_pages]
    qm, own_head = _decode_query(q_ref)
    p, p_new, l = _decode_weights(qm, [_page_scores(qm, own_head, k_ref) for k_ref in k_refs], kn_ref)
    values = [_page_values(pj, v_ref) for pj, v_ref in zip(p, v_refs)]
    o_ref[0] = _decode_finish(values, p_new, l, vn_ref, lam_ref, g_ref, out_scale)


def attn_sample(q, k_new, v_new, cache_k, cache_v, page_table, lam, att_gain, out_scale):
    DB = q.shape[0]
    n_pages = page_table.shape[1]
    tok =pl.BlockSpec((1, H_ATT, HEAD_W), lambda b, pt: (b, 0, 0))
    vec = pl.BlockSpec((1, HEAD_W), lambda b, pt: (0, 0))
    page = lambda j: pl.BlockSpec((1, PAGE_SIZE, H_ATT, HEAD_W),
                                  lambda b, pt: (pt[b * n_pages + j], 0, 0, 0))
    pages = [page(j) for j in range(n_pages)]
    return pl.pallas_call(
        functools.partial(_attn_sample_kernel, n_pages=n_pages, out_scale=out_scale),
        grid_spec=pltpu.PrefetchScalarGridSpec(
            num_scalar_prefetch=1,
            grid=(DB,),
            in_specs=[vec, tok, tok, tok, vec] + pages + pages,
            out_specs=tok,
        ),
        out_shape=jax.ShapeDtypeStruct((DB, H_ATT, HEAD_W), F32),
        compiler_params=_cparams("arbitrary"),
    )(page_table.reshape(-1), lam, q, k_new, v_new, att_gain,
      *([cache_k] * n_pages), *([cache_v] * n_pages))


def _gates(zf, lb):
    sig = jax.nn.sigmoid(zf)
    f = lb + (1.0 - lb) * sig
    return jnp.log(f), (1.0 - lb) * (1.0 - sig)


def _hgrn_prompt_kernel(q_ref, f_ref, i_ref, g_ref, lb_ref, gain_ref, tril_ref, o_ref, st_ref, s_scr,
                        *, rows, chunk):
    t = pl.program_id(1)

    @pl.when(t == 0)
    def _():
        s_scr[...] = jnp.zeros_like(s_scr)

    tril = tril_ref[...]
    r = lax.broadcasted_iota(jnp.int32, (rows, rows), 0)
    c = lax.broadcasted_iota(jnp.int32, (rows, rows), 1)
    band = (c <= r) & (c // chunk == r // chunk)
    mid = chunk // 2 - 1
    n_chunks = rows // chunk

    heads = [slice(h * HEAD_W, (h + 1) * HEAD_W) for h in range(H_RNN)]
    chunks = [slice(c * chunk, (c + 1) * chunk) for c in range(n_chunks)]
    gates = [_gates(f_ref[:, sl], lb_ref[:, sl]) for sl in heads]
    cum = []
    for log_f, _ in gates:
        hi, md, lo = _split3(log_f)
        cum.append(_dot(tril, hi) + _dot(tril, md) + _dot(tril, lo))
    v = [i_ref[:, sl].astype(BF16) for sl in heads]
    q_mid, k_mid, q_dec, k_end, decay = ([[] for _ in heads] for _ in range(5))
    for h, sl in enumerate(heads):
        q, kk, bc = q_ref[:, sl], gates[h][1], cum[h]
        for c, rc in enumerate(chunks):
            bl = bc[rc]
            if c:
                bl = bl - bc[rc.start - 1:rc.start]
            b_mid, b_last = bl[mid:mid + 1], bl[chunk - 1:chunk]
            q_mid[h].append(q[rc] * jnp.exp(bl - b_mid))
            k_mid[h].append(kk[rc] * jnp.exp(b_mid - bl))
            q_dec[h].append((q[rc] * jnp.exp(bl)).astype(BF16))
            k_end[h].append((kk[rc] * jnp.exp(b_last - bl)).astype(BF16))
            decay[h].append(jnp.exp(b_last))
    a = [_dot_nt(jnp.concatenate(q_mid[h], axis=0).astype(BF16), jnp.concatenate(k_mid[h], axis=0).astype(BF16))
         for h in range(H_RNN)]
    intra = [_dot(jnp.where(band, a[h], 0.0).astype(BF16), v[h]) for h in range(H_RNN)]
    update = [[_dot_tn(v[h][rc], k_end[h][c]) for c, rc in enumerate(chunks)] for h in range(H_RNN)]
    st = [s_scr[h] for h in range(H_RNN)]
    for c, rc in enumerate(chunks):
        for h, sl in enumerate(heads):
            o = intra[h][rc] + _dot_nt(q_dec[h][c], st[h].astype(BF16))
            st[h] = st[h] * decay[h][c] + update[h][c]
            g = g_ref[rc, sl]
            o_ref[rc, sl] = (_rms(o, gain_ref[...]) * (g * jax.nn.sigmoid(g))).astype(BF16)
    for h in range(H_RNN):
        s_scr[h] = st[h]

    @pl.when(t == pl.num_programs(1) - 1)
    def _():
        for h in range(H_RNN):
            st_ref[0, h] = s_scr[h].T


def hgrn_prompt(z, lb, rnn_gain, B, S):
    rows = _tile(S, RNN_ROWS)
    chunk = _tile(rows, RNN_CHUNK)
    nt = S // rows
    tril = (jnp.arange(rows)[:, None] >= jnp.arange(rows)[None, :]).astype(BF16)
    col = lambda c: pl.BlockSpec((rows, D_HEADS), lambda b, t: (b * nt + t, c))
    return pl.pallas_call(
        functools.partial(_hgrn_prompt_kernel, rows=rows, chunk=chunk),
        grid=(B, nt),
        in_specs=[col(3), col(4), col(5), col(6),
                  pl.BlockSpec((1, D_HEADS), lambda b, t: (0, 0)),
                  pl.BlockSpec((1, HEAD_W), lambda b, t: (0, 0)),
                  pl.BlockSpec((rows, rows), lambda b, t: (0, 0))],
        out_specs=(pl.BlockSpec((rows, D_HEADS), lambda b, t: (b * nt + t, 0)),
                   pl.BlockSpec((1, H_RNN, HEAD_W, HEAD_W), lambda b, t: (b, 0, 0, 0))),
        out_shape=(jax.ShapeDtypeStruct((B * S, D_HEADS), BF16),
                   jax.ShapeDtypeStruct((B, H_RNN, HEAD_W, HEAD_W), F32)),
        scratch_shapes=[pltpu.VMEM((H_RNN, HEAD_W, HEAD_W), F32)],
        compiler_params=_cparams("parallel", "arbitrary"),
    )(z, z, z, z, lb, rnn_gain, tril)


def _hgrn_step_kernel(q_ref, f_ref, i_ref, g_ref, lb_ref, gain_ref, s0_ref, o_ref, s1_ref, *, bs):
    def column(rowvec):
        return jnp.broadcast_to(rowvec, (HEAD_W, HEAD_W)).T

    def head(h, _):
        sl = pl.ds(pl.multiple_of(h * HEAD_W, HEAD_W), HEAD_W)
        log_f, kk = _gates(f_ref[:, sl], lb_ref[:, sl])
        q, v, g = q_ref[:, sl], i_ref[:, sl], g_ref[:, sl]
        decay = jnp.exp(log_f)
        q_dec = q * decay
        a = jnp.sum(q_dec * (kk * jnp.exp(-log_f)), axis=-1, keepdims=True)
        inter = []
        for b in range(bs):
            s0 = s0_ref[b, h]
            q_rows = jnp.broadcast_to(q_dec[b:b + 1], (16, HEAD_W)).astype(BF16)
            inter.append(_dot(q_rows, s0.astype(BF16))[0:1])
            s1_ref[b, h] = column(decay[b:b + 1]) * s0 + column(kk[b:b + 1]) * v[b:b + 1]
        o = a * v + jnp.concatenate(inter, axis=0)
        o_ref[:, sl] = (_rms(o, gain_ref[...]) * (g * jax.nn.sigmoid(g))).astype(BF16)
        return 0

    lax.fori_loop(0, H_RNN, head, 0)


def hgrn_step(z, lb, rnn_gain, s0, bs_pref=16):
    DB = z.shape[0]
    bs = _tile(DB, bs_pref)
    col = lambda c: pl.BlockSpec((bs, D_HEADS), lambda i: (i, c))
    st = pl.BlockSpec((bs, H_RNN, HEAD_W, HEAD_W), lambda i: (i, 0, 0, 0))
    return pl.pallas_call(
        functools.partial(_hgrn_step_kernel, bs=bs),
        grid=(DB // bs,),
        in_specs=[col(3), col(4), col(5), col(6),
                  pl.BlockSpec((1, D_HEADS), lambda i: (0, 0)),
                  pl.BlockSpec((1, HEAD_W), lambda i: (0, 0)), st],
        out_specs=(pl.BlockSpec((bs, D_HEADS), lambda i: (i, 0)), st),
        out_shape=(jax.ShapeDtypeStruct((DB, D_HEADS), BF16), jax.ShapeDtypeStruct(s0.shape, F32)),
        compiler_params=_cparams("parallel"),
    )(z, z, z, z, lb, rnn_gain, s0)


def _out_proj_kernel(a_ref, r_ref, wa_ref, wr_ref, x_ref, g_ref, x1_ref, h1_ref):
    x1 = x_ref[...] + _dot(a_ref[...], wa_ref[...]) + _dot(r_ref[...], wr_ref[...])
    x1_ref[...] = x1
    h1_ref[...] = _rms(x1, g_ref[...]).astype(BF16)


def out_proj(o_att, o_rnn, w_out, x, ffn_gain, tm_pref=512):
    M, D = x.shape
    tm = _tile(M, tm_pref)
    act = pl.BlockSpec((tm, D_HEADS), lambda i: (i, 0))
    full = pl.BlockSpec((tm, D), lambda i: (i, 0))
    return pl.pallas_call(
        _out_proj_kernel,
        grid=(M // tm,),
        in_specs=[act, act,
                  pl.BlockSpec((D_HEADS, D), lambda i: (0, 0)),
                  pl.BlockSpec((D_HEADS, D), lambda i: (1, 0)),
                  full, pl.BlockSpec((1, D), lambda i: (0, 0))],
        out_specs=(full, full),
        out_shape=(jax.ShapeDtypeStruct((M, D), F32), jax.ShapeDtypeStruct((M, D), BF16)),
        compiler_params=_cparams("parallel"),
    )(o_att, o_rnn, w_out, w_out, x, ffn_gain)


def _conv_gate(g, u, first_tile, cw_ref, cb_ref, act_ref, tail_ref, carry_ref):
    @pl.when(first_tile)
    def _():
        carry_ref[...] = jnp.zeros_like(carry_ref)

    tm = g.shape[0]
    r = lax.broadcasted_iota(jnp.int32, g.shape, 0)
    prev2, prev1 = carry_ref[0:1], carry_ref[1:2]
    g1 = jnp.where(r == 0, prev1, pltpu.roll(g, 1, 0))
    g2 = jnp.where(r == 0, prev2, jnp.where(r == 1, prev1, pltpu.roll(g, 2, 0)))
    cw = cw_ref[...]
    c = cb_ref[...] + g2 * cw[0:1] + g1 * cw[1:2] + g * cw[2:3]
    act_ref[...] = (c * jax.nn.sigmoid(c) * u).astype(BF16)
    tail = g[tm - (CONV_W - 1):]
    carry_ref[0:CONV_W - 1] = tail
    tail_ref[0] = tail


def _ffn_seq_kernel(h_ref, wg_ref, wu_ref, cw_ref, cb_ref, act_ref, tail_ref, carry_ref, *, tiles_per_seq):
    h = h_ref[...]
    g = _dot(h, wg_ref[...])
    u = _dot(h, wu_ref[...])
    _conv_gate(g, u, pl.program_id(1) % tiles_per_seq == 0, cw_ref, cb_ref, act_ref, tail_ref, carry_ref)


def _ffn_decode_kernel(pt_ref, h_ref, wg_ref, wu_ref, cw_ref, cb_ref, lam_ref, q_ref, kn_ref, vn_ref, g_ref,
                       *refs, n_pages, tiles_m, tiles_per_seq, out_scale):
    del pt_ref
    k_refs, v_refs = refs[:n_pages], refs[n_pages:2 * n_pages]
    act_ref, tail_ref, o_ref, carry_ref = refs[2 * n_pages:]
    first_tile = (pl.program_id(0) % tiles_m) % tiles_per_seq == 0
    qm, own_head = _decode_query(q_ref)
    s = [_page_scores(qm, own_head, k_ref) for k_ref in k_refs]
    h = h_ref[...]
    g = _dot(h, wg_ref[...])
    u = _dot(h, wu_ref[...])
    p, p_new, l = _decode_weights(qm, s, kn_ref)
    values = [_page_values(pj, v_ref) for pj, v_ref in zip(p, v_refs)]
    o_ref[0] = _decode_finish(values, p_new, l, vn_ref, lam_ref, g_ref, out_scale)
    _conv_gate(g, u, first_tile, cw_ref, cb_ref, act_ref, tail_ref, carry_ref)


def ffn_seq_decode_tiles(M, S, F, DB, tm_pref=256, tn_pref=1408):
    tm, tn = _tile(S, tm_pref), _tile(F, tn_pref)
    return (tm, tn) if tn % HEAD_W == 0 and (M // tm) * (F // tn) == DB else None


def ffn_seq_decode(h, w_gate, w_up, conv_w, conv_b, B, S, tiles,
                   q, k_new, v_new, cache_k, cache_v, page_table, lam, att_gain, out_scale):
    M, D = h.shape
    F = w_gate.shape[1]
    DB, n_pages = page_table.shape
    tm, tn = tiles
    tiles_m, tiles_per_seq = M // tm, S // tm
    row = lambda i: i % tiles_m
    col = lambda i: i // tiles_m
    wspec = pl.BlockSpec((D, tn), lambda i, pt: (0, col(i)), pipeline_mode=pl.Buffered(1))
    tok = pl.BlockSpec((1, H_ATT, HEAD_W), lambda i, pt: (i, 0, 0))
    vec = pl.BlockSpec((1, HEAD_W), lambda i, pt: (0, 0))
    page = lambda j: pl.BlockSpec((1, PAGE_SIZE, H_ATT, HEAD_W),
                                  lambda i, pt: (pt[i * n_pages + j], 0, 0, 0))
    pages = [page(j) for j in range(n_pages)]
    return pl.pallas_call(
        functools.partial(_ffn_decode_kernel, n_pages=n_pages, tiles_m=tiles_m,
                          tiles_per_seq=tiles_per_seq, out_scale=out_scale),
        grid_spec=pltpu.PrefetchScalarGridSpec(
            num_scalar_prefetch=1,
            grid=(DB,),
            in_specs=[pl.BlockSpec((tm, D), lambda i, pt: (row(i), 0)), wspec, wspec,
                      pl.BlockSpec((CONV_W, tn), lambda i, pt: (0, col(i))),
                      pl.BlockSpec((1, tn), lambda i, pt: (0, col(i))),
                      vec, tok, tok, tok, vec] + pages + pages,
            out_specs=(pl.BlockSpec((tm, tn), lambda i, pt: (row(i), col(i))),
                       pl.BlockSpec((1, CONV_W - 1, tn), lambda i, pt: (row(i) // tiles_per_seq, 0, col(i))),
                       tok),
            scratch_shapes=[pltpu.VMEM((8, tn), F32)],
        ),
        out_shape=(jax.ShapeDtypeStruct((M, F), BF16), jax.ShapeDtypeStruct((B, CONV_W - 1, F), F32),
                   jax.ShapeDtypeStruct((DB, H_ATT, HEAD_W), F32)),
        compiler_params=_cparams("arbitrary"),
    )(page_table.reshape(-1), h, w_gate, w_up, conv_w, conv_b, lam, q, k_new, v_new, att_gain,
      *([cache_k] * n_pages), *([cache_v] * n_pages))


def ffn_seq(h, w_gate, w_up, conv_w, conv_b, B, S, tm_pref=512, tn_pref=1408):
    M, D = h.shape
    F = w_gate.shape[1]
    tm, tn = _tile(S, tm_pref), _tile(F, tn_pref)
    tiles_per_seq = S // tm
    wspec = pl.BlockSpec((D, tn), lambda n, m: (0, n))
    return pl.pallas_call(
        functools.partial(_ffn_seq_kernel, tiles_per_seq=tiles_per_seq),
        grid=(F // tn, M // tm),
        in_specs=[pl.BlockSpec((tm, D), lambda n, m: (m, 0)), wspec, wspec,
                  pl.BlockSpec((CONV_W, tn), lambda n, m: (0, n)),
                  pl.BlockSpec((1, tn), lambda n, m: (0, n))],
        out_specs=(pl.BlockSpec((tm, tn), lambda n, m: (m, n)),
                   pl.BlockSpec((1, CONV_W - 1, tn), lambda n, m: (m // tiles_per_seq, 0, n))),
        out_shape=(jax.ShapeDtypeStruct((M, F), BF16), jax.ShapeDtypeStruct((B, CONV_W - 1, F), F32)),
        scratch_shapes=[pltpu.VMEM((8, tn), F32)],
        compiler_params=_cparams("parallel", "arbitrary"),
    )(h, w_gate, w_up, conv_w, conv_b)


def _ffn_step_kernel(h_ref, wg_ref, wu_ref, cw_ref, cb_ref, p2_ref, p1_ref, act_ref, g_ref):
    h = h_ref[...]
    g = _dot(h, wg_ref[...])
    u = _dot(h, wu_ref[...])
    cw = cw_ref[...]
    c = cb_ref[...] + p2_ref[...] * cw[0:1] + p1_ref[...] * cw[1:2] + g * cw[2:3]
    act_ref[...] = (c * jax.nn.sigmoid(c) * u).astype(BF16)
    g_ref[...] = g


def ffn_step(h, w_gate, w_up, conv_w, conv_b, prev2, prev1, tn_pref=1408):
    M, D = h.shape
    F = w_gate.shape[1]
    tn = _tile(F, tn_pref)
    wspec = pl.BlockSpec((D, tn), lambda n: (0, n))
    tile = pl.BlockSpec((M, tn), lambda n: (0, n))
    return pl.pallas_call(
        _ffn_step_kernel,
        grid=(F // tn,),
        in_specs=[pl.BlockSpec((M, D), lambda n: (0, 0)), wspec, wspec,
                  pl.BlockSpec((CONV_W, tn), lambda n: (0, n)),
                  pl.BlockSpec((1, tn), lambda n: (0, n)), tile, tile],
        out_specs=(tile, tile),
        out_shape=(jax.ShapeDtypeStruct((M, F), BF16), jax.ShapeDtypeStruct((M, F), F32)),
        compiler_params=_cparams("parallel"),
    )(h, w_gate, w_up, conv_w, conv_b, prev2, prev1)


def _down_kernel(a_ref, w_ref, x_ref, o_ref):
    o_ref[...] = x_ref[...] + _dot(a_ref[...], w_ref[...])


def ffn_down(act, w_down, x, tm_pref=1024, tn_pref=512):
    M, F = act.shape
    D = w_down.shape[1]
    tm, tn = _tile(M, tm_pref), _tile(D, tn_pref)
    tile = pl.BlockSpec((tm, tn), lambda i, j: (i, j))
    return pl.pallas_call(
        _down_kernel,
        grid=(M // tm, D // tn),
        in_specs=[pl.BlockSpec((tm, F), lambda i, j: (i, 0)),
                  pl.BlockSpec((F, tn), lambda i, j: (0, j)), tile],
        out_specs=tile,
        out_shape=jax.ShapeDtypeStruct((M, D), F32),
        compiler_params=_cparams("parallel", "arbitrary"),
    )(act, w_down, x)


def _ple_kernel(x_ref, g_ref, wg_ref, p_ref, wp_ref, o_ref, h_ref, *, tn):
    j = pl.program_id(1)

    @pl.when(j == 0)
    def _():
        h_ref[...] = _rms(x_ref[...], g_ref[...]).astype(BF16)

    gate = jax.nn.sigmoid(_dot(h_ref[...], wg_ref[...]))
    emb = _dot(p_ref[...].astype(BF16), wp_ref[...])
    o_ref[...] = x_ref[:, pl.ds(pl.multiple_of(j * tn, tn), tn)] + gate * emb


def ple(x, gain, w_gate, p, w_proj, tm_pref=512, tn_pref=2048):
    M, D = x.shape
    DP = p.shape[1]
    tm, tn = _tile(M, tm_pref), _tile(D, tn_pref)
    return pl.pallas_call(
        functools.partial(_ple_kernel, tn=tn),
        grid=(M // tm, D // tn),
        in_specs=[pl.BlockSpec((tm, D), lambda i, j: (i, 0)),
                  pl.BlockSpec((1, D), lambda i, j: (0, 0)),
                  pl.BlockSpec((D, tn), lambda i, j: (0, j)),
                  pl.BlockSpec((tm, DP), lambda i, j: (i, 0)),
                  pl.BlockSpec((DP, tn), lambda i, j: (0, j))],
        out_specs=pl.BlockSpec((tm, tn), lambda i, j: (i, j)),
        out_shape=jax.ShapeDtypeStruct((M, D), F32),
        scratch_shapes=[pltpu.VMEM((tm, D), BF16)],
        compiler_params=_cparams("parallel", "arbitrary"),
    )(x, gain, w_gate, p, w_proj)


def _rope_tables(pos):
    half = DK_ATT // 2
    inv_freq = 1.0 / (ROPE_THETA ** (jnp.arange(half, dtype=F32) * (2.0 / DK_ATT)))
    ang = pos.astype(F32)[:, None] * inv_freq[None, :]
    cos, sin = jnp.cos(ang), jnp.sin(ang)
    return (jnp.concatenate([cos] * 4, axis=-1), jnp.concatenate([-sin, sin] * 2, axis=-1))


def _row(v):
    return v.reshape(1, -1).astype(F32)


def kernel(x_prompt, x_sample, p_prompt, p_sample, cache_k, cache_v, state_rnn, state_ffn_conv, page_table,
           norm_mix, w_in, q_norm, k_norm, lam_q1, lam_k1, lam_q2, lam_k2, att_out_norm, rnn_out_norm,
           lower_bounds, w_out, norm_ffn, w_gate, w_up, conv_w, conv_b, w_down, norm_ple, w_ple_gate,
           w_ple_proj):
    B, S, D = x_prompt.shape
    DB, T, _ = x_sample.shape
    assert T == 1, "the sample group is implemented for one new token per sequence"
    depth = w_in.shape[0]
    past = page_table.shape[1] * PAGE_SIZE
    cos_p, sin_p = _rope_tables(jnp.arange(S, dtype=jnp.int32))
    cos_p, sin_p = jnp.tile(cos_p, (B, 1)), jnp.tile(sin_p, (B, 1))
    cos_s, sin_s = _rope_tables(jnp.full((DB,), past, dtype=jnp.int32))

    yp = x_prompt.reshape(B * S, D)
    ys = x_sample.reshape(DB, D)
    outs = [[] for _ in range(8)]
    for i in range(depth):
        lam_init = 0.8 - 0.6 * math.exp(-0.3 * i)
        out_scale = 1.0 - lam_init
        lam, lb = layer_params(jnp.stack([lam_q1[i], lam_k1[i], lam_q2[i], lam_k2[i]]).astype(F32),
                               lower_bounds.astype(F32), i, lam_init)
        w_in_b, w_out_b = w_in[i].astype(BF16), w_out[i].astype(BF16)
        w_gate_b, w_up_b, w_down_b = w_gate[i].astype(BF16), w_up[i].astype(BF16), w_down[i].astype(BF16)
        w_pg_b, w_pp_b = w_ple_gate[i].astype(BF16), w_ple_proj[i].astype(BF16)
        q_gain = jnp.tile(_row(q_norm[i]), (1, 2))
        k_gain = jnp.tile(_row(k_norm[i]), (1, 2))
        att_gain, rnn_gain = _row(att_out_norm[i]), _row(rnn_out_norm[i])

        def mix_in(x, cos, sin, seqs=None):
            z = norm_matmul(x, _row(norm_mix[i]), w_in_b)
            return (z,) + tuple(qk_prep(z, cos, sin, q_gain, k_gain, seqs))

        def mix_out(x, o_att, o_rnn):
            return out_proj(o_att, o_rnn, w_out_b, x, _row(norm_ffn[i]))

        def embed(x, act, p):
            x = ffn_down(act, w_down_b, x)
            return ple(x, _row(norm_ple[i]), w_pg_b, p, w_pp_b)

        z, qt, k, kb, v, vt = mix_in(yp, cos_p, sin_p, (B, S))
        o_att = attn_prompt(qt, kb.reshape(B, S, -1), vt, lam, att_gain, out_scale).reshape(B * S, -1)
        o_rnn, rnn_p = hgrn_prompt(z, lb, rnn_gain, B, S)
        x1, h1 = mix_out(yp, o_att, o_rnn)
        outs[0].append(k.reshape(B, S // PAGE_SIZE, PAGE_SIZE, H_ATT, HEAD_W))
        outs[1].append(v.reshape(B, S // PAGE_SIZE, PAGE_SIZE, H_ATT, HEAD_W))
        outs[2].append(rnn_p.astype(state_rnn.dtype))

        z, q, k, kb, v, vb = mix_in(ys, cos_s, sin_s)
        per_head = lambda a: a.reshape(DB, H_ATT, HEAD_W)
        decode_args = (per_head(q), per_head(k), per_head(v), cache_k[i], cache_v[i], page_table,
                       lam, att_gain, out_scale)
        ffn_args = (h1, w_gate_b, w_up_b, conv_w[i], _row(conv_b[i]), B, S)
        tiles = ffn_seq_decode_tiles(B * S, S, w_gate_b.shape[1], DB)
        if tiles is not None:
            act, conv_p, o_att = ffn_seq_decode(*ffn_args, tiles, *decode_args)
        else:
            act, conv_p = ffn_seq(*ffn_args)
            o_att = attn_sample(*decode_args)
        o_att = o_att.reshape(DB, D_HEADS).astype(BF16)
        yp = embed(x1, act, p_prompt[i].reshape(B * S, -1))
        outs[3].append(conv_p)
        o_rnn, rnn_s = hgrn_step(z, lb, rnn_gain, state_rnn[i].astype(F32))
        x1, h1 = mix_out(ys, o_att, o_rnn)
        buf = state_ffn_conv[i]
        act, g_new = ffn_step(h1, w_gate_b, w_up_b, conv_w[i], _row(conv_b[i]), buf[:, 0], buf[:, 1])
        ys = embed(x1, act, p_sample[i].reshape(DB, -1))
        outs[4].append(k.reshape(DB, T, H_ATT, HEAD_W))
        outs[5].append(v.reshape(DB, T, H_ATT, HEAD_W))
        outs[6].append(rnn_s.astype(state_rnn.dtype))
        outs[7].append(jnp.stack([buf[:, 1], g_new], axis=1))

    return (yp.reshape(B, S, D), ys.reshape(DB, T, D)) + tuple(jnp.stack(o) for o in outs)
```

```python
import functools
import math

import jax
import jax.numpy as jnp
from jax import lax
from jax.experimental import pallas as pl
from jax.experimental.pallas import tpu as pltpu

H_ATT = 8
DK_ATT = 64
H_RNN = 8
HEAD_W = 128
D_HEADS = H_ATT * HEAD_W
PAGE_SIZE = 128
CONV_W = 3
ROPE_THETA = 10000.0
EPS = 1e-6
NEG_INF = -1e30
RNN_CHUNK = 32
RNN_ROWS = 256
ATT_HEADS_PER_STEP = 4
VT_ROWS = HEAD_W + 16
Q_SCALE = DK_ATT ** -0.5 * math.log2(math.e)
VMEM_LIMIT_V7X = 56 * 1024 * 1024

F32 = jnp.float32
BF16 = jnp.bfloat16


def _cparams(*sem):
    return pltpu.CompilerParams(dimension_semantics=sem, vmem_limit_bytes=VMEM_LIMIT_V7X)


def _tile(n, pref):
    t = min(n, pref)
    while n % t:
        t //= 2
    return t


def _split3(x):
    hi = x.astype(BF16)
    r1 = x - hi.astype(F32)
    mid = r1.astype(BF16)
    lo = (r1 - mid.astype(F32)).astype(BF16)
    return hi, mid, lo


def _dot(a, b):
    return jnp.dot(a, b, preferred_element_type=F32)


def _dot_nt(a, b):
    return lax.dot_general(a, b, (((1,), (1,)), ((), ())), preferred_element_type=F32)


def _dot_tn(a, b):
    return lax.dot_general(a, b, (((0,), (0,)), ((), ())), preferred_element_type=F32)


def _rms(x, gain):
    return x * lax.rsqrt(jnp.mean(x * x, axis=-1, keepdims=True) + EPS) * gain


def _params_kernel(lam_ref, lb_in_ref, lam_out_ref, lb_out_ref, *, layer, lam_init):
    v = lam_ref[...]
    s1 = jnp.sum(v[0:1] * v[1:2], axis=-1, keepdims=True)
    s2 = jnp.sum(v[2:3] * v[3:4], axis=-1, keepdims=True)
    lam = jnp.exp(s1) - jnp.exp(s2) + lam_init
    lam_out_ref[...] = jnp.broadcast_to(lam, lam_out_ref.shape)
    lb = lb_in_ref[...]
    e = jnp.exp(lb - jnp.max(lb, axis=0, keepdims=True))
    sm = e / jnp.sum(e, axis=0, keepdims=True)
    lb_out_ref[...] = jnp.sum(sm[0:layer + 1], axis=0, keepdims=True)


def layer_params(lam_vecs, lower_bounds, layer, lam_init):
    return pl.pallas_call(
        functools.partial(_params_kernel, layer=layer, lam_init=lam_init),
        out_shape=(jax.ShapeDtypeStruct((1, HEAD_W), F32),
                   jax.ShapeDtypeStruct((1, lower_bounds.shape[1]), F32)),
    )(lam_vecs, lower_bounds)


def _norm_matmul_kernel(x_ref, g_ref, w_ref, o_ref, h_ref):
    @pl.when(pl.program_id(1) == 0)
    def _():
        h_ref[...] = _rms(x_ref[...], g_ref[...]).astype(BF16)

    o_ref[...] = _dot(h_ref[...], w_ref[...])


def norm_matmul(x, gain, w, tm_pref=1024, tn_pref=1024):
    M, K = x.shape
    N = w.shape[1]
    tm, tn = _tile(M, tm_pref), _tile(N, tn_pref)
    return pl.pallas_call(
        _norm_matmul_kernel,
        grid=(M // tm, N // tn),
        in_specs=[pl.BlockSpec((tm, K), lambda i, j: (i, 0)),
                  pl.BlockSpec((1, K), lambda i, j: (0, 0)),
                  pl.BlockSpec((K, tn), lambda i, j: (0, j))],
        out_specs=pl.BlockSpec((tm, tn), lambda i, j: (i, j)),
        out_shape=jax.ShapeDtypeStruct((M, N), F32),
        scratch_shapes=[pltpu.VMEM((tm, K), BF16)],
        compiler_params=_cparams("parallel", "arbitrary"),
    )(x, gain, w)


def _qk_prep_kernel(zq_ref, zk_ref, zv_ref, cos_ref, sin_ref, qg_ref, kg_ref, grp_ref,
                    q_ref, k_ref, kb_ref, v_ref, vb_ref, *, transposed):
    cos = cos_ref[...]
    sin = sin_ref[...]
    lane = lax.broadcasted_iota(jnp.int32, cos.shape, 1)
    first_half = (lane % DK_ATT) < (DK_ATT // 2)
    grp = grp_ref[...]

    def norm_rope(x, gain):
        hi, mid, lo = _split3(x * x)
        ss = (_dot(hi, grp) + _dot(mid, grp) + _dot(lo, grp)) * (1.0 / DK_ATT)
        y = x * lax.rsqrt(ss + EPS) * gain
        swapped = jnp.where(first_half, pltpu.roll(y, HEAD_W - DK_ATT // 2, 1),
                            pltpu.roll(y, DK_ATT // 2, 1))
        return y * cos + swapped * sin

    for h in range(H_ATT):
        sl = slice(h * HEAD_W, (h + 1) * HEAD_W)
        q = norm_rope(zq_ref[:, sl], qg_ref[...]) * Q_SCALE
        k = norm_rope(zk_ref[:, sl], kg_ref[...])
        k_ref[:, sl] = k
        kb_ref[:, sl] = k.astype(BF16)
        v = zv_ref[:, sl]
        v_ref[:, sl] = v
        if transposed:
            q_ref[0, sl, :] = q.T.astype(BF16)
            r0 = h * VT_ROWS
            vb_ref[0, r0:r0 + HEAD_W, :] = v.T.astype(BF16)
            pad = (VT_ROWS - HEAD_W, v.shape[0])
            vb_ref[0, r0 + HEAD_W:r0 + VT_ROWS, :] = (lax.broadcasted_iota(jnp.int32, pad, 0) == 0).astype(BF16)
        else:
            q_ref[:, sl] = q.astype(BF16)
            vb_ref[:, sl] = v.astype(BF16)


def qk_prep(z, cos, sin, q_gain, k_gain, seqs=None, tm_pref=512):
    M = z.shape[0]
    tm = _tile(M if seqs is None else seqs[1], tm_pref)
    lane = jnp.arange(HEAD_W)
    grp = (lane[:, None] // DK_ATT == lane[None, :] // DK_ATT).astype(BF16)
    col = lambda c: pl.BlockSpec((tm, D_HEADS), lambda i: (i, c))
    row = pl.BlockSpec((tm, HEAD_W), lambda i: (i, 0))
    vec = pl.BlockSpec((1, HEAD_W), lambda i: (0, 0))
    out = pl.BlockSpec((tm, D_HEADS), lambda i: (i, 0))
    sd = lambda dt: jax.ShapeDtypeStruct((M, D_HEADS), dt)
    if seqs is None:
        out_q, out_v, sd_q, sd_v = out, out, sd(BF16), sd(BF16)
    else:
        B, S = seqs
        tiles = S // tm
        out_q = pl.BlockSpec((1, D_HEADS, tm), lambda i: (i // tiles, 0, i % tiles))
        out_v = pl.BlockSpec((1, H_ATT * VT_ROWS, tm), lambda i: (i // tiles, 0, i % tiles))
        sd_q = jax.ShapeDtypeStruct((B, D_HEADS, S), BF16)
        sd_v = jax.ShapeDtypeStruct((B, H_ATT * VT_ROWS, S), BF16)
    return pl.pallas_call(
        functools.partial(_qk_prep_kernel, transposed=seqs is not None),
        grid=(M // tm,),
        in_specs=[col(0), col(1), col(2), row, row, vec, vec,
                  pl.BlockSpec((HEAD_W, HEAD_W), lambda i: (0, 0))],
        out_specs=(out_q, out, out, out, out_v),
        out_shape=(sd_q, sd(F32), sd(BF16), sd(F32), sd_v),
        compiler_params=_cparams("parallel"),
    )(z, z, z, cos, sin, q_gain, k_gain, grp)


def _attn_prompt_kernel(lam_ref, qt_ref, k_ref, vt_ref, g_ref, o_ref, m_scr, acc_scr, *, tq, tk, out_scale):
    qi = pl.program_id(2)
    n_full = (qi * tq) // tk
    dim = lax.broadcasted_iota(jnp.int32, (HEAD_W, tq), 0)
    q2t = []
    for u in range(ATT_HEADS_PER_STEP):
        qt = qt_ref[0, u * HEAD_W:(u + 1) * HEAD_W, :]
        zero = jnp.zeros_like(qt)
        q2t.append(jnp.concatenate([jnp.where(dim < DK_ATT, qt, zero),
                                    jnp.where(dim >= DK_ATT, qt, zero)], axis=1))
    m_scr[...] = jnp.full(m_scr.shape, NEG_INF, F32)
    acc_scr[...] = jnp.zeros(acc_scr.shape, F32)

    def blocks(kbs, masked):
        units = [(pl.multiple_of(kb * tk, tk), u) for kb in kbs for u in range(ATT_HEADS_PER_STEP)]

        def scores(start, u):
            return _dot(k_ref[0, pl.ds(start, tk), u * HEAD_W:(u + 1) * HEAD_W], q2t[u])

        s_next = scores(*units[0])
        for i, (start, u) in enumerate(units):
            s = s_next
            if i + 1 < len(units):
                s_next = scores(*units[i + 1])
            if masked:
                key = start + lax.broadcasted_iota(jnp.int32, s.shape, 0)
                qry = qi * tq + lax.broadcasted_iota(jnp.int32, s.shape, 1) % tq
                s = jnp.where(key <= qry, s, NEG_INF)
            m_prev = m_scr[u]
            m_new = jnp.maximum(m_prev, jnp.max(s, axis=0, keepdims=True))
            m_scr[u] = m_new
            p = jnp.exp2(s - m_new).astype(BF16)
            vt = vt_ref[0, u * VT_ROWS:(u + 1) * VT_ROWS, pl.ds(start, tk)]
            acc_scr[u] = jnp.exp2(m_prev - m_new) * acc_scr[u] + _dot(vt, p)

    def pair(i, carry):
        blocks([2 * i, 2 * i + 1], False)
        return carry

    lax.fori_loop(0, n_full // 2, pair, 0)

    @pl.when(n_full % 2 == 1)
    def _():
        blocks([n_full - 1], False)

    blocks([n_full], True)
    for u in range(ATT_HEADS_PER_STEP):
        acc = acc_scr[u]
        o = acc[:HEAD_W] / acc[HEAD_W:HEAD_W + 1]
        od = o[:, :tq] - lam_ref[:, :1] * o[:, tq:]
        y = od * lax.rsqrt(jnp.mean(od * od, axis=0, keepdims=True) + EPS) * g_ref[...] * out_scale
        o_ref[0, :, u * HEAD_W:(u + 1) * HEAD_W] = y.T.astype(BF16)


def attn_prompt(qt, k, vt, lam, att_gain, out_scale, tq_pref=256, tk_pref=512):
    B, S, _ = k.shape
    tq = _tile(S, tq_pref)
    tk = max(tq, _tile(S, tk_pref))
    U = ATT_HEADS_PER_STEP
    return pl.pallas_call(
        functools.partial(_attn_prompt_kernel, tq=tq, tk=tk, out_scale=out_scale),
        grid=(B, H_ATT // U, S // tq),
        in_specs=[pl.BlockSpec((1, HEAD_W), lambda b, h, i: (0, 0)),
                  pl.BlockSpec((1, U * HEAD_W, tq), lambda b, h, i: (b, h, i)),
                  pl.BlockSpec((1, S, U * HEAD_W), lambda b, h, i: (b, 0, h)),
                  pl.BlockSpec((1, U * VT_ROWS, S), lambda b, h, i: (b, h, 0)),
                  pl.BlockSpec((HEAD_W, 1), lambda b, h, i: (0, 0))],
        out_specs=pl.BlockSpec((1, tq, U * HEAD_W), lambda b, h, i: (b, i, h)),
        out_shape=jax.ShapeDtypeStruct(k.shape, BF16),
        scratch_shapes=[pltpu.VMEM((U, 1, 2 * tq), F32), pltpu.VMEM((U, VT_ROWS, 2 * tq), F32)],
        compiler_params=_cparams("parallel", "parallel", "arbitrary"),
    )(lam, qt, k, vt, att_gain.reshape(HEAD_W, 1))


PAGE_KEYS = PAGE_SIZE * H_ATT


def _decode_query(q_ref):
    rows = 2 * H_ATT
    q = q_ref[0]
    q16 = jnp.concatenate([q, q], axis=0)
    r = lax.broadcasted_iota(jnp.int32, (rows, HEAD_W), 0)
    c = lax.broadcasted_iota(jnp.int32, (rows, HEAD_W), 1)
    qm = jnp.where(c // DK_ATT == r // H_ATT, q16, jnp.zeros_like(q16))
    own_head = (lax.broadcasted_iota(jnp.int32, (rows, PAGE_KEYS), 1) % H_ATT
                == lax.broadcasted_iota(jnp.int32, (rows, PAGE_KEYS), 0) % H_ATT)
    return qm, own_head


def _page_scores(qm, own_head, k_ref):
    return jnp.where(own_head, _dot_nt(qm, k_ref[0].reshape(PAGE_KEYS, HEAD_W).astype(BF16)), NEG_INF)


def _decode_weights(qm, s, kn_ref):
    k_new = jnp.concatenate([kn_ref[0], kn_ref[0]], axis=0)
    s_new = jnp.sum(qm.astype(F32) * k_new, axis=-1, keepdims=True)
    m = jnp.maximum(jnp.max(functools.reduce(jnp.maximum, s), axis=-1, keepdims=True), s_new)
    p = [jnp.exp2(sj - m) for sj in s]
    p_new = jnp.exp2(s_new - m)
    l = jnp.sum(functools.reduce(jnp.add, p), axis=-1, keepdims=True) + p_new
    return p, p_new, l


def _page_values(p, v_ref):
    return _dot(p.astype(BF16), v_ref[0].reshape(PAGE_KEYS, HEAD_W).astype(BF16))


def _decode_finish(values, p_new, l, vn_ref, lam_ref, g_ref, out_scale):
    v_new = jnp.concatenate([vn_ref[0], vn_ref[0]], axis=0)
    o = (functools.reduce(jnp.add, values) + p_new * v_new) / l
    od = o[:H_ATT] - lam_ref[...] * o[H_ATT:]
    return _rms(od, g_ref[...]) * out_scale


def _attn_sample_kernel(pt_ref, lam_ref, q_ref, kn_ref, vn_ref, g_ref, *refs, n_pages, out_scale):
    del pt_ref
    k_refs, v_refs, o_ref = refs[:n_pages], refs[n_pages:2 * n_pages], refs[2 * n_pages]
    qm, own_head = _decode_query(q_ref)
    p, p_new, l = _decode_weights(qm, [_page_scores(qm, own_head, k_ref) for k_ref in k_refs], kn_ref)
    values = [_page_values(pj, v_ref) for pj, v_ref in zip(p, v_refs)]
    o_ref[0] = _decode_finish(values, p_new, l, vn_ref, lam_ref, g_ref, out_scale)


def attn_sample(q, k_new, v_new, cache_k, cache_v, page_table, lam, att_gain, out_scale):
    DB = q.shape[0]
    n_pages = page_table.shape[1]
    tok =pl.BlockSpec((1, H_ATT, HEAD_W), lambda b, pt: (b, 0, 0))
    vec = pl.BlockSpec((1, HEAD_W), lambda b, pt: (0, 0))
    page = lambda j: pl.BlockSpec((1, PAGE_SIZE, H_ATT, HEAD_W),
                                  lambda b, pt: (pt[b * n_pages + j], 0, 0, 0))
    pages = [page(j) for j in range(n_pages)]
    return pl.pallas_call(
        functools.partial(_attn_sample_kernel, n_pages=n_pages, out_scale=out_scale),
        grid_spec=pltpu.PrefetchScalarGridSpec(
            num_scalar_prefetch=1,
            grid=(DB,),
            in_specs=[vec, tok, tok, tok, vec] + pages + pages,
            out_specs=tok,
        ),
        out_shape=jax.ShapeDtypeStruct((DB, H_ATT, HEAD_W), F32),
        compiler_params=_cparams("arbitrary"),
    )(page_table.reshape(-1), lam, q, k_new, v_new, att_gain,
      *([cache_k] * n_pages), *([cache_v] * n_pages))


def _gates(zf, lb):
    sig = jax.nn.sigmoid(zf)
    f = lb + (1.0 - lb) * sig
    return jnp.log(f), (1.0 - lb) * (1.0 - sig)


def _hgrn_prompt_kernel(q_ref, f_ref, i_ref, g_ref, lb_ref, gain_ref, tril_ref, o_ref, st_ref, s_scr,
                        *, rows, chunk):
    t = pl.program_id(1)

    @pl.when(t == 0)
    def _():
        s_scr[...] = jnp.zeros_like(s_scr)

    tril = tril_ref[...]
    r = lax.broadcasted_iota(jnp.int32, (rows, rows), 0)
    c = lax.broadcasted_iota(jnp.int32, (rows, rows), 1)
    band = (c <= r) & (c // chunk == r // chunk)
    mid = chunk // 2 - 1
    n_chunks = rows // chunk

    heads = [slice(h * HEAD_W, (h + 1) * HEAD_W) for h in range(H_RNN)]
    chunks = [slice(c * chunk, (c + 1) * chunk) for c in range(n_chunks)]
    gates = [_gates(f_ref[:, sl], lb_ref[:, sl]) for sl in heads]
    cum = []
    for log_f, _ in gates:
        hi, md, lo = _split3(log_f)
        cum.append(_dot(tril, hi) + _dot(tril, md) + _dot(tril, lo))
    v = [i_ref[:, sl].astype(BF16) for sl in heads]
    q_mid, k_mid, q_dec, k_end, decay = ([[] for _ in heads] for _ in range(5))
    for h, sl in enumerate(heads):
        q, kk, bc = q_ref[:, sl], gates[h][1], cum[h]
        for c, rc in enumerate(chunks):
            bl = bc[rc]
            if c:
                bl = bl - bc[rc.start - 1:rc.start]
            b_mid, b_last = bl[mid:mid + 1], bl[chunk - 1:chunk]
            q_mid[h].append(q[rc] * jnp.exp(bl - b_mid))
            k_mid[h].append(kk[rc] * jnp.exp(b_mid - bl))
            q_dec[h].append((q[rc] * jnp.exp(bl)).astype(BF16))
            k_end[h].append((kk[rc] * jnp.exp(b_last - bl)).astype(BF16))
            decay[h].append(jnp.exp(b_last))
    a = [_dot_nt(jnp.concatenate(q_mid[h], axis=0).astype(BF16), jnp.concatenate(k_mid[h], axis=0).astype(BF16))
         for h in range(H_RNN)]
    intra = [_dot(jnp.where(band, a[h], 0.0).astype(BF16), v[h]) for h in range(H_RNN)]
    update = [[_dot_tn(v[h][rc], k_end[h][c]) for c, rc in enumerate(chunks)] for h in range(H_RNN)]
    st = [s_scr[h] for h in range(H_RNN)]
    for c, rc in enumerate(chunks):
        for h, sl in enumerate(heads):
            o = intra[h][rc] + _dot_nt(q_dec[h][c], st[h].astype(BF16))
            st[h] = st[h] * decay[h][c] + update[h][c]
            g = g_ref[rc, sl]
            o_ref[rc, sl] = (_rms(o, gain_ref[...]) * (g * jax.nn.sigmoid(g))).astype(BF16)
    for h in range(H_RNN):
        s_scr[h] = st[h]

    @pl.when(t == pl.num_programs(1) - 1)
    def _():
        for h in range(H_RNN):
            st_ref[0, h] = s_scr[h].T


def hgrn_prompt(z, lb, rnn_gain, B, S):
    rows = _tile(S, RNN_ROWS)
    chunk = _tile(rows, RNN_CHUNK)
    nt = S // rows
    tril = (jnp.arange(rows)[:, None] >= jnp.arange(rows)[None, :]).astype(BF16)
    col = lambda c: pl.BlockSpec((rows, D_HEADS), lambda b, t: (b * nt + t, c))
    return pl.pallas_call(
        functools.partial(_hgrn_prompt_kernel, rows=rows, chunk=chunk),
        grid=(B, nt),
        in_specs=[col(3), col(4), col(5), col(6),
                  pl.BlockSpec((1, D_HEADS), lambda b, t: (0, 0)),
                  pl.BlockSpec((1, HEAD_W), lambda b, t: (0, 0)),
                  pl.BlockSpec((rows, rows), lambda b, t: (0, 0))],
        out_specs=(pl.BlockSpec((rows, D_HEADS), lambda b, t: (b * nt + t, 0)),
                   pl.BlockSpec((1, H_RNN, HEAD_W, HEAD_W), lambda b, t: (b, 0, 0, 0))),
        out_shape=(jax.ShapeDtypeStruct((B * S, D_HEADS), BF16),
                   jax.ShapeDtypeStruct((B, H_RNN, HEAD_W, HEAD_W), F32)),
        scratch_shapes=[pltpu.VMEM((H_RNN, HEAD_W, HEAD_W), F32)],
        compiler_params=_cparams("parallel", "arbitrary"),
    )(z, z, z, z, lb, rnn_gain, tril)


def _hgrn_step_kernel(q_ref, f_ref, i_ref, g_ref, lb_ref, gain_ref, s0_ref, o_ref, s1_ref, *, bs):
    def column(rowvec):
        return jnp.broadcast_to(rowvec, (HEAD_W, HEAD_W)).T

    def head(h, _):
        sl = pl.ds(pl.multiple_of(h * HEAD_W, HEAD_W), HEAD_W)
        log_f, kk = _gates(f_ref[:, sl], lb_ref[:, sl])
        q, v, g = q_ref[:, sl], i_ref[:, sl], g_ref[:, sl]
        decay = jnp.exp(log_f)
        q_dec = q * decay
        a = jnp.sum(q_dec * (kk * jnp.exp(-log_f)), axis=-1, keepdims=True)
        inter = []
        for b in range(bs):
            s0 = s0_ref[b, h]
            q_rows = jnp.broadcast_to(q_dec[b:b + 1], (16, HEAD_W)).astype(BF16)
            inter.append(_dot(q_rows, s0.astype(BF16))[0:1])
            s1_ref[b, h] = column(decay[b:b + 1]) * s0 + column(kk[b:b + 1]) * v[b:b + 1]
        o = a * v + jnp.concatenate(inter, axis=0)
        o_ref[:, sl] = (_rms(o, gain_ref[...]) * (g * jax.nn.sigmoid(g))).astype(BF16)
        return 0

    lax.fori_loop(0, H_RNN, head, 0)


def hgrn_step(z, lb, rnn_gain, s0, bs_pref=16):
    DB = z.shape[0]
    bs = _tile(DB, bs_pref)
    col = lambda c: pl.BlockSpec((bs, D_HEADS), lambda i: (i, c))
    st = pl.BlockSpec((bs, H_RNN, HEAD_W, HEAD_W), lambda i: (i, 0, 0, 0))
    return pl.pallas_call(
        functools.partial(_hgrn_step_kernel, bs=bs),
        grid=(DB // bs,),
        in_specs=[col(3), col(4), col(5), col(6),
                  pl.BlockSpec((1, D_HEADS), lambda i: (0, 0)),
                  pl.BlockSpec((1, HEAD_W), lambda i: (0, 0)), st],
        out_specs=(pl.BlockSpec((bs, D_HEADS), lambda i: (i, 0)), st),
        out_shape=(jax.ShapeDtypeStruct((DB, D_HEADS), BF16), jax.ShapeDtypeStruct(s0.shape, F32)),
        compiler_params=_cparams("parallel"),
    )(z, z, z, z, lb, rnn_gain, s0)


def _out_proj_kernel(a_ref, r_ref, wa_ref, wr_ref, x_ref, g_ref, x1_ref, h1_ref):
    x1 = x_ref[...] + _dot(a_ref[...], wa_ref[...]) + _dot(r_ref[...], wr_ref[...])
    x1_ref[...] = x1
    h1_ref[...] = _rms(x1, g_ref[...]).astype(BF16)


def out_proj(o_att, o_rnn, w_out, x, ffn_gain, tm_pref=512):
    M, D = x.shape
    tm = _tile(M, tm_pref)
    act = pl.BlockSpec((tm, D_HEADS), lambda i: (i, 0))
    full = pl.BlockSpec((tm, D), lambda i: (i, 0))
    return pl.pallas_call(
        _out_proj_kernel,
        grid=(M // tm,),
        in_specs=[act, act,
                  pl.BlockSpec((D_HEADS, D), lambda i: (0, 0)),
                  pl.BlockSpec((D_HEADS, D), lambda i: (1, 0)),
                  full, pl.BlockSpec((1, D), lambda i: (0, 0))],
        out_specs=(full, full),
        out_shape=(jax.ShapeDtypeStruct((M, D), F32), jax.ShapeDtypeStruct((M, D), BF16)),
        compiler_params=_cparams("parallel"),
    )(o_att, o_rnn, w_out, w_out, x, ffn_gain)


def _conv_gate(g, u, first_tile, cw_ref, cb_ref, act_ref, tail_ref, carry_ref):
    @pl.when(first_tile)
    def _():
        carry_ref[...] = jnp.zeros_like(carry_ref)

    tm = g.shape[0]
    r = lax.broadcasted_iota(jnp.int32, g.shape, 0)
    prev2, prev1 = carry_ref[0:1], carry_ref[1:2]
    g1 = jnp.where(r == 0, prev1, pltpu.roll(g, 1, 0))
    g2 = jnp.where(r == 0, prev2, jnp.where(r == 1, prev1, pltpu.roll(g, 2, 0)))
    cw = cw_ref[...]
    c = cb_ref[...] + g2 * cw[0:1] + g1 * cw[1:2] + g * cw[2:3]
    act_ref[...] = (c * jax.nn.sigmoid(c) * u).astype(BF16)
    tail = g[tm - (CONV_W - 1):]
    carry_ref[0:CONV_W - 1] = tail
    tail_ref[0] = tail


def _ffn_seq_kernel(h_ref, wg_ref, wu_ref, cw_ref, cb_ref, act_ref, tail_ref, carry_ref, *, tiles_per_seq):
    h = h_ref[...]
    g = _dot(h, wg_ref[...])
    u = _dot(h, wu_ref[...])
    _conv_gate(g, u, pl.program_id(1) % tiles_per_seq == 0, cw_ref, cb_ref, act_ref, tail_ref, carry_ref)


def _ffn_decode_kernel(pt_ref, h_ref, wg_ref, wu_ref, cw_ref, cb_ref, lam_ref, q_ref, kn_ref, vn_ref, g_ref,
                       *refs, n_pages, tiles_m, tiles_per_seq, out_scale):
    del pt_ref
    k_refs, v_refs = refs[:n_pages], refs[n_pages:2 * n_pages]
    act_ref, tail_ref, o_ref, carry_ref = refs[2 * n_pages:]
    first_tile = (pl.program_id(0) % tiles_m) % tiles_per_seq == 0
    qm, own_head = _decode_query(q_ref)
    s = [_page_scores(qm, own_head, k_ref) for k_ref in k_refs]
    h = h_ref[...]
    g = _dot(h, wg_ref[...])
    u = _dot(h, wu_ref[...])
    p, p_new, l = _decode_weights(qm, s, kn_ref)
    values = [_page_values(pj, v_ref) for pj, v_ref in zip(p, v_refs)]
    o_ref[0] = _decode_finish(values, p_new, l, vn_ref, lam_ref, g_ref, out_scale)
    _conv_gate(g, u, first_tile, cw_ref, cb_ref, act_ref, tail_ref, carry_ref)


def ffn_seq_decode_tiles(M, S, F, DB, tm_pref=256, tn_pref=1408):
    tm, tn = _tile(S, tm_pref), _tile(F, tn_pref)
    return (tm, tn) if tn % HEAD_W == 0 and (M // tm) * (F // tn) == DB else None


def ffn_seq_decode(h, w_gate, w_up, conv_w, conv_b, B, S, tiles,
                   q, k_new, v_new, cache_k, cache_v, page_table, lam, att_gain, out_scale):
    M, D = h.shape
    F = w_gate.shape[1]
    DB, n_pages = page_table.shape
    tm, tn = tiles
    tiles_m, tiles_per_seq = M // tm, S // tm
    row = lambda i: i % tiles_m
    col = lambda i: i // tiles_m
    wspec = pl.BlockSpec((D, tn), lambda i, pt: (0, col(i)), pipeline_mode=pl.Buffered(1))
    tok = pl.BlockSpec((1, H_ATT, HEAD_W), lambda i, pt: (i, 0, 0))
    vec = pl.BlockSpec((1, HEAD_W), lambda i, pt: (0, 0))
    page = lambda j: pl.BlockSpec((1, PAGE_SIZE, H_ATT, HEAD_W),
                                  lambda i, pt: (pt[i * n_pages + j], 0, 0, 0))
    pages = [page(j) for j in range(n_pages)]
    return pl.pallas_call(
        functools.partial(_ffn_decode_kernel, n_pages=n_pages, tiles_m=tiles_m,
                          tiles_per_seq=tiles_per_seq, out_scale=out_scale),
        grid_spec=pltpu.PrefetchScalarGridSpec(
            num_scalar_prefetch=1,
            grid=(DB,),
            in_specs=[pl.BlockSpec((tm, D), lambda i, pt: (row(i), 0)), wspec, wspec,
                      pl.BlockSpec((CONV_W, tn), lambda i, pt: (0, col(i))),
                      pl.BlockSpec((1, tn), lambda i, pt: (0, col(i))),
                      vec, tok, tok, tok, vec] + pages + pages,
            out_specs=(pl.BlockSpec((tm, tn), lambda i, pt: (row(i), col(i))),
                       pl.BlockSpec((1, CONV_W - 1, tn), lambda i, pt: (row(i) // tiles_per_seq, 0, col(i))),
                       tok),
            scratch_shapes=[pltpu.VMEM((8, tn), F32)],
        ),
        out_shape=(jax.ShapeDtypeStruct((M, F), BF16), jax.ShapeDtypeStruct((B, CONV_W - 1, F), F32),
                   jax.ShapeDtypeStruct((DB, H_ATT, HEAD_W), F32)),
        compiler_params=_cparams("arbitrary"),
    )(page_table.reshape(-1), h, w_gate, w_up, conv_w, conv_b, lam, q, k_new, v_new, att_gain,
      *([cache_k] * n_pages), *([cache_v] * n_pages))


def ffn_seq(h, w_gate, w_up, conv_w, conv_b, B, S, tm_pref=512, tn_pref=1408):
    M, D = h.shape
    F = w_gate.shape[1]
    tm, tn = _tile(S, tm_pref), _tile(F, tn_pref)
    tiles_per_seq = S // tm
    wspec = pl.BlockSpec((D, tn), lambda n, m: (0, n))
    return pl.pallas_call(
        functools.partial(_ffn_seq_kernel, tiles_per_seq=tiles_per_seq),
        grid=(F // tn, M // tm),
        in_specs=[pl.BlockSpec((tm, D), lambda n, m: (m, 0)), wspec, wspec,
                  pl.BlockSpec((CONV_W, tn), lambda n, m: (0, n)),
                  pl.BlockSpec((1, tn), lambda n, m: (0, n))],
        out_specs=(pl.BlockSpec((tm, tn), lambda n, m: (m, n)),
                   pl.BlockSpec((1, CONV_W - 1, tn), lambda n, m: (m // tiles_per_seq, 0, n))),
        out_shape=(jax.ShapeDtypeStruct((M, F), BF16), jax.ShapeDtypeStruct((B, CONV_W - 1, F), F32)),
        scratch_shapes=[pltpu.VMEM((8, tn), F32)],
        compiler_params=_cparams("parallel", "arbitrary"),
    )(h, w_gate, w_up, conv_w, conv_b)


def _ffn_step_kernel(h_ref, wg_ref, wu_ref, cw_ref, cb_ref, p2_ref, p1_ref, act_ref, g_ref):
    h = h_ref[...]
    g = _dot(h, wg_ref[...])
    u = _dot(h, wu_ref[...])
    cw = cw_ref[...]
    c = cb_ref[...] + p2_ref[...] * cw[0:1] + p1_ref[...] * cw[1:2] + g * cw[2:3]
    act_ref[...] = (c * jax.nn.sigmoid(c) * u).astype(BF16)
    g_ref[...] = g


def ffn_step(h, w_gate, w_up, conv_w, conv_b, prev2, prev1, tn_pref=1408):
    M, D = h.shape
    F = w_gate.shape[1]
    tn = _tile(F, tn_pref)
    wspec = pl.BlockSpec((D, tn), lambda n: (0, n))
    tile = pl.BlockSpec((M, tn), lambda n: (0, n))
    return pl.pallas_call(
        _ffn_step_kernel,
        grid=(F // tn,),
        in_specs=[pl.BlockSpec((M, D), lambda n: (0, 0)), wspec, wspec,
                  pl.BlockSpec((CONV_W, tn), lambda n: (0, n)),
                  pl.BlockSpec((1, tn), lambda n: (0, n)), tile, tile],
        out_specs=(tile, tile),
        out_shape=(jax.ShapeDtypeStruct((M, F), BF16), jax.ShapeDtypeStruct((M, F), F32)),
        compiler_params=_cparams("parallel"),
    )(h, w_gate, w_up, conv_w, conv_b, prev2, prev1)


def _down_kernel(a_ref, w_ref, x_ref, o_ref):
    o_ref[...] = x_ref[...] + _dot(a_ref[...], w_ref[...])


def ffn_down(act, w_down, x, tm_pref=1024, tn_pref=512):
    M, F = act.shape
    D = w_down.shape[1]
    tm, tn = _tile(M, tm_pref), _tile(D, tn_pref)
    tile = pl.BlockSpec((tm, tn), lambda i, j: (i, j))
    return pl.pallas_call(
        _down_kernel,
        grid=(M // tm, D // tn),
        in_specs=[pl.BlockSpec((tm, F), lambda i, j: (i, 0)),
                  pl.BlockSpec((F, tn), lambda i, j: (0, j)), tile],
        out_specs=tile,
        out_shape=jax.ShapeDtypeStruct((M, D), F32),
        compiler_params=_cparams("parallel", "arbitrary"),
    )(act, w_down, x)


def _ple_kernel(x_ref, g_ref, wg_ref, p_ref, wp_ref, o_ref, h_ref, *, tn):
    j = pl.program_id(1)

    @pl.when(j == 0)
    def _():
        h_ref[...] = _rms(x_ref[...], g_ref[...]).astype(BF16)

    gate = jax.nn.sigmoid(_dot(h_ref[...], wg_ref[...]))
    emb = _dot(p_ref[...].astype(BF16), wp_ref[...])
    o_ref[...] = x_ref[:, pl.ds(pl.multiple_of(j * tn, tn), tn)] + gate * emb


def ple(x, gain, w_gate, p, w_proj, tm_pref=512, tn_pref=2048):
    M, D = x.shape
    DP = p.shape[1]
    tm, tn = _tile(M, tm_pref), _tile(D, tn_pref)
    return pl.pallas_call(
        functools.partial(_ple_kernel, tn=tn),
        grid=(M // tm, D // tn),
        in_specs=[pl.BlockSpec((tm, D), lambda i, j: (i, 0)),
                  pl.BlockSpec((1, D), lambda i, j: (0, 0)),
                  pl.BlockSpec((D, tn), lambda i, j: (0, j)),
                  pl.BlockSpec((tm, DP), lambda i, j: (i, 0)),
                  pl.BlockSpec((DP, tn), lambda i, j: (0, j))],
        out_specs=pl.BlockSpec((tm, tn), lambda i, j: (i, j)),
        out_shape=jax.ShapeDtypeStruct((M, D), F32),
        scratch_shapes=[pltpu.VMEM((tm, D), BF16)],
        compiler_params=_cparams("parallel", "arbitrary"),
    )(x, gain, w_gate, p, w_proj)


def _rope_tables(pos):
    half = DK_ATT // 2
    inv_freq = 1.0 / (ROPE_THETA ** (jnp.arange(half, dtype=F32) * (2.0 / DK_ATT)))
    ang = pos.astype(F32)[:, None] * inv_freq[None, :]
    cos, sin = jnp.cos(ang), jnp.sin(ang)
    return (jnp.concatenate([cos] * 4, axis=-1), jnp.concatenate([-sin, sin] * 2, axis=-1))


def _row(v):
    return v.reshape(1, -1).astype(F32)


def kernel(x_prompt, x_sample, p_prompt, p_sample, cache_k, cache_v, state_rnn, state_ffn_conv, page_table,
           norm_mix, w_in, q_norm, k_norm, lam_q1, lam_k1, lam_q2, lam_k2, att_out_norm, rnn_out_norm,
           lower_bounds, w_out, norm_ffn, w_gate, w_up, conv_w, conv_b, w_down, norm_ple, w_ple_gate,
           w_ple_proj):
    B, S, D = x_prompt.shape
    DB, T, _ = x_sample.shape
    assert T == 1, "the sample group is implemented for one new token per sequence"
    depth = w_in.shape[0]
    past = page_table.shape[1] * PAGE_SIZE
    cos_p, sin_p = _rope_tables(jnp.arange(S, dtype=jnp.int32))
    cos_p, sin_p = jnp.tile(cos_p, (B, 1)), jnp.tile(sin_p, (B, 1))
    cos_s, sin_s = _rope_tables(jnp.full((DB,), past, dtype=jnp.int32))

    yp = x_prompt.reshape(B * S, D)
    ys = x_sample.reshape(DB, D)
    outs = [[] for _ in range(8)]
    for i in range(depth):
        lam_init = 0.8 - 0.6 * math.exp(-0.3 * i)
        out_scale = 1.0 - lam_init
        lam, lb = layer_params(jnp.stack([lam_q1[i], lam_k1[i], lam_q2[i], lam_k2[i]]).astype(F32),
                               lower_bounds.astype(F32), i, lam_init)
        w_in_b, w_out_b = w_in[i].astype(BF16), w_out[i].astype(BF16)
        w_gate_b, w_up_b, w_down_b = w_gate[i].astype(BF16), w_up[i].astype(BF16), w_down[i].astype(BF16)
        w_pg_b, w_pp_b = w_ple_gate[i].astype(BF16), w_ple_proj[i].astype(BF16)
        q_gain = jnp.tile(_row(q_norm[i]), (1, 2))
        k_gain = jnp.tile(_row(k_norm[i]), (1, 2))
        att_gain, rnn_gain = _row(att_out_norm[i]), _row(rnn_out_norm[i])

        def mix_in(x, cos, sin, seqs=None):
            z = norm_matmul(x, _row(norm_mix[i]), w_in_b)
            return (z,) + tuple(qk_prep(z, cos, sin, q_gain, k_gain, seqs))

        def mix_out(x, o_att, o_rnn):
            return out_proj(o_att, o_rnn, w_out_b, x, _row(norm_ffn[i]))

        def embed(x, act, p):
            x = ffn_down(act, w_down_b, x)
            return ple(x, _row(norm_ple[i]), w_pg_b, p, w_pp_b)

        z, qt, k, kb, v, vt = mix_in(yp, cos_p, sin_p, (B, S))
        o_att = attn_prompt(qt, kb.reshape(B, S, -1), vt, lam, att_gain, out_scale).reshape(B * S, -1)
        o_rnn, rnn_p = hgrn_prompt(z, lb, rnn_gain, B, S)
        x1, h1 = mix_out(yp, o_att, o_rnn)
        outs[0].append(k.reshape(B, S // PAGE_SIZE, PAGE_SIZE, H_ATT, HEAD_W))
        outs[1].append(v.reshape(B, S // PAGE_SIZE, PAGE_SIZE, H_ATT, HEAD_W))
        outs[2].append(rnn_p.astype(state_rnn.dtype))

        z, q, k, kb, v, vb = mix_in(ys, cos_s, sin_s)
        per_head = lambda a: a.reshape(DB, H_ATT, HEAD_W)
        decode_args = (per_head(q), per_head(k), per_head(v), cache_k[i], cache_v[i], page_table,
                       lam, att_gain, out_scale)
        ffn_args = (h1, w_gate_b, w_up_b, conv_w[i], _row(conv_b[i]), B, S)
        tiles = ffn_seq_decode_tiles(B * S, S, w_gate_b.shape[1], DB)
        if tiles is not None:
            act, conv_p, o_att = ffn_seq_decode(*ffn_args, tiles, *decode_args)
        else:
            act, conv_p = ffn_seq(*ffn_args)
            o_att = attn_sample(*decode_args)
        o_att = o_att.reshape(DB, D_HEADS).astype(BF16)
        yp = embed(x1, act, p_prompt[i].reshape(B * S, -1))
        outs[3].append(conv_p)
        o_rnn, rnn_s = hgrn_step(z, lb, rnn_gain, state_rnn[i].astype(F32))
        x1, h1 = mix_out(ys, o_att, o_rnn)
        buf = state_ffn_conv[i]
        act, g_new = ffn_step(h1, w_gate_b, w_up_b, conv_w[i], _row(conv_b[i]), buf[:, 0], buf[:, 1])
        ys = embed(x1, act, p_sample[i].reshape(DB, -1))
        outs[4].append(k.reshape(DB, T, H_ATT, HEAD_W))
        outs[5].append(v.reshape(DB, T, H_ATT, HEAD_W))
        outs[6].append(rnn_s.astype(state_rnn.dtype))
        outs[7].append(jnp.stack([buf[:, 1], g_new], axis=1))

    return (yp.reshape(B, S, D), ys.reshape(DB, T, D)) + tuple(jnp.stack(o) for o in outs)
```

```python
import functools
import math

import jax
import jax.numpy as jnp
from jax import lax
from jax.experimental import pallas as pl
from jax.experimental.pallas import tpu as pltpu

H_ATT = 8
DK_ATT = 64
H_RNN = 8
HEAD_W = 128
D_HEADS = H_ATT * HEAD_W
PAGE_SIZE = 128
CONV_W = 3
ROPE_THETA = 10000.0
EPS = 1e-6
NEG_INF = -1e30
RNN_CHUNK = 32
RNN_ROWS = 256
ATT_HEADS_PER_STEP = 4
VT_ROWS = HEAD_W + 16
Q_SCALE = DK_ATT ** -0.5 * math.log2(math.e)
VMEM_LIMIT_V7X = 56 * 1024 * 1024

F32 = jnp.float32
BF16 = jnp.bfloat16


def _cparams(*sem):
    return pltpu.CompilerParams(dimension_semantics=sem, vmem_limit_bytes=VMEM_LIMIT_V7X)


def _tile(n, pref):
    t = min(n, pref)
    while n % t:
        t //= 2
    return t


def _split3(x):
    hi = x.astype(BF16)
    r1 = x - hi.astype(F32)
    mid = r1.astype(BF16)
    lo = (r1 - mid.astype(F32)).astype(BF16)
    return hi, mid, lo


def _dot(a, b):
    return jnp.dot(a, b, preferred_element_type=F32)


def _dot_nt(a, b):
    return lax.dot_general(a, b, (((1,), (1,)), ((), ())), preferred_element_type=F32)


def _dot_tn(a, b):
    return lax.dot_general(a, b, (((0,), (0,)), ((), ())), preferred_element_type=F32)


def _rms(x, gain):
    return x * lax.rsqrt(jnp.mean(x * x, axis=-1, keepdims=True) + EPS) * gain


def _params_kernel(lam_ref, lb_in_ref, lam_out_ref, lb_out_ref, *, layer, lam_init):
    v = lam_ref[...]
    s1 = jnp.sum(v[0:1] * v[1:2], axis=-1, keepdims=True)
    s2 = jnp.sum(v[2:3] * v[3:4], axis=-1, keepdims=True)
    lam = jnp.exp(s1) - jnp.exp(s2) + lam_init
    lam_out_ref[...] = jnp.broadcast_to(lam, lam_out_ref.shape)
    lb = lb_in_ref[...]
    e = jnp.exp(lb - jnp.max(lb, axis=0, keepdims=True))
    sm = e / jnp.sum(e, axis=0, keepdims=True)
    lb_out_ref[...] = jnp.sum(sm[0:layer + 1], axis=0, keepdims=True)


def layer_params(lam_vecs, lower_bounds, layer, lam_init):
    return pl.pallas_call(
        functools.partial(_params_kernel, layer=layer, lam_init=lam_init),
        out_shape=(jax.ShapeDtypeStruct((1, HEAD_W), F32),
                   jax.ShapeDtypeStruct((1, lower_bounds.shape[1]), F32)),
    )(lam_vecs, lower_bounds)


def _rounded(w_ref, wb_ref):
    if wb_ref is None:
        return w_ref[...]
    wb_ref[...] = w_ref[...].astype(BF16)
    return wb_ref[...]


def _weight_specs(w, block, index_map, row_tiles):
    if w.dtype == BF16:
        return (), ()
    assert row_tiles == 1
    return (pl.BlockSpec(block, index_map),), (jax.ShapeDtypeStruct(w.shape, BF16),)


def _norm_matmul_kernel(x_ref, g_ref, w_ref, o_ref, *rest):
    wb_ref, h_ref = (rest[0], rest[1]) if len(rest) == 2 else (None, rest[0])

    @pl.when(pl.program_id(1) == 0)
    def _():
        h_ref[...] = _rms(x_ref[...], g_ref[...]).astype(BF16)

    o_ref[...] = _dot(h_ref[...], _rounded(w_ref, wb_ref))


def norm_matmul(x, gain, w, tm_pref=1024, tn_pref=1024):
    M, K = x.shape
    N = w.shape[1]
    tm, tn = _tile(M, tm_pref), _tile(N, tn_pref)
    wmap = lambda i, j: (0, j)
    wb_spec, wb_shape = _weight_specs(w, (K, tn), wmap, M // tm)
    out = pl.pallas_call(
        _norm_matmul_kernel,
        grid=(M // tm, N // tn),
        in_specs=[pl.BlockSpec((tm, K), lambda i, j: (i, 0)),
                  pl.BlockSpec((1, K), lambda i, j: (0, 0)),
                  pl.BlockSpec((K, tn), wmap)],
        out_specs=(pl.BlockSpec((tm, tn), lambda i, j: (i, j)),) + wb_spec,
        out_shape=(jax.ShapeDtypeStruct((M, N), F32),) + wb_shape,
        scratch_shapes=[pltpu.VMEM((tm, K), BF16)],
        compiler_params=_cparams("parallel", "arbitrary"),
    )(x, gain, w)
    return out if wb_shape else out[0]


def _qk_prep_kernel(zq_ref, zk_ref, zv_ref, cos_ref, sin_ref, qg_ref, kg_ref, grp_ref,
                    q_ref, k_ref, kb_ref, v_ref, vb_ref, *, transposed):
    cos = cos_ref[...]
    sin = sin_ref[...]
    lane = lax.broadcasted_iota(jnp.int32, cos.shape, 1)
    first_half = (lane % DK_ATT) < (DK_ATT // 2)
    grp = grp_ref[...]

    def norm_rope(x, gain):
        hi, mid, lo = _split3(x * x)
        ss = (_dot(hi, grp) + _dot(mid, grp) + _dot(lo, grp)) * (1.0 / DK_ATT)
        y = x * lax.rsqrt(ss + EPS) * gain
        swapped = jnp.where(first_half, pltpu.roll(y, HEAD_W - DK_ATT // 2, 1),
                            pltpu.roll(y, DK_ATT // 2, 1))
        return y * cos + swapped * sin

    for h in range(H_ATT):
        sl = slice(h * HEAD_W, (h + 1) * HEAD_W)
        q = norm_rope(zq_ref[:, sl], qg_ref[...]) * Q_SCALE
        k = norm_rope(zk_ref[:, sl], kg_ref[...])
        k_ref[:, sl] = k
        kb_ref[:, sl] = k.astype(BF16)
        v = zv_ref[:, sl]
        v_ref[:, sl] = v
        if transposed:
            q_ref[0, sl, :] = q.T.astype(BF16)
            r0 = h * VT_ROWS
            vb_ref[0, r0:r0 + HEAD_W, :] = v.T.astype(BF16)
            pad = (VT_ROWS - HEAD_W, v.shape[0])
            vb_ref[0, r0 + HEAD_W:r0 + VT_ROWS, :] = (lax.broadcasted_iota(jnp.int32, pad, 0) == 0).astype(BF16)
        else:
            q_ref[:, sl] = q.astype(BF16)
            vb_ref[:, sl] = v.astype(BF16)


def qk_prep(z, cos, sin, q_gain, k_gain, seqs=None, tm_pref=512):
    M = z.shape[0]
    tm = _tile(M if seqs is None else seqs[1], tm_pref)
    lane = jnp.arange(HEAD_W)
    grp = (lane[:, None] // DK_ATT == lane[None, :] // DK_ATT).astype(BF16)
    col = lambda c: pl.BlockSpec((tm, D_HEADS), lambda i: (i, c))
    row = pl.BlockSpec((tm, HEAD_W), lambda i: (i, 0))
    vec = pl.BlockSpec((1, HEAD_W), lambda i: (0, 0))
    out = pl.BlockSpec((tm, D_HEADS), lambda i: (i, 0))
    sd = lambda dt: jax.ShapeDtypeStruct((M, D_HEADS), dt)
    if seqs is None:
        out_q, out_v, sd_q, sd_v = out, out, sd(BF16), sd(BF16)
    else:
        B, S = seqs
        tiles = S // tm
        out_q = pl.BlockSpec((1, D_HEADS, tm), lambda i: (i // tiles, 0, i % tiles))
        out_v = pl.BlockSpec((1, H_ATT * VT_ROWS, tm), lambda i: (i // tiles, 0, i % tiles))
        sd_q = jax.ShapeDtypeStruct((B, D_HEADS, S), BF16)
        sd_v = jax.ShapeDtypeStruct((B, H_ATT * VT_ROWS, S), BF16)
    return pl.pallas_call(
        functools.partial(_qk_prep_kernel, transposed=seqs is not None),
        grid=(M // tm,),
        in_specs=[col(0), col(1), col(2), row, row, vec, vec,
                  pl.BlockSpec((HEAD_W, HEAD_W), lambda i: (0, 0))],
        out_specs=(out_q, out, out, out, out_v),
        out_shape=(sd_q, sd(F32), sd(BF16), sd(F32), sd_v),
        compiler_params=_cparams("parallel"),
    )(z, z, z, cos, sin, q_gain, k_gain, grp)


def _attn_prompt_kernel(lam_ref, qt_ref, k_ref, vt_ref, g_ref, o_ref, m_scr, acc_scr, *, tq, tk, out_scale):
    qi = pl.program_id(2)
    n_full = (qi * tq) // tk
    dim = lax.broadcasted_iota(jnp.int32, (HEAD_W, tq), 0)
    q2t = []
    for u in range(ATT_HEADS_PER_STEP):
        qt = qt_ref[0, u * HEAD_W:(u + 1) * HEAD_W, :]
        zero = jnp.zeros_like(qt)
        q2t.append(jnp.concatenate([jnp.where(dim < DK_ATT, qt, zero),
                                    jnp.where(dim >= DK_ATT, qt, zero)], axis=1))
    m_scr[...] = jnp.full(m_scr.shape, NEG_INF, F32)
    acc_scr[...] = jnp.zeros(acc_scr.shape, F32)

    def blocks(kbs, masked):
        units = [(pl.multiple_of(kb * tk, tk), u) for kb in kbs for u in range(ATT_HEADS_PER_STEP)]

        def scores(start, u):
            return _dot(k_ref[0, pl.ds(start, tk), u * HEAD_W:(u + 1) * HEAD_W], q2t[u])

        s_next = scores(*units[0])
        for i, (start, u) in enumerate(units):
            s = s_next
            if i + 1 < len(units):
                s_next = scores(*units[i + 1])
            if masked:
                key = start + lax.broadcasted_iota(jnp.int32, s.shape, 0)
                qry = qi * tq + lax.broadcasted_iota(jnp.int32, s.shape, 1) % tq
                s = jnp.where(key <= qry, s, NEG_INF)
            m_prev = m_scr[u]
            m_new = jnp.maximum(m_prev, jnp.max(s, axis=0, keepdims=True))
            m_scr[u] = m_new
            p = jnp.exp2(s - m_new).astype(BF16)
            vt = vt_ref[0, u * VT_ROWS:(u + 1) * VT_ROWS, pl.ds(start, tk)]
            acc_scr[u] = jnp.exp2(m_prev - m_new) * acc_scr[u] + _dot(vt, p)

    def pair(i, carry):
        blocks([2 * i, 2 * i + 1], False)
        return carry

    lax.fori_loop(0, n_full // 2, pair, 0)

    @pl.when(n_full % 2 == 1)
    def _():
        blocks([n_full - 1], False)

    blocks([n_full], True)
    for u in range(ATT_HEADS_PER_STEP):
        acc = acc_scr[u]
        o = acc[:HEAD_W] / acc[HEAD_W:HEAD_W + 1]
        od = o[:, :tq] - lam_ref[:, :1] * o[:, tq:]
        y = od * lax.rsqrt(jnp.mean(od * od, axis=0, keepdims=True) + EPS) * g_ref[...] * out_scale
        o_ref[0, :, u * HEAD_W:(u + 1) * HEAD_W] = y.T.astype(BF16)


def attn_prompt(qt, k, vt, lam, att_gain, out_scale, tq_pref=256, tk_pref=512):
    B, S, _ = k.shape
    tq = _tile(S, tq_pref)
    tk = max(tq, _tile(S, tk_pref))
    U = ATT_HEADS_PER_STEP
    return pl.pallas_call(
        functools.partial(_attn_prompt_kernel, tq=tq, tk=tk, out_scale=out_scale),
        grid=(B, H_ATT // U, S // tq),
        in_specs=[pl.BlockSpec((1, HEAD_W), lambda b, h, i: (0, 0)),
                  pl.BlockSpec((1, U * HEAD_W, tq), lambda b, h, i: (b, h, i)),
                  pl.BlockSpec((1, S, U * HEAD_W), lambda b, h, i: (b, 0, h)),
                  pl.BlockSpec((1, U * VT_ROWS, S), lambda b, h, i: (b, h, 0)),
                  pl.BlockSpec((HEAD_W, 1), lambda b, h, i: (0, 0))],
        out_specs=pl.BlockSpec((1, tq, U * HEAD_W), lambda b, h, i: (b, i, h)),
        out_shape=jax.ShapeDtypeStruct(k.shape, BF16),
        scratch_shapes=[pltpu.VMEM((U, 1, 2 * tq), F32), pltpu.VMEM((U, VT_ROWS, 2 * tq), F32)],
        compiler_params=_cparams("parallel", "parallel", "arbitrary"),
    )(lam, qt, k, vt, att_gain.reshape(HEAD_W, 1))


PAGE_KEYS = PAGE_SIZE * H_ATT


def _decode_query(q_ref):
    rows = 2 * H_ATT
    q = q_ref[0]
    q16 = jnp.concatenate([q, q], axis=0)
    r = lax.broadcasted_iota(jnp.int32, (rows, HEAD_W), 0)
    c = lax.broadcasted_iota(jnp.int32, (rows, HEAD_W), 1)
    qm = jnp.where(c // DK_ATT == r // H_ATT, q16, jnp.zeros_like(q16))
    own_head = (lax.broadcasted_iota(jnp.int32, (rows, PAGE_KEYS), 1) % H_ATT
                == lax.broadcasted_iota(jnp.int32, (rows, PAGE_KEYS), 0) % H_ATT)
    return qm, own_head


def _page_scores(qm, own_head, k_ref):
    return jnp.where(own_head, _dot_nt(qm, k_ref[0].reshape(PAGE_KEYS, HEAD_W).astype(BF16)), NEG_INF)


def _decode_weights(qm, s, kn_ref):
    k_new = jnp.concatenate([kn_ref[0], kn_ref[0]], axis=0)
    s_new = jnp.sum(qm.astype(F32) * k_new, axis=-1, keepdims=True)
    m = jnp.maximum(jnp.max(functools.reduce(jnp.maximum, s), axis=-1, keepdims=True), s_new)
    p = [jnp.exp2(sj - m) for sj in s]
    p_new = jnp.exp2(s_new - m)
    l = jnp.sum(functools.reduce(jnp.add, p), axis=-1, keepdims=True) + p_new
    return p, p_new, l


def _page_values(p, v_ref):
    return _dot(p.astype(BF16), v_ref[0].reshape(PAGE_KEYS, HEAD_W).astype(BF16))


def _decode_finish(values, p_new, l, vn_ref, lam_ref, g_ref, out_scale):
    v_new = jnp.concatenate([vn_ref[0], vn_ref[0]], axis=0)
    o = (functools.reduce(jnp.add, values) + p_new * v_new) / l
    od = o[:H_ATT] - lam_ref[...] * o[H_ATT:]
    return _rms(od, g_ref[...]) * out_scale


def _attn_sample_kernel(pt_ref, lam_ref, q_ref, kn_ref, vn_ref, g_ref, *refs, n_pages, out_scale):
    del pt_ref
    k_refs, v_refs, o_ref = refs[:n_pages], refs[n_pages:2 * n_pages], refs[2 * n_pages]
    qm, own_head = _decode_query(q_ref)
    p, p_new, l = _decode_weights(qm, [_page_scores(qm, own_head, k_ref) for k_ref in k_refs], kn_ref)
    values = [_page_values(pj, v_ref) for pj, v_ref in zip(p, v_refs)]
    o_ref[0] = _decode_finish(values, p_new, l, vn_ref, lam_ref, g_ref, out_scale)


def attn_sample(q, k_new, v_new, cache_k, cache_v, page_table, lam, att_gain, out_scale):
    DB = q.shape[0]
    n_pages = page_table.shape[1]
    tok =pl.BlockSpec((1, H_ATT, HEAD_W), lambda b, pt: (b, 0, 0))
    vec = pl.BlockSpec((1, HEAD_W), lambda b, pt: (0, 0))
    page = lambda j: pl.BlockSpec((1, PAGE_SIZE, H_ATT, HEAD_W),
                                  lambda b, pt: (pt[b * n_pages + j], 0, 0, 0))
    pages = [page(j) for j in range(n_pages)]
    return pl.pallas_call(
        functools.partial(_attn_sample_kernel, n_pages=n_pages, out_scale=out_scale),
        grid_spec=pltpu.PrefetchScalarGridSpec(
            num_scalar_prefetch=1,
            grid=(DB,),
            in_specs=[vec, tok, tok, tok, vec] + pages + pages,
            out_specs=tok,
        ),
        out_shape=jax.ShapeDtypeStruct((DB, H_ATT, HEAD_W), F32),
        compiler_params=_cparams("arbitrary"),
    )(page_table.reshape(-1), lam, q, k_new, v_new, att_gain,
      *([cache_k] * n_pages), *([cache_v] * n_pages))


def _gates(zf, lb):
    sig = jax.nn.sigmoid(zf)
    f = lb + (1.0 - lb) * sig
    return jnp.log(f), (1.0 - lb) * (1.0 - sig)


def _hgrn_prompt_kernel(q_ref, f_ref, i_ref, g_ref, lb_ref, gain_ref, tril_ref, o_ref, st_ref, s_scr,
                        *, rows, chunk):
    t = pl.program_id(1)

    @pl.when(t == 0)
    def _():
        s_scr[...] = jnp.zeros_like(s_scr)

    tril = tril_ref[...]
    r = lax.broadcasted_iota(jnp.int32, (rows, rows), 0)
    c = lax.broadcasted_iota(jnp.int32, (rows, rows), 1)
    band = (c <= r) & (c // chunk == r // chunk)
    mid = chunk // 2 - 1
    n_chunks = rows // chunk

    heads = [slice(h * HEAD_W, (h + 1) * HEAD_W) for h in range(H_RNN)]
    chunks = [slice(c * chunk, (c + 1) * chunk) for c in range(n_chunks)]
    gates = [_gates(f_ref[:, sl], lb_ref[:, sl]) for sl in heads]
    cum = []
    for log_f, _ in gates:
        hi, md, lo = _split3(log_f)
        cum.append(_dot(tril, hi) + _dot(tril, md) + _dot(tril, lo))
    v = [i_ref[:, sl].astype(BF16) for sl in heads]
    q_mid, k_mid, q_dec, k_end, decay = ([[] for _ in heads] for _ in range(5))
    for h, sl in enumerate(heads):
        q, kk, bc = q_ref[:, sl], gates[h][1], cum[h]
        for c, rc in enumerate(chunks):
            bl = bc[rc]
            if c:
                bl = bl - bc[rc.start - 1:rc.start]
            b_mid, b_last = bl[mid:mid + 1], bl[chunk - 1:chunk]
            q_mid[h].append(q[rc] * jnp.exp(bl - b_mid))
            k_mid[h].append(kk[rc] * jnp.exp(b_mid - bl))
            q_dec[h].append((q[rc] * jnp.exp(bl)).astype(BF16))
            k_end[h].append((kk[rc] * jnp.exp(b_last - bl)).astype(BF16))
            decay[h].append(jnp.exp(b_last))
    a = [_dot_nt(jnp.concatenate(q_mid[h], axis=0).astype(BF16), jnp.concatenate(k_mid[h], axis=0).astype(BF16))
         for h in range(H_RNN)]
    intra = [_dot(jnp.where(band, a[h], 0.0).astype(BF16), v[h]) for h in range(H_RNN)]
    update = [[_dot_tn(v[h][rc], k_end[h][c]) for c, rc in enumerate(chunks)] for h in range(H_RNN)]
    st = [s_scr[h] for h in range(H_RNN)]
    for c, rc in enumerate(chunks):
        for h, sl in enumerate(heads):
            o = intra[h][rc] + _dot_nt(q_dec[h][c], st[h].astype(BF16))
            st[h] = st[h] * decay[h][c] + update[h][c]
            g = g_ref[rc, sl]
            o_ref[rc, sl] = (_rms(o, gain_ref[...]) * (g * jax.nn.sigmoid(g))).astype(BF16)
    for h in range(H_RNN):
        s_scr[h] = st[h]

    @pl.when(t == pl.num_programs(1) - 1)
    def _():
        for h in range(H_RNN):
            st_ref[0, h] = s_scr[h].T


def hgrn_prompt(z, lb, rnn_gain, B, S):
    rows = _tile(S, RNN_ROWS)
    chunk = _tile(rows, RNN_CHUNK)
    nt = S // rows
    tril = (jnp.arange(rows)[:, None] >= jnp.arange(rows)[None, :]).astype(BF16)
    col = lambda c: pl.BlockSpec((rows, D_HEADS), lambda b, t: (b * nt + t, c))
    return pl.pallas_call(
        functools.partial(_hgrn_prompt_kernel, rows=rows, chunk=chunk),
        grid=(B, nt),
        in_specs=[col(3), col(4), col(5), col(6),
                  pl.BlockSpec((1, D_HEADS), lambda b, t: (0, 0)),
                  pl.BlockSpec((1, HEAD_W), lambda b, t: (0, 0)),
                  pl.BlockSpec((rows, rows), lambda b, t: (0, 0))],
        out_specs=(pl.BlockSpec((rows, D_HEADS), lambda b, t: (b * nt + t, 0)),
                   pl.BlockSpec((1, H_RNN, HEAD_W, HEAD_W), lambda b, t: (b, 0, 0, 0))),
        out_shape=(jax.ShapeDtypeStruct((B * S, D_HEADS), BF16),
                   jax.ShapeDtypeStruct((B, H_RNN, HEAD_W, HEAD_W), F32)),
        scratch_shapes=[pltpu.VMEM((H_RNN, HEAD_W, HEAD_W), F32)],
        compiler_params=_cparams("parallel", "arbitrary"),
    )(z, z, z, z, lb, rnn_gain, tril)


def _hgrn_step_kernel(q_ref, f_ref, i_ref, g_ref, lb_ref, gain_ref, s0_ref, o_ref, s1_ref, *, bs):
    def column(rowvec):
        return jnp.broadcast_to(rowvec, (HEAD_W, HEAD_W)).T

    def head(h, _):
        sl = pl.ds(pl.multiple_of(h * HEAD_W, HEAD_W), HEAD_W)
        log_f, kk = _gates(f_ref[:, sl], lb_ref[:, sl])
        q, v, g = q_ref[:, sl], i_ref[:, sl], g_ref[:, sl]
        decay = jnp.exp(log_f)
        q_dec = q * decay
        a = jnp.sum(q_dec * (kk * jnp.exp(-log_f)), axis=-1, keepdims=True)
        inter = []
        for b in range(bs):
            s0 = s0_ref[b, h]
            q_rows = jnp.broadcast_to(q_dec[b:b + 1], (16, HEAD_W)).astype(BF16)
            inter.append(_dot(q_rows, s0.astype(BF16))[0:1])
            s1_ref[b, h] = column(decay[b:b + 1]) * s0 + column(kk[b:b + 1]) * v[b:b + 1]
        o = a * v + jnp.concatenate(inter, axis=0)
        o_ref[:, sl] = (_rms(o, gain_ref[...]) * (g * jax.nn.sigmoid(g))).astype(BF16)
        return 0

    lax.fori_loop(0, H_RNN, head, 0)


def hgrn_step(z, lb, rnn_gain, s0, bs_pref=16):
    DB = z.shape[0]
    bs = _tile(DB, bs_pref)
    col = lambda c: pl.BlockSpec((bs, D_HEADS), lambda i: (i, c))
    st = pl.BlockSpec((bs, H_RNN, HEAD_W, HEAD_W), lambda i: (i, 0, 0, 0))
    return pl.pallas_call(
        functools.partial(_hgrn_step_kernel, bs=bs),
        grid=(DB // bs,),
        in_specs=[col(3), col(4), col(5), col(6),
                  pl.BlockSpec((1, D_HEADS), lambda i: (0, 0)),
                  pl.BlockSpec((1, HEAD_W), lambda i: (0, 0)), st],
        out_specs=(pl.BlockSpec((bs, D_HEADS), lambda i: (i, 0)), st),
        out_shape=(jax.ShapeDtypeStruct((DB, D_HEADS), BF16), jax.ShapeDtypeStruct(s0.shape, F32)),
        compiler_params=_cparams("parallel"),
    )(z, z, z, z, lb, rnn_gain, s0)


def _out_proj_kernel(a_ref, r_ref, wa_ref, wr_ref, x_ref, g_ref, x1_ref, h1_ref):
    x1 = x_ref[...] + _dot(a_ref[...], wa_ref[...]) + _dot(r_ref[...], wr_ref[...])
    x1_ref[...] = x1
    h1_ref[...] = _rms(x1, g_ref[...]).astype(BF16)


def out_proj(o_att, o_rnn, w_out, x, ffn_gain, tm_pref=512):
    M, D = x.shape
    tm = _tile(M, tm_pref)
    act = pl.BlockSpec((tm, D_HEADS), lambda i: (i, 0))
    full = pl.BlockSpec((tm, D), lambda i: (i, 0))
    return pl.pallas_call(
        _out_proj_kernel,
        grid=(M // tm,),
        in_specs=[act, act,
                  pl.BlockSpec((D_HEADS, D), lambda i: (0, 0)),
                  pl.BlockSpec((D_HEADS, D), lambda i: (1, 0)),
                  full, pl.BlockSpec((1, D), lambda i: (0, 0))],
        out_specs=(full, full),
        out_shape=(jax.ShapeDtypeStruct((M, D), F32), jax.ShapeDtypeStruct((M, D), BF16)),
        compiler_params=_cparams("parallel"),
    )(o_att, o_rnn, w_out, w_out, x, ffn_gain)


def _conv_gate(g, u, first_tile, cw_ref, cb_ref, act_ref, tail_ref, carry_ref):
    @pl.when(first_tile)
    def _():
        carry_ref[...] = jnp.zeros_like(carry_ref)

    tm = g.shape[0]
    r = lax.broadcasted_iota(jnp.int32, g.shape, 0)
    prev2, prev1 = carry_ref[0:1], carry_ref[1:2]
    g1 = jnp.where(r == 0, prev1, pltpu.roll(g, 1, 0))
    g2 = jnp.where(r == 0, prev2, jnp.where(r == 1, prev1, pltpu.roll(g, 2, 0)))
    cw = cw_ref[...]
    c = cb_ref[...] + g2 * cw[0:1] + g1 * cw[1:2] + g * cw[2:3]
    act_ref[...] = (c * jax.nn.sigmoid(c) * u).astype(BF16)
    tail = g[tm - (CONV_W - 1):]
    carry_ref[0:CONV_W - 1] = tail
    tail_ref[0] = tail


def _ffn_seq_kernel(h_ref, wg_ref, wu_ref, cw_ref, cb_ref, act_ref, tail_ref, carry_ref, *, tiles_per_seq):
    h = h_ref[...]
    g = _dot(h, wg_ref[...])
    u = _dot(h, wu_ref[...])
    _conv_gate(g, u, pl.program_id(1) % tiles_per_seq == 0, cw_ref, cb_ref, act_ref, tail_ref, carry_ref)


def _ffn_decode_kernel(pt_ref, h_ref, wg_ref, wu_ref, cw_ref, cb_ref, lam_ref, q_ref, kn_ref, vn_ref, g_ref,
                       *refs, n_pages, tiles_m, tiles_per_seq, out_scale):
    del pt_ref
    k_refs, v_refs = refs[:n_pages], refs[n_pages:2 * n_pages]
    act_ref, tail_ref, o_ref, carry_ref = refs[2 * n_pages:]
    first_tile = (pl.program_id(0) % tiles_m) % tiles_per_seq == 0
    qm, own_head = _decode_query(q_ref)
    s = [_page_scores(qm, own_head, k_ref) for k_ref in k_refs]
    h = h_ref[...]
    g = _dot(h, wg_ref[...])
    u = _dot(h, wu_ref[...])
    p, p_new, l = _decode_weights(qm, s, kn_ref)
    values = [_page_values(pj, v_ref) for pj, v_ref in zip(p, v_refs)]
    o_ref[0] = _decode_finish(values, p_new, l, vn_ref, lam_ref, g_ref, out_scale)
    _conv_gate(g, u, first_tile, cw_ref, cb_ref, act_ref, tail_ref, carry_ref)


def ffn_seq_decode_tiles(M, S, F, DB, tm_pref=256, tn_pref=1408):
    tm, tn = _tile(S, tm_pref), _tile(F, tn_pref)
    return (tm, tn) if tn % HEAD_W == 0 and (M // tm) * (F // tn) == DB else None


def ffn_seq_decode(h, w_gate, w_up, conv_w, conv_b, B, S, tiles,
                   q, k_new, v_new, cache_k, cache_v, page_table, lam, att_gain, out_scale):
    M, D = h.shape
    F = w_gate.shape[1]
    DB, n_pages = page_table.shape
    tm, tn = tiles
    tiles_m, tiles_per_seq = M // tm, S // tm
    row = lambda i: i % tiles_m
    col = lambda i: i // tiles_m
    wspec = pl.BlockSpec((D, tn), lambda i, pt: (0, col(i)), pipeline_mode=pl.Buffered(1))
    tok = pl.BlockSpec((1, H_ATT, HEAD_W), lambda i, pt: (i, 0, 0))
    vec = pl.BlockSpec((1, HEAD_W), lambda i, pt: (0, 0))
    page = lambda j: pl.BlockSpec((1, PAGE_SIZE, H_ATT, HEAD_W),
                                  lambda i, pt: (pt[i * n_pages + j], 0, 0, 0))
    pages = [page(j) for j in range(n_pages)]
    return pl.pallas_call(
        functools.partial(_ffn_decode_kernel, n_pages=n_pages, tiles_m=tiles_m,
                          tiles_per_seq=tiles_per_seq, out_scale=out_scale),
        grid_spec=pltpu.PrefetchScalarGridSpec(
            num_scalar_prefetch=1,
            grid=(DB,),
            in_specs=[pl.BlockSpec((tm, D), lambda i, pt: (row(i), 0)), wspec, wspec,
                      pl.BlockSpec((CONV_W, tn), lambda i, pt: (0, col(i))),
                      pl.BlockSpec((1, tn), lambda i, pt: (0, col(i))),
                      vec, tok, tok, tok, vec] + pages + pages,
            out_specs=(pl.BlockSpec((tm, tn), lambda i, pt: (row(i), col(i))),
                       pl.BlockSpec((1, CONV_W - 1, tn), lambda i, pt: (row(i) // tiles_per_seq, 0, col(i))),
                       tok),
            scratch_shapes=[pltpu.VMEM((8, tn), F32)],
        ),
        out_shape=(jax.ShapeDtypeStruct((M, F), BF16), jax.ShapeDtypeStruct((B, CONV_W - 1, F), F32),
                   jax.ShapeDtypeStruct((DB, H_ATT, HEAD_W), F32)),
        compiler_params=_cparams("arbitrary"),
    )(page_table.reshape(-1), h, w_gate, w_up, conv_w, conv_b, lam, q, k_new, v_new, att_gain,
      *([cache_k] * n_pages), *([cache_v] * n_pages))


def ffn_seq(h, w_gate, w_up, conv_w, conv_b, B, S, tm_pref=512, tn_pref=1408):
    M, D = h.shape
    F = w_gate.shape[1]
    tm, tn = _tile(S, tm_pref), _tile(F, tn_pref)
    tiles_per_seq = S // tm
    wspec = pl.BlockSpec((D, tn), lambda n, m: (0, n))
    return pl.pallas_call(
        functools.partial(_ffn_seq_kernel, tiles_per_seq=tiles_per_seq),
        grid=(F // tn, M // tm),
        in_specs=[pl.BlockSpec((tm, D), lambda n, m: (m, 0)), wspec, wspec,
                  pl.BlockSpec((CONV_W, tn), lambda n, m: (0, n)),
                  pl.BlockSpec((1, tn), lambda n, m: (0, n))],
        out_specs=(pl.BlockSpec((tm, tn), lambda n, m: (m, n)),
                   pl.BlockSpec((1, CONV_W - 1, tn), lambda n, m: (m // tiles_per_seq, 0, n))),
        out_shape=(jax.ShapeDtypeStruct((M, F), BF16), jax.ShapeDtypeStruct((B, CONV_W - 1, F), F32)),
        scratch_shapes=[pltpu.VMEM((8, tn), F32)],
        compiler_params=_cparams("parallel", "arbitrary"),
    )(h, w_gate, w_up, conv_w, conv_b)


def _ffn_step_kernel(h_ref, wg_ref, wu_ref, cw_ref, cb_ref, p2_ref, p1_ref, act_ref, g_ref):
    h = h_ref[...]
    g = _dot(h, wg_ref[...])
    u = _dot(h, wu_ref[...])
    cw = cw_ref[...]
    c = cb_ref[...] + p2_ref[...] * cw[0:1] + p1_ref[...] * cw[1:2] + g * cw[2:3]
    act_ref[...] = (c * jax.nn.sigmoid(c) * u).astype(BF16)
    g_ref[...] = g


def ffn_step(h, w_gate, w_up, conv_w, conv_b, prev2, prev1, tn_pref=1408):
    M, D = h.shape
    F = w_gate.shape[1]
    tn = _tile(F, tn_pref)
    wspec = pl.BlockSpec((D, tn), lambda n: (0, n))
    tile = pl.BlockSpec((M, tn), lambda n: (0, n))
    return pl.pallas_call(
        _ffn_step_kernel,
        grid=(F // tn,),
        in_specs=[pl.BlockSpec((M, D), lambda n: (0, 0)), wspec, wspec,
                  pl.BlockSpec((CONV_W, tn), lambda n: (0, n)),
                  pl.BlockSpec((1, tn), lambda n: (0, n)), tile, tile],
        out_specs=(tile, tile),
        out_shape=(jax.ShapeDtypeStruct((M, F), BF16), jax.ShapeDtypeStruct((M, F), F32)),
        compiler_params=_cparams("parallel"),
    )(h, w_gate, w_up, conv_w, conv_b, prev2, prev1)


def _down_kernel(a_ref, w_ref, x_ref, o_ref, wb_ref=None):
    o_ref[...] = x_ref[...] + _dot(a_ref[...], _rounded(w_ref, wb_ref))


def ffn_down(act, w_down, x, tm_pref=1024, tn_pref=512):
    M, F = act.shape
    D = w_down.shape[1]
    tm, tn = _tile(M, tm_pref), _tile(D, tn_pref)
    tile = pl.BlockSpec((tm, tn), lambda i, j: (i, j))
    wmap = lambda i, j: (0, j)
    wb_spec, wb_shape = _weight_specs(w_down, (F, tn), wmap, M // tm)
    out = pl.pallas_call(
        _down_kernel,
        grid=(M // tm, D // tn),
        in_specs=[pl.BlockSpec((tm, F), lambda i, j: (i, 0)),
                  pl.BlockSpec((F, tn), wmap), tile],
        out_specs=(tile,) + wb_spec,
        out_shape=(jax.ShapeDtypeStruct((M, D), F32),) + wb_shape,
        compiler_params=_cparams("parallel", "arbitrary"),
    )(act, w_down, x)
    return out if wb_shape else out[0]


def _ple_kernel(x_ref, g_ref, wg_ref, p_ref, wp_ref, o_ref, h_ref, *, tn):
    j = pl.program_id(1)

    @pl.when(j == 0)
    def _():
        h_ref[...] = _rms(x_ref[...], g_ref[...]).astype(BF16)

    gate = jax.nn.sigmoid(_dot(h_ref[...], wg_ref[...]))
    emb = _dot(p_ref[...].astype(BF16), wp_ref[...])
    o_ref[...] = x_ref[:, pl.ds(pl.multiple_of(j * tn, tn), tn)] + gate * emb


def ple(x, gain, w_gate, p, w_proj, tm_pref=512, tn_pref=2048):
    M, D = x.shape
    DP = p.shape[1]
    tm, tn = _tile(M, tm_pref), _tile(D, tn_pref)
    return pl.pallas_call(
        functools.partial(_ple_kernel, tn=tn),
        grid=(M // tm, D // tn),
        in_specs=[pl.BlockSpec((tm, D), lambda i, j: (i, 0)),
                  pl.BlockSpec((1, D), lambda i, j: (0, 0)),
                  pl.BlockSpec((D, tn), lambda i, j: (0, j)),
                  pl.BlockSpec((tm, DP), lambda i, j: (i, 0)),
                  pl.BlockSpec((DP, tn), lambda i, j: (0, j))],
        out_specs=pl.BlockSpec((tm, tn), lambda i, j: (i, j)),
        out_shape=jax.ShapeDtypeStruct((M, D), F32),
        scratch_shapes=[pltpu.VMEM((tm, D), BF16)],
        compiler_params=_cparams("parallel", "arbitrary"),
    )(x, gain, w_gate, p, w_proj)


def _rope_tables(pos):
    half = DK_ATT // 2
    inv_freq = 1.0 / (ROPE_THETA ** (jnp.arange(half, dtype=F32) * (2.0 / DK_ATT)))
    ang = pos.astype(F32)[:, None] * inv_freq[None, :]
    cos, sin = jnp.cos(ang), jnp.sin(ang)
    return (jnp.concatenate([cos] * 4, axis=-1), jnp.concatenate([-sin, sin] * 2, axis=-1))


def _row(v):
    return v.reshape(1, -1).astype(F32)


def kernel(x_prompt, x_sample, p_prompt, p_sample, cache_k, cache_v, state_rnn, state_ffn_conv, page_table,
           norm_mix, w_in, q_norm, k_norm, lam_q1, lam_k1, lam_q2, lam_k2, att_out_norm, rnn_out_norm,
           lower_bounds, w_out, norm_ffn, w_gate, w_up, conv_w, conv_b, w_down, norm_ple, w_ple_gate,
           w_ple_proj):
    B, S, D = x_prompt.shape
    DB, T, _ = x_sample.shape
    assert T == 1, "the sample group is implemented for one new token per sequence"
    depth = w_in.shape[0]
    past = page_table.shape[1] * PAGE_SIZE
    cos_p, sin_p = _rope_tables(jnp.arange(S, dtype=jnp.int32))
    cos_p, sin_p = jnp.tile(cos_p, (B, 1)), jnp.tile(sin_p, (B, 1))
    cos_s, sin_s = _rope_tables(jnp.full((DB,), past, dtype=jnp.int32))

    yp = x_prompt.reshape(B * S, D)
    ys = x_sample.reshape(DB, D)
    outs = [[] for _ in range(8)]
    for i in range(depth):
        lam_init = 0.8 - 0.6 * math.exp(-0.3 * i)
        out_scale = 1.0 - lam_init
        lam, lb = layer_params(jnp.stack([lam_q1[i], lam_k1[i], lam_q2[i], lam_k2[i]]).astype(F32),
                               lower_bounds.astype(F32), i, lam_init)
        w_out_b, w_gate_b, w_up_b = w_out[i].astype(BF16), w_gate[i].astype(BF16), w_up[i].astype(BF16)
        w_pg_b, w_pp_b = w_ple_gate[i].astype(BF16), w_ple_proj[i].astype(BF16)
        q_gain = jnp.tile(_row(q_norm[i]), (1, 2))
        k_gain = jnp.tile(_row(k_norm[i]), (1, 2))
        att_gain, rnn_gain = _row(att_out_norm[i]), _row(rnn_out_norm[i])

        def mix_out(x, o_att, o_rnn):
            return out_proj(o_att, o_rnn, w_out_b, x, _row(norm_ffn[i]))

        def embed(x, p):
            return ple(x, _row(norm_ple[i]), w_pg_b, p, w_pp_b)

        z_s, w_in_b = norm_matmul(ys, _row(norm_mix[i]), w_in[i])
        q_s, k_s, _, v_s, _ = qk_prep(z_s, cos_s, sin_s, q_gain, k_gain)

        z = norm_matmul(yp, _row(norm_mix[i]), w_in_b)
        qt, k, kb, v, vt = qk_prep(z, cos_p, sin_p, q_gain, k_gain, (B, S))
        o_att = attn_prompt(qt, kb.reshape(B, S, -1), vt, lam, att_gain, out_scale).reshape(B * S, -1)
        o_rnn, rnn_p = hgrn_prompt(z, lb, rnn_gain, B, S)
        x1, h1 = mix_out(yp, o_att, o_rnn)
        outs[0].append(k.reshape(B, S // PAGE_SIZE, PAGE_SIZE, H_ATT, HEAD_W))
        outs[1].append(v.reshape(B, S // PAGE_SIZE, PAGE_SIZE, H_ATT, HEAD_W))
        outs[2].append(rnn_p.astype(state_rnn.dtype))

        per_head = lambda a: a.reshape(DB, H_ATT, HEAD_W)
        decode_args = (per_head(q_s), per_head(k_s), per_head(v_s), cache_k[i], cache_v[i], page_table,
                       lam, att_gain, out_scale)
        ffn_args = (h1, w_gate_b, w_up_b, conv_w[i], _row(conv_b[i]), B, S)
        tiles = ffn_seq_decode_tiles(B * S, S, w_gate_b.shape[1], DB)
        if tiles is not None:
            act, conv_p, o_att_s = ffn_seq_decode(*ffn_args, tiles, *decode_args)
        else:
            act, conv_p = ffn_seq(*ffn_args)
            o_att_s = attn_sample(*decode_args)
        outs[3].append(conv_p)
        o_rnn_s, rnn_s = hgrn_step(z_s, lb, rnn_gain, state_rnn[i].astype(F32))
        x1_s, h1_s = mix_out(ys, o_att_s.reshape(DB, D_HEADS).astype(BF16), o_rnn_s)
        buf = state_ffn_conv[i]
        act_s, g_new = ffn_step(h1_s, w_gate_b, w_up_b, conv_w[i], _row(conv_b[i]), buf[:, 0], buf[:, 1])
        x2_s, w_down_b = ffn_down(act_s, w_down[i], x1_s)
        ys = embed(x2_s, p_sample[i].reshape(DB, -1))
        yp = embed(ffn_down(act, w_down_b, x1), p_prompt[i].reshape(B * S, -1))
        outs[4].append(k_s.reshape(DB, T, H_ATT, HEAD_W))
        outs[5].append(v_s.reshape(DB, T, H_ATT, HEAD_W))
        outs[6].append(rnn_s.astype(state_rnn.dtype))
        outs[7].append(jnp.stack([buf[:, 1], g_new], axis=1))

    return (yp.reshape(B, S, D), ys.reshape(DB, T, D)) + tuple(jnp.stack(o) for o in outs)
```

```python
import functools
import math

import jax
import jax.numpy as jnp
from jax import lax
from jax.experimental import pallas as pl
from jax.experimental.pallas import tpu as pltpu

H_ATT = 8
DK_ATT = 64
H_RNN = 8
HEAD_W = 128
D_HEADS = H_ATT * HEAD_W
PAGE_SIZE = 128
CONV_W = 3
ROPE_THETA = 10000.0
EPS = 1e-6
NEG_INF = -1e30
RNN_CHUNK = 32
RNN_ROWS = 256
ATT_HEADS_PER_STEP = 4
VT_ROWS = HEAD_W + 16
Q_SCALE = DK_ATT ** -0.5 * math.log2(math.e)
VMEM_LIMIT_V7X = 56 * 1024 * 1024

F32 = jnp.float32
BF16 = jnp.bfloat16


def _cparams(*sem):
    return pltpu.CompilerParams(dimension_semantics=sem, vmem_limit_bytes=VMEM_LIMIT_V7X)


def _tile(n, pref):
    t = min(n, pref)
    while n % t:
        t //= 2
    return t


def _split3(x):
    hi = x.astype(BF16)
    r1 = x - hi.astype(F32)
    mid = r1.astype(BF16)
    lo = (r1 - mid.astype(F32)).astype(BF16)
    return hi, mid, lo


def _dot(a, b):
    return jnp.dot(a, b, preferred_element_type=F32)


def _dot_nt(a, b):
    return lax.dot_general(a, b, (((1,), (1,)), ((), ())), preferred_element_type=F32)


def _dot_tn(a, b):
    return lax.dot_general(a, b, (((0,), (0,)), ((), ())), preferred_element_type=F32)


def _rms(x, gain):
    return x * lax.rsqrt(jnp.mean(x * x, axis=-1, keepdims=True) + EPS) * gain


def _params_kernel(lam_ref, lb_in_ref, lam_out_ref, lb_out_ref, *, layer, lam_init):
    v = lam_ref[...]
    s1 = jnp.sum(v[0:1] * v[1:2], axis=-1, keepdims=True)
    s2 = jnp.sum(v[2:3] * v[3:4], axis=-1, keepdims=True)
    lam = jnp.exp(s1) - jnp.exp(s2) + lam_init
    lam_out_ref[...] = jnp.broadcast_to(lam, lam_out_ref.shape)
    lb = lb_in_ref[...]
    e = jnp.exp(lb - jnp.max(lb, axis=0, keepdims=True))
    sm = e / jnp.sum(e, axis=0, keepdims=True)
    lb_out_ref[...] = jnp.sum(sm[0:layer + 1], axis=0, keepdims=True)


def layer_params(lam_vecs, lower_bounds, layer, lam_init):
    return pl.pallas_call(
        functools.partial(_params_kernel, layer=layer, lam_init=lam_init),
        out_shape=(jax.ShapeDtypeStruct((1, HEAD_W), F32),
                   jax.ShapeDtypeStruct((1, lower_bounds.shape[1]), F32)),
    )(lam_vecs, lower_bounds)


def _rounded(w_ref, wb_ref):
    if wb_ref is None:
        return w_ref[...]
    wb_ref[...] = w_ref[...].astype(BF16)
    return wb_ref[...]


def _weight_specs(w, block, index_map, row_tiles):
    if w.dtype == BF16:
        return (), ()
    assert row_tiles == 1
    return (pl.BlockSpec(block, index_map),), (jax.ShapeDtypeStruct(w.shape, BF16),)


def _norm_matmul_kernel(x_ref, g_ref, w_ref, o_ref, *rest):
    wb_ref, h_ref = (rest[0], rest[1]) if len(rest) == 2 else (None, rest[0])

    @pl.when(pl.program_id(1) == 0)
    def _():
        h_ref[...] = _rms(x_ref[...], g_ref[...]).astype(BF16)

    o_ref[...] = _dot(h_ref[...], _rounded(w_ref, wb_ref))


def norm_matmul(x, gain, w, tm_pref=1024, tn_pref=1024):
    M, K = x.shape
    N = w.shape[1]
    tm, tn = _tile(M, tm_pref), _tile(N, tn_pref)
    wmap = lambda i, j: (0, j)
    wb_spec, wb_shape = _weight_specs(w, (K, tn), wmap, M // tm)
    out = pl.pallas_call(
        _norm_matmul_kernel,
        grid=(M // tm, N // tn),
        in_specs=[pl.BlockSpec((tm, K), lambda i, j: (i, 0)),
                  pl.BlockSpec((1, K), lambda i, j: (0, 0)),
                  pl.BlockSpec((K, tn), wmap)],
        out_specs=(pl.BlockSpec((tm, tn), lambda i, j: (i, j)),) + wb_spec,
        out_shape=(jax.ShapeDtypeStruct((M, N), F32),) + wb_shape,
        scratch_shapes=[pltpu.VMEM((tm, K), BF16)],
        compiler_params=_cparams("parallel", "arbitrary"),
    )(x, gain, w)
    return out if wb_shape else out[0]


def _qk_prep_kernel(zq_ref, zk_ref, zv_ref, cos_ref, sin_ref, qg_ref, kg_ref, grp_ref,
                    q_ref, k_ref, kb_ref, v_ref, vb_ref, *, transposed):
    cos = cos_ref[...]
    sin = sin_ref[...]
    lane = lax.broadcasted_iota(jnp.int32, cos.shape, 1)
    first_half = (lane % DK_ATT) < (DK_ATT // 2)
    grp = grp_ref[...]

    def norm_rope(x, gain):
        hi, mid, lo = _split3(x * x)
        ss = (_dot(hi, grp) + _dot(mid, grp) + _dot(lo, grp)) * (1.0 / DK_ATT)
        y = x * lax.rsqrt(ss + EPS) * gain
        swapped = jnp.where(first_half, pltpu.roll(y, HEAD_W - DK_ATT // 2, 1),
                            pltpu.roll(y, DK_ATT // 2, 1))
        return y * cos + swapped * sin

    for h in range(H_ATT):
        sl = slice(h * HEAD_W, (h + 1) * HEAD_W)
        q = norm_rope(zq_ref[:, sl], qg_ref[...]) * Q_SCALE
        k = norm_rope(zk_ref[:, sl], kg_ref[...])
        k_ref[:, sl] = k
        kb_ref[:, sl] = k.astype(BF16)
        v = zv_ref[:, sl]
        v_ref[:, sl] = v
        if transposed:
            q_ref[0, sl, :] = q.T.astype(BF16)
            r0 = h * VT_ROWS
            vb_ref[0, r0:r0 + HEAD_W, :] = v.T.astype(BF16)
            pad = (VT_ROWS - HEAD_W, v.shape[0])
            vb_ref[0, r0 + HEAD_W:r0 + VT_ROWS, :] = (lax.broadcasted_iota(jnp.int32, pad, 0) == 0).astype(BF16)
        else:
            q_ref[:, sl] = q.astype(BF16)
            vb_ref[:, sl] = v.astype(BF16)


def qk_prep(z, cos, sin, q_gain, k_gain, seqs=None, tm_pref=512):
    M = z.shape[0]
    tm = _tile(M if seqs is None else seqs[1], tm_pref)
    lane = jnp.arange(HEAD_W)
    grp = (lane[:, None] // DK_ATT == lane[None, :] // DK_ATT).astype(BF16)
    col = lambda c: pl.BlockSpec((tm, D_HEADS), lambda i: (i, c))
    row = pl.BlockSpec((tm, HEAD_W), lambda i: (i, 0))
    vec = pl.BlockSpec((1, HEAD_W), lambda i: (0, 0))
    out = pl.BlockSpec((tm, D_HEADS), lambda i: (i, 0))
    sd = lambda dt: jax.ShapeDtypeStruct((M, D_HEADS), dt)
    if seqs is None:
        out_q, out_v, sd_q, sd_v = out, out, sd(BF16), sd(BF16)
    else:
        B, S = seqs
        tiles = S // tm
        out_q = pl.BlockSpec((1, D_HEADS, tm), lambda i: (i // tiles, 0, i % tiles))
        out_v = pl.BlockSpec((1, H_ATT * VT_ROWS, tm), lambda i: (i // tiles, 0, i % tiles))
        sd_q = jax.ShapeDtypeStruct((B, D_HEADS, S), BF16)
        sd_v = jax.ShapeDtypeStruct((B, H_ATT * VT_ROWS, S), BF16)
    return pl.pallas_call(
        functools.partial(_qk_prep_kernel, transposed=seqs is not None),
        grid=(M // tm,),
        in_specs=[col(0), col(1), col(2), row, row, vec, vec,
                  pl.BlockSpec((HEAD_W, HEAD_W), lambda i: (0, 0))],
        out_specs=(out_q, out, out, out, out_v),
        out_shape=(sd_q, sd(F32), sd(BF16), sd(F32), sd_v),
        compiler_params=_cparams("parallel"),
    )(z, z, z, cos, sin, q_gain, k_gain, grp)


def _attn_prompt_kernel(lam_ref, qt_ref, k_ref, vt_ref, g_ref, o_ref, m_scr, acc_scr, *, tq, tk, out_scale):
    qi = pl.program_id(2)
    n_full = (qi * tq) // tk
    dim = lax.broadcasted_iota(jnp.int32, (HEAD_W, tq), 0)
    q2t = []
    for u in range(ATT_HEADS_PER_STEP):
        qt = qt_ref[0, u * HEAD_W:(u + 1) * HEAD_W, :]
        zero = jnp.zeros_like(qt)
        q2t.append(jnp.concatenate([jnp.where(dim < DK_ATT, qt, zero),
                                    jnp.where(dim >= DK_ATT, qt, zero)], axis=1))
    m_scr[...] = jnp.full(m_scr.shape, NEG_INF, F32)
    acc_scr[...] = jnp.zeros(acc_scr.shape, F32)

    def blocks(kbs, masked):
        units = [(pl.multiple_of(kb * tk, tk), u) for kb in kbs for u in range(ATT_HEADS_PER_STEP)]

        def scores(start, u):
            return _dot(k_ref[0, pl.ds(start, tk), u * HEAD_W:(u + 1) * HEAD_W], q2t[u])

        s_next = scores(*units[0])
        for i, (start, u) in enumerate(units):
            s = s_next
            if i + 1 < len(units):
                s_next = scores(*units[i + 1])
            if masked:
                key = start + lax.broadcasted_iota(jnp.int32, s.shape, 0)
                qry = qi * tq + lax.broadcasted_iota(jnp.int32, s.shape, 1) % tq
                s = jnp.where(key <= qry, s, NEG_INF)
            m_prev = m_scr[u]
            m_new = jnp.maximum(m_prev, jnp.max(s, axis=0, keepdims=True))
            m_scr[u] = m_new
            p = jnp.exp2(s - m_new).astype(BF16)
            vt = vt_ref[0, u * VT_ROWS:(u + 1) * VT_ROWS, pl.ds(start, tk)]
            acc_scr[u] = jnp.exp2(m_prev - m_new) * acc_scr[u] + _dot(vt, p)

    def pair(i, carry):
        blocks([2 * i, 2 * i + 1], False)
        return carry

    lax.fori_loop(0, n_full // 2, pair, 0)

    @pl.when(n_full % 2 == 1)
    def _():
        blocks([n_full - 1], False)

    blocks([n_full], True)
    for u in range(ATT_HEADS_PER_STEP):
        acc = acc_scr[u]
        o = acc[:HEAD_W] / acc[HEAD_W:HEAD_W + 1]
        od = o[:, :tq] - lam_ref[:, :1] * o[:, tq:]
        y = od * lax.rsqrt(jnp.mean(od * od, axis=0, keepdims=True) + EPS) * g_ref[...] * out_scale
        o_ref[0, :, u * HEAD_W:(u + 1) * HEAD_W] = y.T.astype(BF16)


def attn_prompt(qt, k, vt, lam, att_gain, out_scale, tq_pref=256, tk_pref=512):
    B, S, _ = k.shape
    tq = _tile(S, tq_pref)
    tk = max(tq, _tile(S, tk_pref))
    U = ATT_HEADS_PER_STEP
    return pl.pallas_call(
        functools.partial(_attn_prompt_kernel, tq=tq, tk=tk, out_scale=out_scale),
        grid=(B, H_ATT // U, S // tq),
        in_specs=[pl.BlockSpec((1, HEAD_W), lambda b, h, i: (0, 0)),
                  pl.BlockSpec((1, U * HEAD_W, tq), lambda b, h, i: (b, h, i)),
                  pl.BlockSpec((1, S, U * HEAD_W), lambda b, h, i: (b, 0, h)),
                  pl.BlockSpec((1, U * VT_ROWS, S), lambda b, h, i: (b, h, 0)),
                  pl.BlockSpec((HEAD_W, 1), lambda b, h, i: (0, 0))],
        out_specs=pl.BlockSpec((1, tq, U * HEAD_W), lambda b, h, i: (b, i, h)),
        out_shape=jax.ShapeDtypeStruct(k.shape, BF16),
        scratch_shapes=[pltpu.VMEM((U, 1, 2 * tq), F32), pltpu.VMEM((U, VT_ROWS, 2 * tq), F32)],
        compiler_params=_cparams("parallel", "parallel", "arbitrary"),
    )(lam, qt, k, vt, att_gain.reshape(HEAD_W, 1))


PAGE_KEYS = PAGE_SIZE * H_ATT


def _decode_query(q_ref):
    rows = 2 * H_ATT
    q = q_ref[0]
    q16 = jnp.concatenate([q, q], axis=0)
    r = lax.broadcasted_iota(jnp.int32, (rows, HEAD_W), 0)
    c = lax.broadcasted_iota(jnp.int32, (rows, HEAD_W), 1)
    qm = jnp.where(c // DK_ATT == r // H_ATT, q16, jnp.zeros_like(q16))
    own_head = (lax.broadcasted_iota(jnp.int32, (rows, PAGE_KEYS), 1) % H_ATT
                == lax.broadcasted_iota(jnp.int32, (rows, PAGE_KEYS), 0) % H_ATT)
    return qm, own_head


def _page_scores(qm, own_head, k_ref):
    return jnp.where(own_head, _dot_nt(qm, k_ref[0].reshape(PAGE_KEYS, HEAD_W).astype(BF16)), NEG_INF)


def _decode_weights(qm, s, kn_ref):
    k_new = jnp.concatenate([kn_ref[0], kn_ref[0]], axis=0)
    s_new = jnp.sum(qm.astype(F32) * k_new, axis=-1, keepdims=True)
    m = jnp.maximum(jnp.max(functools.reduce(jnp.maximum, s), axis=-1, keepdims=True), s_new)
    p = [jnp.exp2(sj - m) for sj in s]
    p_new = jnp.exp2(s_new - m)
    l = jnp.sum(functools.reduce(jnp.add, p), axis=-1, keepdims=True) + p_new
    return p, p_new, l


def _page_values(p, v_ref):
    return _dot(p.astype(BF16), v_ref[0].reshape(PAGE_KEYS, HEAD_W).astype(BF16))


def _decode_finish(values, p_new, l, vn_ref, lam_ref, g_ref, out_scale):
    v_new = jnp.concatenate([vn_ref[0], vn_ref[0]], axis=0)
    o = (functools.reduce(jnp.add, values) + p_new * v_new) / l
    od = o[:H_ATT] - lam_ref[...] * o[H_ATT:]
    return _rms(od, g_ref[...]) * out_scale


def _attn_sample_kernel(pt_ref, lam_ref, q_ref, kn_ref, vn_ref, g_ref, *refs, n_pages, out_scale):
    del pt_ref
    k_refs, v_refs, o_ref = refs[:n_pages], refs[n_pages:2 * n_pages], refs[2 * n_pages]
    qm, own_head = _decode_query(q_ref)
    p, p_new, l = _decode_weights(qm, [_page_scores(qm, own_head, k_ref) for k_ref in k_refs], kn_ref)
    values = [_page_values(pj, v_ref) for pj, v_ref in zip(p, v_refs)]
    o_ref[0] = _decode_finish(values, p_new, l, vn_ref, lam_ref, g_ref, out_scale)


def attn_sample(q, k_new, v_new, cache_k, cache_v, page_table, lam, att_gain, out_scale):
    DB = q.shape[0]
    n_pages = page_table.shape[1]
    tok =pl.BlockSpec((1, H_ATT, HEAD_W), lambda b, pt: (b, 0, 0))
    vec = pl.BlockSpec((1, HEAD_W), lambda b, pt: (0, 0))
    page = lambda j: pl.BlockSpec((1, PAGE_SIZE, H_ATT, HEAD_W),
                                  lambda b, pt: (pt[b * n_pages + j], 0, 0, 0))
    pages = [page(j) for j in range(n_pages)]
    return pl.pallas_call(
        functools.partial(_attn_sample_kernel, n_pages=n_pages, out_scale=out_scale),
        grid_spec=pltpu.PrefetchScalarGridSpec(
            num_scalar_prefetch=1,
            grid=(DB,),
            in_specs=[vec, tok, tok, tok, vec] + pages + pages,
            out_specs=tok,
        ),
        out_shape=jax.ShapeDtypeStruct((DB, H_ATT, HEAD_W), F32),
        compiler_params=_cparams("arbitrary"),
    )(page_table.reshape(-1), lam, q, k_new, v_new, att_gain,
      *([cache_k] * n_pages), *([cache_v] * n_pages))


def _gates(zf, lb):
    sig = jax.nn.sigmoid(zf)
    f = lb + (1.0 - lb) * sig
    return jnp.log(f), (1.0 - lb) * (1.0 - sig)


def _hgrn_prompt_kernel(q_ref, f_ref, i_ref, g_ref, lb_ref, gain_ref, tril_ref, o_ref, st_ref, s_scr,
                        *, rows, chunk):
    t = pl.program_id(1)

    @pl.when(t == 0)
    def _():
        s_scr[...] = jnp.zeros_like(s_scr)

    tril = tril_ref[...]
    r = lax.broadcasted_iota(jnp.int32, (rows, rows), 0)
    c = lax.broadcasted_iota(jnp.int32, (rows, rows), 1)
    band = (c <= r) & (c // chunk == r // chunk)
    mid = chunk // 2 - 1
    n_chunks = rows // chunk

    heads = [slice(h * HEAD_W, (h + 1) * HEAD_W) for h in range(H_RNN)]
    chunks = [slice(c * chunk, (c + 1) * chunk) for c in range(n_chunks)]
    gates = [_gates(f_ref[:, sl], lb_ref[:, sl]) for sl in heads]
    cum = []
    for log_f, _ in gates:
        hi, md, lo = _split3(log_f)
        cum.append(_dot(tril, hi) + _dot(tril, md) + _dot(tril, lo))
    v = [i_ref[:, sl].astype(BF16) for sl in heads]
    q_mid, k_mid, q_dec, k_end, decay = ([[] for _ in heads] for _ in range(5))
    for h, sl in enumerate(heads):
        q, kk, bc = q_ref[:, sl], gates[h][1], cum[h]
        for c, rc in enumerate(chunks):
            bl = bc[rc]
            if c:
                bl = bl - bc[rc.start - 1:rc.start]
            b_mid, b_last = bl[mid:mid + 1], bl[chunk - 1:chunk]
            q_mid[h].append(q[rc] * jnp.exp(bl - b_mid))
            k_mid[h].append(kk[rc] * jnp.exp(b_mid - bl))
            q_dec[h].append((q[rc] * jnp.exp(bl)).astype(BF16))
            k_end[h].append((kk[rc] * jnp.exp(b_last - bl)).astype(BF16))
            decay[h].append(jnp.exp(b_last))
    a = [_dot_nt(jnp.concatenate(q_mid[h], axis=0).astype(BF16), jnp.concatenate(k_mid[h], axis=0).astype(BF16))
         for h in range(H_RNN)]
    intra = [_dot(jnp.where(band, a[h], 0.0).astype(BF16), v[h]) for h in range(H_RNN)]
    update = [[_dot_tn(v[h][rc], k_end[h][c]) for c, rc in enumerate(chunks)] for h in range(H_RNN)]
    st = [s_scr[h] for h in range(H_RNN)]
    for c, rc in enumerate(chunks):
        for h, sl in enumerate(heads):
            o = intra[h][rc] + _dot_nt(q_dec[h][c], st[h].astype(BF16))
            st[h] = st[h] * decay[h][c] + update[h][c]
            g = g_ref[rc, sl]
            o_ref[rc, sl] = (_rms(o, gain_ref[...]) * (g * jax.nn.sigmoid(g))).astype(BF16)
    for h in range(H_RNN):
        s_scr[h] = st[h]

    @pl.when(t == pl.num_programs(1) - 1)
    def _():
        for h in range(H_RNN):
            st_ref[0, h] = s_scr[h].T


def hgrn_prompt(z, lb, rnn_gain, B, S):
    rows = _tile(S, RNN_ROWS)
    chunk = _tile(rows, RNN_CHUNK)
    nt = S // rows
    tril = (jnp.arange(rows)[:, None] >= jnp.arange(rows)[None, :]).astype(BF16)
    col = lambda c: pl.BlockSpec((rows, D_HEADS), lambda b, t: (b * nt + t, c))
    return pl.pallas_call(
        functools.partial(_hgrn_prompt_kernel, rows=rows, chunk=chunk),
        grid=(B, nt),
        in_specs=[col(3), col(4), col(5), col(6),
                  pl.BlockSpec((1, D_HEADS), lambda b, t: (0, 0)),
                  pl.BlockSpec((1, HEAD_W), lambda b, t: (0, 0)),
                  pl.BlockSpec((rows, rows), lambda b, t: (0, 0))],
        out_specs=(pl.BlockSpec((rows, D_HEADS), lambda b, t: (b * nt + t, 0)),
                   pl.BlockSpec((1, H_RNN, HEAD_W, HEAD_W), lambda b, t: (b, 0, 0, 0))),
        out_shape=(jax.ShapeDtypeStruct((B * S, D_HEADS), BF16),
                   jax.ShapeDtypeStruct((B, H_RNN, HEAD_W, HEAD_W), F32)),
        scratch_shapes=[pltpu.VMEM((H_RNN, HEAD_W, HEAD_W), F32)],
        compiler_params=_cparams("parallel", "arbitrary"),
    )(z, z, z, z, lb, rnn_gain, tril)


def _hgrn_step_kernel(q_ref, f_ref, i_ref, g_ref, lb_ref, gain_ref, s0_ref, o_ref, s1_ref, *, bs):
    def column(rowvec):
        return jnp.broadcast_to(rowvec, (HEAD_W, HEAD_W)).T

    def head(h, _):
        sl = pl.ds(pl.multiple_of(h * HEAD_W, HEAD_W), HEAD_W)
        log_f, kk = _gates(f_ref[:, sl], lb_ref[:, sl])
        q, v, g = q_ref[:, sl], i_ref[:, sl], g_ref[:, sl]
        decay = jnp.exp(log_f)
        q_dec = q * decay
        a = jnp.sum(q_dec * (kk * jnp.exp(-log_f)), axis=-1, keepdims=True)
        inter = []
        for b in range(bs):
            s0 = s0_ref[b, h]
            q_rows = jnp.broadcast_to(q_dec[b:b + 1], (16, HEAD_W)).astype(BF16)
            inter.append(_dot(q_rows, s0.astype(BF16))[0:1])
            s1_ref[b, h] = column(decay[b:b + 1]) * s0 + column(kk[b:b + 1]) * v[b:b + 1]
        o = a * v + jnp.concatenate(inter, axis=0)
        o_ref[:, sl] = (_rms(o, gain_ref[...]) * (g * jax.nn.sigmoid(g))).astype(BF16)
        return 0

    lax.fori_loop(0, H_RNN, head, 0)


def hgrn_step(z, lb, rnn_gain, s0, bs_pref=16):
    DB = z.shape[0]
    bs = _tile(DB, bs_pref)
    col = lambda c: pl.BlockSpec((bs, D_HEADS), lambda i: (i, c))
    st = pl.BlockSpec((bs, H_RNN, HEAD_W, HEAD_W), lambda i: (i, 0, 0, 0))
    return pl.pallas_call(
        functools.partial(_hgrn_step_kernel, bs=bs),
        grid=(DB // bs,),
        in_specs=[col(3), col(4), col(5), col(6),
                  pl.BlockSpec((1, D_HEADS), lambda i: (0, 0)),
                  pl.BlockSpec((1, HEAD_W), lambda i: (0, 0)), st],
        out_specs=(pl.BlockSpec((bs, D_HEADS), lambda i: (i, 0)), st),
        out_shape=(jax.ShapeDtypeStruct((DB, D_HEADS), BF16), jax.ShapeDtypeStruct(s0.shape, F32)),
        compiler_params=_cparams("parallel"),
    )(z, z, z, z, lb, rnn_gain, s0)


def _out_proj_kernel(a_ref, r_ref, wa_ref, wr_ref, x_ref, g_ref, x1_ref, h1_ref):
    x1 = x_ref[...] + _dot(a_ref[...], wa_ref[...]) + _dot(r_ref[...], wr_ref[...])
    x1_ref[...] = x1
    h1_ref[...] = _rms(x1, g_ref[...]).astype(BF16)


def out_proj(o_att, o_rnn, w_out, x, ffn_gain, tm_pref=512):
    M, D = x.shape
    tm = _tile(M, tm_pref)
    act = pl.BlockSpec((tm, D_HEADS), lambda i: (i, 0))
    full = pl.BlockSpec((tm, D), lambda i: (i, 0))
    return pl.pallas_call(
        _out_proj_kernel,
        grid=(M // tm,),
        in_specs=[act, act,
                  pl.BlockSpec((D_HEADS, D), lambda i: (0, 0)),
                  pl.BlockSpec((D_HEADS, D), lambda i: (1, 0)),
                  full, pl.BlockSpec((1, D), lambda i: (0, 0))],
        out_specs=(full, full),
        out_shape=(jax.ShapeDtypeStruct((M, D), F32), jax.ShapeDtypeStruct((M, D), BF16)),
        compiler_params=_cparams("parallel"),
    )(o_att, o_rnn, w_out, w_out, x, ffn_gain)


def _conv_gate(g, u, first_tile, cw_ref, cb_ref, act_ref, tail_ref, carry_ref):
    @pl.when(first_tile)
    def _():
        carry_ref[...] = jnp.zeros_like(carry_ref)

    tm = g.shape[0]
    r = lax.broadcasted_iota(jnp.int32, g.shape, 0)
    prev2, prev1 = carry_ref[0:1], carry_ref[1:2]
    g1 = jnp.where(r == 0, prev1, pltpu.roll(g, 1, 0))
    g2 = jnp.where(r == 0, prev2, jnp.where(r == 1, prev1, pltpu.roll(g, 2, 0)))
    cw = cw_ref[...]
    c = cb_ref[...] + g2 * cw[0:1] + g1 * cw[1:2] + g * cw[2:3]
    act_ref[...] = (c * jax.nn.sigmoid(c) * u).astype(BF16)
    tail = g[tm - (CONV_W - 1):]
    carry_ref[0:CONV_W - 1] = tail
    tail_ref[0] = tail


def _ffn_seq_kernel(h_ref, wg_ref, wu_ref, cw_ref, cb_ref, act_ref, tail_ref, carry_ref, *, tiles_per_seq):
    h = h_ref[...]
    g = _dot(h, wg_ref[...])
    u = _dot(h, wu_ref[...])
    _conv_gate(g, u, pl.program_id(1) % tiles_per_seq == 0, cw_ref, cb_ref, act_ref, tail_ref, carry_ref)


def _ffn_decode_kernel(pt_ref, h_ref, wg_ref, wu_ref, cw_ref, cb_ref, lam_ref, q_ref, kn_ref, vn_ref, g_ref,
                       *refs, n_pages, tiles_m, tiles_per_seq, out_scale):
    del pt_ref
    k_refs, v_refs = refs[:n_pages], refs[n_pages:2 * n_pages]
    act_ref, tail_ref, o_ref, carry_ref = refs[2 * n_pages:]
    first_tile = (pl.program_id(0) % tiles_m) % tiles_per_seq == 0
    qm, own_head = _decode_query(q_ref)
    s = [_page_scores(qm, own_head, k_ref) for k_ref in k_refs]
    h = h_ref[...]
    g = _dot(h, wg_ref[...])
    u = _dot(h, wu_ref[...])
    p, p_new, l = _decode_weights(qm, s, kn_ref)
    values = [_page_values(pj, v_ref) for pj, v_ref in zip(p, v_refs)]
    o_ref[0] = _decode_finish(values, p_new, l, vn_ref, lam_ref, g_ref, out_scale)
    _conv_gate(g, u, first_tile, cw_ref, cb_ref, act_ref, tail_ref, carry_ref)


def ffn_seq_decode_tiles(M, S, F, DB, tm_pref=256, tn_pref=1408):
    tm, tn = _tile(S, tm_pref), _tile(F, tn_pref)
    return (tm, tn) if tn % HEAD_W == 0 and (M // tm) * (F // tn) == DB else None


def ffn_seq_decode(h, w_gate, w_up, conv_w, conv_b, B, S, tiles,
                   q, k_new, v_new, cache_k, cache_v, page_table, lam, att_gain, out_scale):
    M, D = h.shape
    F = w_gate.shape[1]
    DB, n_pages = page_table.shape
    tm, tn = tiles
    tiles_m, tiles_per_seq = M // tm, S // tm
    row = lambda i: i % tiles_m
    col = lambda i: i // tiles_m
    wspec = pl.BlockSpec((D, tn), lambda i, pt: (0, col(i)), pipeline_mode=pl.Buffered(1))
    tok = pl.BlockSpec((1, H_ATT, HEAD_W), lambda i, pt: (i, 0, 0))
    vec = pl.BlockSpec((1, HEAD_W), lambda i, pt: (0, 0))
    page = lambda j: pl.BlockSpec((1, PAGE_SIZE, H_ATT, HEAD_W),
                                  lambda i, pt: (pt[i * n_pages + j], 0, 0, 0))
    pages = [page(j) for j in range(n_pages)]
    return pl.pallas_call(
        functools.partial(_ffn_decode_kernel, n_pages=n_pages, tiles_m=tiles_m,
                          tiles_per_seq=tiles_per_seq, out_scale=out_scale),
        grid_spec=pltpu.PrefetchScalarGridSpec(
            num_scalar_prefetch=1,
            grid=(DB,),
            in_specs=[pl.BlockSpec((tm, D), lambda i, pt: (row(i), 0)), wspec, wspec,
                      pl.BlockSpec((CONV_W, tn), lambda i, pt: (0, col(i))),
                      pl.BlockSpec((1, tn), lambda i, pt: (0, col(i))),
                      vec, tok, tok, tok, vec] + pages + pages,
            out_specs=(pl.BlockSpec((tm, tn), lambda i, pt: (row(i), col(i))),
                       pl.BlockSpec((1, CONV_W - 1, tn), lambda i, pt: (row(i) // tiles_per_seq, 0, col(i))),
                       tok),
            scratch_shapes=[pltpu.VMEM((8, tn), F32)],
        ),
        out_shape=(jax.ShapeDtypeStruct((M, F), BF16), jax.ShapeDtypeStruct((B, CONV_W - 1, F), F32),
                   jax.ShapeDtypeStruct((DB, H_ATT, HEAD_W), F32)),
        compiler_params=_cparams("arbitrary"),
    )(page_table.reshape(-1), h, w_gate, w_up, conv_w, conv_b, lam, q, k_new, v_new, att_gain,
      *([cache_k] * n_pages), *([cache_v] * n_pages))


def ffn_seq(h, w_gate, w_up, conv_w, conv_b, B, S, tm_pref=512, tn_pref=1408):
    M, D = h.shape
    F = w_gate.shape[1]
    tm, tn = _tile(S, tm_pref), _tile(F, tn_pref)
    tiles_per_seq = S // tm
    wspec = pl.BlockSpec((D, tn), lambda n, m: (0, n))
    return pl.pallas_call(
        functools.partial(_ffn_seq_kernel, tiles_per_seq=tiles_per_seq),
        grid=(F // tn, M // tm),
        in_specs=[pl.BlockSpec((tm, D), lambda n, m: (m, 0)), wspec, wspec,
                  pl.BlockSpec((CONV_W, tn), lambda n, m: (0, n)),
                  pl.BlockSpec((1, tn), lambda n, m: (0, n))],
        out_specs=(pl.BlockSpec((tm, tn), lambda n, m: (m, n)),
                   pl.BlockSpec((1, CONV_W - 1, tn), lambda n, m: (m // tiles_per_seq, 0, n))),
        out_shape=(jax.ShapeDtypeStruct((M, F), BF16), jax.ShapeDtypeStruct((B, CONV_W - 1, F), F32)),
        scratch_shapes=[pltpu.VMEM((8, tn), F32)],
        compiler_params=_cparams("parallel", "arbitrary"),
    )(h, w_gate, w_up, conv_w, conv_b)


def _ffn_step_kernel(h_ref, wg_ref, wu_ref, cw_ref, cb_ref, p2_ref, p1_ref, act_ref, g_ref):
    h = h_ref[...]
    g = _dot(h, wg_ref[...])
    u = _dot(h, wu_ref[...])
    cw = cw_ref[...]
    c = cb_ref[...] + p2_ref[...] * cw[0:1] + p1_ref[...] * cw[1:2] + g * cw[2:3]
    act_ref[...] = (c * jax.nn.sigmoid(c) * u).astype(BF16)
    g_ref[...] = g


def ffn_step(h, w_gate, w_up, conv_w, conv_b, prev2, prev1, tn_pref=1408):
    M, D = h.shape
    F = w_gate.shape[1]
    tn = _tile(F, tn_pref)
    wspec = pl.BlockSpec((D, tn), lambda n: (0, n))
    tile = pl.BlockSpec((M, tn), lambda n: (0, n))
    return pl.pallas_call(
        _ffn_step_kernel,
        grid=(F // tn,),
        in_specs=[pl.BlockSpec((M, D), lambda n: (0, 0)), wspec, wspec,
                  pl.BlockSpec((CONV_W, tn), lambda n: (0, n)),
                  pl.BlockSpec((1, tn), lambda n: (0, n)), tile, tile],
        out_specs=(tile, tile),
        out_shape=(jax.ShapeDtypeStruct((M, F), BF16), jax.ShapeDtypeStruct((M, F), F32)),
        compiler_params=_cparams("parallel"),
    )(h, w_gate, w_up, conv_w, conv_b, prev2, prev1)


def _down_kernel(a_ref, w_ref, x_ref, o_ref, wb_ref=None):
    o_ref[...] = x_ref[...] + _dot(a_ref[...], _rounded(w_ref, wb_ref))


def ffn_down(act, w_down, x, tm_pref=1024, tn_pref=512):
    M, F = act.shape
    D = w_down.shape[1]
    tn_pref = tn_pref * BF16.dtype.itemsize // w_down.dtype.itemsize
    tm, tn = _tile(M, tm_pref), _tile(D, tn_pref)
    tile = pl.BlockSpec((tm, tn), lambda i, j: (i, j))
    wmap = lambda i, j: (0, j)
    wb_spec, wb_shape = _weight_specs(w_down, (F, tn), wmap, M // tm)
    out = pl.pallas_call(
        _down_kernel,
        grid=(M // tm, D // tn),
        in_specs=[pl.BlockSpec((tm, F), lambda i, j: (i, 0)),
                  pl.BlockSpec((F, tn), wmap), tile],
        out_specs=(tile,) + wb_spec,
        out_shape=(jax.ShapeDtypeStruct((M, D), F32),) + wb_shape,
        compiler_params=_cparams("parallel", "arbitrary"),
    )(act, w_down, x)
    return out if wb_shape else out[0]


def _ple_kernel(x_ref, g_ref, wg_ref, p_ref, wp_ref, o_ref, h_ref, *, tn):
    j = pl.program_id(1)

    @pl.when(j == 0)
    def _():
        h_ref[...] = _rms(x_ref[...], g_ref[...]).astype(BF16)

    gate = jax.nn.sigmoid(_dot(h_ref[...], wg_ref[...]))
    emb = _dot(p_ref[...].astype(BF16), wp_ref[...])
    o_ref[...] = x_ref[:, pl.ds(pl.multiple_of(j * tn, tn), tn)] + gate * emb


def ple(x, gain, w_gate, p, w_proj, tm_pref=512, tn_pref=2048):
    M, D = x.shape
    DP = p.shape[1]
    tm, tn = _tile(M, tm_pref), _tile(D, tn_pref)
    return pl.pallas_call(
        functools.partial(_ple_kernel, tn=tn),
        grid=(M // tm, D // tn),
        in_specs=[pl.BlockSpec((tm, D), lambda i, j: (i, 0)),
                  pl.BlockSpec((1, D), lambda i, j: (0, 0)),
                  pl.BlockSpec((D, tn), lambda i, j: (0, j)),
                  pl.BlockSpec((tm, DP), lambda i, j: (i, 0)),
                  pl.BlockSpec((DP, tn), lambda i, j: (0, j))],
        out_specs=pl.BlockSpec((tm, tn), lambda i, j: (i, j)),
        out_shape=jax.ShapeDtypeStruct((M, D), F32),
        scratch_shapes=[pltpu.VMEM((tm, D), BF16)],
        compiler_params=_cparams("parallel", "arbitrary"),
    )(x, gain, w_gate, p, w_proj)


def _rope_tables(pos):
    half = DK_ATT // 2
    inv_freq = 1.0 / (ROPE_THETA ** (jnp.arange(half, dtype=F32) * (2.0 / DK_ATT)))
    ang = pos.astype(F32)[:, None] * inv_freq[None, :]
    cos, sin = jnp.cos(ang), jnp.sin(ang)
    return (jnp.concatenate([cos] * 4, axis=-1), jnp.concatenate([-sin, sin] * 2, axis=-1))


def _row(v):
    return v.reshape(1, -1).astype(F32)


def kernel(x_prompt, x_sample, p_prompt, p_sample, cache_k, cache_v, state_rnn, state_ffn_conv, page_table,
           norm_mix, w_in, q_norm, k_norm, lam_q1, lam_k1, lam_q2, lam_k2, att_out_norm, rnn_out_norm,
           lower_bounds, w_out, norm_ffn, w_gate, w_up, conv_w, conv_b, w_down, norm_ple, w_ple_gate,
           w_ple_proj):
    B, S, D = x_prompt.shape
    DB, T, _ = x_sample.shape
    assert T == 1, "the sample group is implemented for one new token per sequence"
    depth = w_in.shape[0]
    past = page_table.shape[1] * PAGE_SIZE
    cos_p, sin_p = _rope_tables(jnp.arange(S, dtype=jnp.int32))
    cos_p, sin_p = jnp.tile(cos_p, (B, 1)), jnp.tile(sin_p, (B, 1))
    cos_s, sin_s = _rope_tables(jnp.full((DB,), past, dtype=jnp.int32))

    yp = x_prompt.reshape(B * S, D)
    ys = x_sample.reshape(DB, D)
    outs = [[] for _ in range(8)]
    for i in range(depth):
        lam_init = 0.8 - 0.6 * math.exp(-0.3 * i)
        out_scale = 1.0 - lam_init
        lam, lb = layer_params(jnp.stack([lam_q1[i], lam_k1[i], lam_q2[i], lam_k2[i]]).astype(F32),
                               lower_bounds.astype(F32), i, lam_init)
        w_out_b, w_gate_b, w_up_b = w_out[i].astype(BF16), w_gate[i].astype(BF16), w_up[i].astype(BF16)
        w_pg_b, w_pp_b = w_ple_gate[i].astype(BF16), w_ple_proj[i].astype(BF16)
        q_gain = jnp.tile(_row(q_norm[i]), (1, 2))
        k_gain = jnp.tile(_row(k_norm[i]), (1, 2))
        att_gain, rnn_gain = _row(att_out_norm[i]), _row(rnn_out_norm[i])

        def mix_out(x, o_att, o_rnn):
            return out_proj(o_att, o_rnn, w_out_b, x, _row(norm_ffn[i]))

        def embed(x, p):
            return ple(x, _row(norm_ple[i]), w_pg_b, p, w_pp_b)

        z_s, w_in_b = norm_matmul(ys, _row(norm_mix[i]), w_in[i])
        q_s, k_s, _, v_s, _ = qk_prep(z_s, cos_s, sin_s, q_gain, k_gain)

        z = norm_matmul(yp, _row(norm_mix[i]), w_in_b)
        qt, k, kb, v, vt = qk_prep(z, cos_p, sin_p, q_gain, k_gain, (B, S))
        o_att = attn_prompt(qt, kb.reshape(B, S, -1), vt, lam, att_gain, out_scale).reshape(B * S, -1)
        o_rnn, rnn_p = hgrn_prompt(z, lb, rnn_gain, B, S)
        x1, h1 = mix_out(yp, o_att, o_rnn)
        outs[0].append(k.reshape(B, S // PAGE_SIZE, PAGE_SIZE, H_ATT, HEAD_W))
        outs[1].append(v.reshape(B, S // PAGE_SIZE, PAGE_SIZE, H_ATT, HEAD_W))
        outs[2].append(rnn_p.astype(state_rnn.dtype))

        per_head = lambda a: a.reshape(DB, H_ATT, HEAD_W)
        decode_args = (per_head(q_s), per_head(k_s), per_head(v_s), cache_k[i], cache_v[i], page_table,
                       lam, att_gain, out_scale)
        ffn_args = (h1, w_gate_b, w_up_b, conv_w[i], _row(conv_b[i]), B, S)
        tiles = ffn_seq_decode_tiles(B * S, S, w_gate_b.shape[1], DB)
        if tiles is not None:
            act, conv_p, o_att_s = ffn_seq_decode(*ffn_args, tiles, *decode_args)
        else:
            act, conv_p = ffn_seq(*ffn_args)
            o_att_s = attn_sample(*decode_args)
        outs[3].append(conv_p)
        o_rnn_s, rnn_s = hgrn_step(z_s, lb, rnn_gain, state_rnn[i].astype(F32))
        x1_s, h1_s = mix_out(ys, o_att_s.reshape(DB, D_HEADS).astype(BF16), o_rnn_s)
        buf = state_ffn_conv[i]
        act_s, g_new = ffn_step(h1_s, w_gate_b, w_up_b, conv_w[i], _row(conv_b[i]), buf[:, 0], buf[:, 1])
        x2_s, w_down_b = ffn_down(act_s, w_down[i], x1_s)
        ys = embed(x2_s, p_sample[i].reshape(DB, -1))
        yp = embed(ffn_down(act, w_down_b, x1), p_prompt[i].reshape(B * S, -1))
        outs[4].append(k_s.reshape(DB, T, H_ATT, HEAD_W))
        outs[5].append(v_s.reshape(DB, T, H_ATT, HEAD_W))
        outs[6].append(rnn_s.astype(state_rnn.dtype))
        outs[7].append(jnp.stack([buf[:, 1], g_new], axis=1))

    return (yp.reshape(B, S, D), ys.reshape(DB, T, D)) + tuple(jnp.stack(o) for o in outs)
```
